```python
import jax, jax.numpy as jnp
from jax import lax
import numpy as np

D_MODEL = 2048
BATCH = 32
SEQ = 256
DEPTH = 4
DEC_BATCH = 2
DEC_SEQ = 1024
PAST_LEN = 512

GRID_W = 64
N_MIXERS = 2
N_POOL_LAYERS = (DEPTH + 1) // 2
N_MLA_LAYERS = DEPTH // 2
POOL_WINDOWS = (2, 4, 8, 16)
N_POOL_GROUPS = 4
POOL_CH = D_MODEL // N_POOL_GROUPS
N_HEADS = 16
Q_RANK = 512
KV_RANK = 512
NOPE_DIM = 128
ROPE_DIM = 64
V_DIM = 128
ROPE_QUARTER = ROPE_DIM // 4
ROPE_THETA = 10000.0
ATTN_SCALE = (NOPE_DIM + ROPE_DIM) ** -0.5
Q_BLOCK = 128
N_EXPERT_GROUPS = 4
EXPERTS_PER_GROUP = 8
N_EXPERTS = N_EXPERT_GROUPS * EXPERTS_PER_GROUP
TOP_K_IN_GROUP = 2
EXPERT_HIDDEN = 512
N_MOD = 6
EPS = 1e-6

kernel_name = 'hybrid_pool_mla_hmoe_diffusion_step'


def rmsnorm(x, g):
    xf = x.astype(jnp.float32)
    y = xf * lax.rsqrt(jnp.mean(xf * xf, axis=-1, keepdims=True) + EPS)
    return (y * g.astype(jnp.float32)).astype(x.dtype)


def ada_mod(cond, w_ada, b_ada):
    m = jax.nn.silu(cond) @ w_ada + b_ada
    m = m.reshape(cond.shape[0], 1, N_MOD, D_MODEL)
    return [m[:, :, j] for j in range(N_MOD)]


def modulate(h, shift, scale):
    return h * (1.0 + scale) + shift


def grid_rope_tables(n_lat):
    rows = n_lat // GRID_W
    row_ids = jnp.repeat(jnp.arange(rows), GRID_W).astype(jnp.float32)
    col_ids = jnp.tile(jnp.arange(GRID_W), rows).astype(jnp.float32)
    freqs = ROPE_THETA ** (-jnp.arange(ROPE_QUARTER, dtype=jnp.float32) / ROPE_QUARTER)
    ang = jnp.stack([row_ids[:, None] * freqs[None, :], col_ids[:, None] * freqs[None, :]], axis=1)
    return jnp.cos(ang), jnp.sin(ang)


def rope_2d(x, cos, sin):
    shp = x.shape
    xr = x.astype(jnp.float32).reshape(*shp[:-1], 2, 2, ROPE_QUARTER)
    x1, x2 = xr[..., 0, :], xr[..., 1, :]
    extra = x.ndim - 3
    c = cos.reshape(cos.shape[0], *([1] * extra), 2, ROPE_QUARTER)
    s = sin.reshape(sin.shape[0], *([1] * extra), 2, ROPE_QUARTER)
    out = jnp.stack([x1 * c - x2 * s, x1 * s + x2 * c], axis=-2)
    return out.reshape(shp).astype(x.dtype)


def pool_mix(h, w_pool, pool_scale):
    b, t, d = h.shape
    hf = h.astype(jnp.float32)
    cs = jnp.concatenate([jnp.zeros((b, 1, d), jnp.float32), jnp.cumsum(hf, axis=1)], axis=1)
    pos = jnp.arange(t)
    means = []
    for g, w in enumerate(POOL_WINDOWS):
        lo = jnp.clip(pos - w // 2, 0, t)
        hi = jnp.clip(pos + (w - w // 2), 0, t)
        csg = cs[:, :, g * POOL_CH:(g + 1) * POOL_CH]
        cnt = (hi - lo).astype(jnp.float32)[None, :, None]
        means.append((csg[:, hi] - csg[:, lo]) / cnt)
    diff = (jnp.concatenate(means, axis=-1) - hf).astype(h.dtype).reshape(b, t, N_POOL_GROUPS, POOL_CH)
    out = jnp.einsum('btgc,gcd->btgd', diff, w_pool).reshape(b, t, d)
    return out * pool_scale


def mla_queries(h, w_dq, g_q, w_uq):
    cq = rmsnorm(h @ w_dq, g_q)
    q = jnp.einsum('btr,rhd->bthd', cq, w_uq)
    return q[..., :NOPE_DIM], q[..., NOPE_DIM:]


def mla_kv_latent(h, w_dkv, g_kv):
    kv = h @ w_dkv
    return rmsnorm(kv[..., :KV_RANK], g_kv), kv[..., KV_RANK:]


def mla_expand(ckv, w_uk, w_uv):
    k_nope = jnp.einsum('btr,rhd->bthd', ckv, w_uk)
    v = jnp.einsum('btr,rhd->bthd', ckv, w_uv)
    return k_nope, v


def mla_attend(q_nope, q_rope, k_nope, k_rope, v):
    b, tq, h, _ = q_nope.shape
    nb = tq // Q_BLOCK
    qn = q_nope.reshape(b, nb, Q_BLOCK, h, NOPE_DIM).transpose(1, 0, 2, 3, 4)
    qr = q_rope.reshape(b, nb, Q_BLOCK, h, ROPE_DIM).transpose(1, 0, 2, 3, 4)

    def one_block(args):
        qn_b, qr_b = args
        s = (jnp.einsum('bqhd,bkhd->bhqk', qn_b, k_nope, preferred_element_type=jnp.float32)
             + jnp.einsum('bqhr,bkr->bhqk', qr_b, k_rope, preferred_element_type=jnp.float32)) * ATTN_SCALE
        p = jax.nn.softmax(s, axis=-1).astype(v.dtype)
        return jnp.einsum('bhqk,bkhv->bqhv', p, v)

    o = lax.map(one_block, (qn, qr))
    return o.transpose(1, 0, 2, 3, 4).reshape(b, tq, h * V_DIM)


def mla_context(h, w_dq, g_q, w_uq, w_dkv, g_kv, w_uk, w_uv, w_o):
    qn, qr = mla_queries(h, w_dq, g_q, w_uq)
    ckv, kr = mla_kv_latent(h, w_dkv, g_kv)
    kn, v = mla_expand(ckv, w_uk, w_uv)
    return mla_attend(qn, qr, kn, kr, v) @ w_o, ckv, kr


def mla_latent(h, ckv_ctx, kr_ctx, cos, sin, w_dq, g_q, w_uq, w_dkv, g_kv, w_uk, w_uv, w_o):
    qn, qr = mla_queries(h, w_dq, g_q, w_uq)
    qr = rope_2d(qr, cos, sin)
    ckv, kr = mla_kv_latent(h, w_dkv, g_kv)
    kr = rope_2d(kr, cos, sin)
    ckv_all = jnp.concatenate([ckv, ckv_ctx.astype(ckv.dtype)], axis=1)
    kr_all = jnp.concatenate([kr, kr_ctx.astype(kr.dtype)], axis=1)
    kn, v = mla_expand(ckv_all, w_uk, w_uv)
    return mla_attend(qn, qr, kn, kr_all, v) @ w_o


def hier_moe(h, w_rg, b_rg, w_re, b_re, w_gate, w_up, w_down):
    b, t, d = h.shape
    x = h.reshape(b * t, d)
    n = x.shape[0]
    g_logits = jnp.dot(x, w_rg, preferred_element_type=jnp.float32) + b_rg.astype(jnp.float32)
    p_grp = jax.nn.softmax(g_logits, axis=-1)
    grp = jnp.argmax(g_logits, axis=-1)
    e_logits = (jnp.dot(x, w_re, preferred_element_type=jnp.float32) + b_re.astype(jnp.float32)).reshape(n, N_EXPERT_GROUPS, EXPERTS_PER_GROUP)
    e_sel = jnp.take_along_axis(e_logits, grp[:, None, None], axis=1)[:, 0]
    top_logit, top_idx = lax.top_k(e_sel, TOP_K_IN_GROUP)
    p_top = jax.nn.softmax(top_logit, axis=-1) * jnp.take_along_axis(p_grp, grp[:, None], axis=1)
    expert_id = grp[:, None] * EXPERTS_PER_GROUP + top_idx
    comb = jnp.sum(jax.nn.one_hot(expert_id, N_EXPERTS, dtype=jnp.float32) * p_top[..., None], axis=1).astype(h.dtype)
    hid = jax.nn.silu(jnp.einsum('nd,edf->nef', x, w_gate)) * jnp.einsum('nd,edf->nef', x, w_up)
    out = jnp.einsum('nef,efd->nd', hid * comb[:, :, None], w_down)
    return out.reshape(b, t, d)


def setup_inputs(seed: int = 0) -> dict:
    key = jax.random.key(seed)
    ks = jax.random.split(key, 32)

    def nrm(k, shape, scale):
        return jax.random.normal(k, shape, jnp.float32) * scale

    D = D_MODEL
    return {
        'x_prompt': nrm(ks[0], (BATCH, SEQ, D), 1.0),
        'x_sample': nrm(ks[1], (DEC_BATCH, DEC_SEQ, D), 1.0),
        'cache_ckv': nrm(ks[2], (DEC_BATCH, N_MLA_LAYERS, PAST_LEN, KV_RANK), 1.0),
        'cache_krope': nrm(ks[3], (DEC_BATCH, N_MLA_LAYERS, PAST_LEN, ROPE_DIM), 1.0),
        'c': nrm(ks[4], (DEC_BATCH, D), 1.0),
        'c_ctx': nrm(ks[5], (D,), 1.0),
        'g_mix': 1.0 + nrm(ks[6], (DEPTH, D), 0.02),
        'g_ffn': 1.0 + nrm(ks[7], (DEPTH, D), 0.02),
        'w_ada': nrm(ks[8], (DEPTH, D, N_MOD * D), 0.5 * D ** -0.5),
        'b_ada': nrm(ks[9], (DEPTH, N_MOD * D), 0.02),
        'w_pool': nrm(ks[10], (N_POOL_LAYERS, N_POOL_GROUPS, POOL_CH, POOL_CH), POOL_CH ** -0.5),
        'pool_scale': 0.5 + nrm(ks[11], (N_POOL_LAYERS, D), 0.05),
        'w_dq': nrm(ks[12], (N_MLA_LAYERS, D, Q_RANK), D ** -0.5),
        'g_q': 1.0 + nrm(ks[13], (N_MLA_LAYERS, Q_RANK), 0.02),
        'w_uq': nrm(ks[14], (N_MLA_LAYERS, Q_RANK, N_HEADS, NOPE_DIM + ROPE_DIM), Q_RANK ** -0.5),
        'w_dkv': nrm(ks[15], (N_MLA_LAYERS, D, KV_RANK + ROPE_DIM), D ** -0.5),
        'g_kv': 1.0 + nrm(ks[16], (N_MLA_LAYERS, KV_RANK), 0.02),
        'w_uk': nrm(ks[17], (N_MLA_LAYERS, KV_RANK, N_HEADS, NOPE_DIM), KV_RANK ** -0.5),
        'w_uv': nrm(ks[18], (N_MLA_LAYERS, KV_RANK, N_HEADS, V_DIM), KV_RANK ** -0.5),
        'w_o': nrm(ks[19], (N_MLA_LAYERS, N_HEADS * V_DIM, D), (N_HEADS * V_DIM) ** -0.5),
        'w_router_grp': nrm(ks[20], (DEPTH, D, N_EXPERT_GROUPS), D ** -0.5),
        'b_router_grp': nrm(ks[21], (DEPTH, N_EXPERT_GROUPS), 0.01),
        'w_router_exp': nrm(ks[22], (DEPTH, D, N_EXPERTS), D ** -0.5),
        'b_router_exp': nrm(ks[23], (DEPTH, N_EXPERTS), 0.01),
        'w_gate': nrm(ks[24], (DEPTH, N_EXPERTS, D, EXPERT_HIDDEN), D ** -0.5),
        'w_up': nrm(ks[25], (DEPTH, N_EXPERTS, D, EXPERT_HIDDEN), D ** -0.5),
        'w_down': nrm(ks[26], (DEPTH, N_EXPERTS, EXPERT_HIDDEN, D), EXPERT_HIDDEN ** -0.5),
        'g_final': 1.0 + nrm(ks[27], (D,), 0.02),
    }


def reference(x_prompt, x_sample, cache_ckv, cache_krope, c, c_ctx, g_mix, g_ffn, w_ada, b_ada,
              w_pool, pool_scale, w_dq, g_q, w_uq, w_dkv, g_kv, w_uk, w_uv, w_o,
              w_router_grp, b_router_grp, w_router_exp, b_router_exp, w_gate, w_up, w_down, g_final):
    xc = x_prompt
    xl = x_sample
    cos, sin = grid_rope_tables(x_sample.shape[1])
    ckv_list = []
    kr_list = []
    for i in range(DEPTH):
        mc = ada_mod(c_ctx[None, :].astype(xc.dtype), w_ada[i], b_ada[i])
        ml = ada_mod(c, w_ada[i], b_ada[i])
        hc = modulate(rmsnorm(xc, g_mix[i]), mc[0], mc[1])
        hl = modulate(rmsnorm(xl, g_mix[i]), ml[0], ml[1])
        if i % N_MIXERS == 0:
            p = i // N_MIXERS
            oc = pool_mix(hc, w_pool[p], pool_scale[p])
            ol = pool_mix(hl, w_pool[p], pool_scale[p])
        else:
            m = i // N_MIXERS
            oc, ckv_c, kr_c = mla_context(hc, w_dq[m], g_q[m], w_uq[m], w_dkv[m], g_kv[m], w_uk[m], w_uv[m], w_o[m])
            ckv_list.append(ckv_c)
            kr_list.append(kr_c)
            ol = mla_latent(hl, cache_ckv[:, m], cache_krope[:, m], cos, sin,
                            w_dq[m], g_q[m], w_uq[m], w_dkv[m], g_kv[m], w_uk[m], w_uv[m], w_o[m])
        xc = xc + mc[2] * oc
        xl = xl + ml[2] * ol
        hc = modulate(rmsnorm(xc, g_ffn[i]), mc[3], mc[4])
        hl = modulate(rmsnorm(xl, g_ffn[i]), ml[3], ml[4])
        xc = xc + mc[5] * hier_moe(hc, w_router_grp[i], b_router_grp[i], w_router_exp[i], b_router_exp[i], w_gate[i], w_up[i], w_down[i])
        xl = xl + ml[5] * hier_moe(hl, w_router_grp[i], b_router_grp[i], w_router_exp[i], b_router_exp[i], w_gate[i], w_up[i], w_down[i])
    y_prompt = rmsnorm(xc, g_final)
    y_sample = rmsnorm(xl, g_final)
    ctx_ckv = jnp.stack(ckv_list, axis=1)
    ctx_krope = jnp.stack(kr_list, axis=1)
    return (y_prompt, y_sample, ctx_ckv, ctx_krope)
```

```python
import functools

import numpy as np
import jax
import jax.numpy as jnp
from jax import lax
from jax.experimental import pallas as pl
from jax.experimental.pallas import tpu as pltpu

D_MODEL = 2048
BATCH = 32
SEQ = 256
DEPTH = 4
DEC_BATCH = 2
DEC_SEQ = 1024
PAST_LEN = 512
GRID_W = 64
N_MIXERS = 2
POOL_WINDOWS = (2, 4, 8, 16)
N_POOL_GROUPS = 4
POOL_CH = D_MODEL // N_POOL_GROUPS
N_HEADS = 16
Q_RANK = 512
KV_RANK = 512
NOPE_DIM = 128
ROPE_DIM = 64
V_DIM = 128
ROPE_QUARTER = ROPE_DIM // 4
ROPE_THETA = 10000.0
ATTN_SCALE = (NOPE_DIM + ROPE_DIM) ** -0.5
N_EXPERT_GROUPS = 4
EXPERTS_PER_GROUP = 8
N_EXPERTS = N_EXPERT_GROUPS * EXPERTS_PER_GROUP
EXPERT_HIDDEN = 512
N_MOD = 6
EPS = 1e-6

LANES = 128
SUBLANES = 8
TM = 256
N_CTX = BATCH * SEQ
N_LAT = DEC_BATCH * DEC_SEQ
N_TOK = N_CTX + N_LAT
CTX_TILES = N_CTX // TM
LAT_TILES = N_LAT // TM
N_TILES = CTX_TILES + LAT_TILES
LAT_TILES_PER_SEQ = DEC_SEQ // TM
COND_ROWS = SUBLANES
POOL_HALO = max(POOL_WINDOWS) // 2
POOL_K = TM + LANES
ROUTE_LANES = LANES
N_SLOTS = 2 * N_TOK
GMM_TILES = N_SLOTS // TM + N_EXPERTS + 2
Y_ROWS = N_SLOTS + 2 * TM
MIB = 1024 * 1024

F32 = jnp.float32
BF16 = jnp.bfloat16


def _mod_row(tile):
    return jnp.where(tile < CTX_TILES, 0, 1 + jnp.maximum(tile - CTX_TILES, 0) // LAT_TILES_PER_SEQ)


def _params(vmem_mib, n_axes=1):
    return pltpu.CompilerParams(
        dimension_semantics=("arbitrary",) * n_axes,
        vmem_limit_bytes=vmem_mib * MIB,
    )


def _norm_mod(x, g, shift, scale):
    y = x * lax.rsqrt(jnp.mean(x * x, axis=-1, keepdims=True) + EPS)
    return (y * g) * (1.0 + scale) + shift


def _route(h_bf, wr_ref, br_ref):
    logits = jnp.dot(h_bf, wr_ref[...], preferred_element_type=F32) + br_ref[...]
    lane_i = lax.broadcasted_iota(jnp.int32, logits.shape, 1)
    lane = lane_i.astype(F32)
    neg = -jnp.inf
    far = float(ROUTE_LANES)
    gmask = (lane_i >= N_EXPERTS) & (lane_i < N_EXPERTS + N_EXPERT_GROUPS)
    gl = jnp.where(gmask, logits, neg)
    gmax = jnp.max(gl, axis=1, keepdims=True)
    grp = jnp.min(jnp.where(gl == gmax, lane, far), axis=1, keepdims=True) - N_EXPERTS
    p_grp = 1.0 / jnp.sum(jnp.exp(gl - gmax), axis=1, keepdims=True)
    lane_grp = lax.shift_right_logical(lane_i, EXPERTS_PER_GROUP.bit_length() - 1).astype(F32)
    emask = lane_grp == grp
    el = jnp.where(emask, logits, neg)
    m1 = jnp.max(el, axis=1, keepdims=True)
    i1 = jnp.min(jnp.where(el == m1, lane, far), axis=1, keepdims=True)
    el2 = jnp.where(lane == i1, neg, el)
    m2 = jnp.max(el2, axis=1, keepdims=True)
    i2 = jnp.min(jnp.where((el2 == m2) & emask & (lane != i1), lane, far), axis=1, keepdims=True)
    e2 = jnp.exp(m2 - m1)
    p1 = p_grp / (1.0 + e2)
    p2 = p1 * e2
    out = jnp.where(lane == 0, i1.astype(F32), 0.0)
    out = jnp.where(lane == 1, i2.astype(F32), out)
    out = jnp.where(lane == 2, p1, out)
    out = jnp.where(lane == 3, p2, out)
    return out


def _mixer_tail(x, mix_out, mod_ref, gffn_ref, wr_ref, br_ref, xnew_ref, h2_ref, route_ref):
    x_new = x + mod_ref[2:3, :] * mix_out
    xnew_ref[...] = x_new
    h2 = _norm_mod(x_new, gffn_ref[...], mod_ref[3:4, :], mod_ref[4:5, :])
    h2_ref[...] = h2
    route_ref[...] = _route(h2.astype(BF16), wr_ref, br_ref)


ADA_TN = 1024


def _ada_kernel(cond_ref, w_ref, b_ref, o_ref):
    c = cond_ref[...]
    s = c * jax.nn.sigmoid(c)
    o_ref[...] = jnp.dot(s.astype(BF16), w_ref[...].astype(BF16), preferred_element_type=F32) + b_ref[...]


def _ada_call(cond, w_ada, b_ada):
    nj = (N_MOD * D_MODEL) // ADA_TN
    return pl.pallas_call(
        _ada_kernel,
        out_shape=jax.ShapeDtypeStruct((DEPTH, COND_ROWS, N_MOD * D_MODEL), F32),
        grid=(DEPTH, nj),
        in_specs=[
            pl.BlockSpec((COND_ROWS, D_MODEL), lambda l, j: (0, 0)),
            pl.BlockSpec((None, D_MODEL, ADA_TN), lambda l, j: (l, 0, j)),
            pl.BlockSpec((None, 1, ADA_TN), lambda l, j: (l, 0, j)),
        ],
        out_specs=pl.BlockSpec((None, COND_ROWS, ADA_TN), lambda l, j: (l, 0, j)),
        compiler_params=_params(40, 2),
        name="ada_mod",
    )(cond, w_ada, b_ada.reshape(DEPTH, 1, N_MOD * D_MODEL))


def _pool_band():
    col = np.arange(POOL_K)
    rel = np.where(col < TM, col, np.where(col < TM + POOL_HALO, col - TM - POOL_HALO, col - POOL_HALO))
    ok = col < TM + 2 * POOL_HALO
    row = np.arange(TM)[:, None]
    mats = []
    for w in POOL_WINDOWS:
        mats.append((rel[None, :] >= row - w // 2) & (rel[None, :] < row + (w - w // 2)) & ok[None, :])
    return np.stack(mats).astype(np.float32)


def _pool_kernel(x_ref, xp_ref, xn_ref, mod_ref, gmix_ref, gffn_ref, band_ref, wpool_ref, pscale_ref,
                 wr_ref, br_ref, xnew_ref, h2_ref, route_ref):
    i = pl.program_id(0)
    lat = i >= CTX_TILES
    tile_pos = jnp.where(lat, jnp.maximum(i - CTX_TILES, 0) % LAT_TILES_PER_SEQ, 0)
    seq_len = jnp.where(lat, DEC_SEQ, SEQ)
    has_prev = tile_pos > 0
    has_next = (tile_pos + 1) * TM < seq_len

    g = gmix_ref[...]
    shift, scale = mod_ref[0:1, :], mod_ref[1:2, :]
    x = x_ref[...]
    h = _norm_mod(x, g, shift, scale)
    hp = jnp.where(has_prev, _norm_mod(xp_ref[...], g, shift, scale), 0.0)
    hn = jnp.where(has_next, _norm_mod(xn_ref[...], g, shift, scale), 0.0)
    hall = jnp.concatenate(
        [h, hp, hn, jnp.zeros((POOL_K - TM - 2 * POOL_HALO, D_MODEL), F32)], axis=0)
    hi = hall.astype(BF16)
    lo = (hall - hi.astype(F32)).astype(BF16)

    pos = tile_pos * TM + lax.broadcasted_iota(jnp.int32, (TM, 1), 0)
    outs = []
    for gi, w in enumerate(POOL_WINDOWS):
        sl = slice(gi * POOL_CH, (gi + 1) * POOL_CH)
        band = band_ref[gi]
        wsum = (jnp.dot(band, hi[:, sl], preferred_element_type=F32)
                + jnp.dot(band, lo[:, sl], preferred_element_type=F32))
        cnt = jnp.minimum(pos + (w - w // 2), seq_len) - jnp.maximum(pos - w // 2, 0)
        diff = wsum / cnt.astype(F32) - h[:, sl]
        outs.append(jnp.dot(diff.astype(BF16), wpool_ref[gi], preferred_element_type=F32))
    mix = jnp.concatenate(outs, axis=1) * pscale_ref[...]
    _mixer_tail(x, mix, mod_ref, gffn_ref, wr_ref, br_ref, xnew_ref, h2_ref, route_ref)


def _tail_out_shapes():
    return (jax.ShapeDtypeStruct((N_TOK, D_MODEL), F32),
            jax.ShapeDtypeStruct((N_TOK, D_MODEL), F32),
            jax.ShapeDtypeStruct((N_TOK, ROUTE_LANES), F32))


def _tail_out_specs():
    return (pl.BlockSpec((TM, D_MODEL), lambda i: (i, 0)),
            pl.BlockSpec((TM, D_MODEL), lambda i: (i, 0)),
            pl.BlockSpec((TM, ROUTE_LANES), lambda i: (i, 0)))


def _row_spec(width):
    return pl.BlockSpec((1, width), lambda i: (0, 0))


def _mod_spec(layer):
    return pl.BlockSpec((None, None, N_MOD, D_MODEL), lambda i: (layer, _mod_row(i), 0, 0))


def _pool_call(layer, x, mod, g_mix, g_ffn, band, wpool, pscale, wr, br):
    halo_blocks = TM // POOL_HALO
    last_halo = N_TOK // POOL_HALO - 1
    return pl.pallas_call(
        _pool_kernel,
        out_shape=_tail_out_shapes(),
        grid=(N_TILES,),
        in_specs=[
            pl.BlockSpec((TM, D_MODEL), lambda i: (i, 0)),
            pl.BlockSpec((POOL_HALO, D_MODEL), lambda i: (jnp.maximum(i * halo_blocks - 1, 0), 0)),
            pl.BlockSpec((POOL_HALO, D_MODEL), lambda i: (jnp.minimum((i + 1) * halo_blocks, last_halo), 0)),
            _mod_spec(layer),
            _row_spec(D_MODEL),
            _row_spec(D_MODEL),
            pl.BlockSpec((N_POOL_GROUPS, TM, POOL_K), lambda i: (0, 0, 0)),
            pl.BlockSpec((N_POOL_GROUPS, POOL_CH, POOL_CH), lambda i: (0, 0, 0)),
            _row_spec(D_MODEL),
            pl.BlockSpec((D_MODEL, ROUTE_LANES), lambda i: (0, 0)),
            _row_spec(ROUTE_LANES),
        ],
        out_specs=_tail_out_specs(),
        compiler_params=_params(48),
        name=f"pool_layer{layer}",
    )(x, x, x, mod, g_mix, g_ffn, band, wpool, pscale, wr, br)


def _qkv_kernel(x_ref, mod_ref, gmix_ref, wdq_ref, gq_ref, wdkv_ref, gkv_ref, wkr_ref, wuqn_ref, wuqr_ref,
                tabq_ref, tabk_ref, qn_ref, qr_ref, ckv_ref, kr_ref):
    i = pl.program_id(0)
    h = _norm_mod(x_ref[...], gmix_ref[...], mod_ref[0:1, :], mod_ref[1:2, :]).astype(BF16)

    def rms(v, g):
        return v * lax.rsqrt(jnp.mean(v * v, axis=-1, keepdims=True) + EPS) * g

    cq = rms(jnp.dot(h, wdq_ref[...], preferred_element_type=F32), gq_ref[...]).astype(BF16)
    ckv_ref[...] = rms(jnp.dot(h, wdkv_ref[...], preferred_element_type=F32), gkv_ref[...])
    qn_ref[...] = jnp.dot(cq, wuqn_ref[...], preferred_element_type=F32).astype(BF16)
    qr = jnp.dot(cq, wuqr_ref[...], preferred_element_type=F32)
    kr = jnp.dot(h, wkr_ref[...], preferred_element_type=F32)
    lane = lax.broadcasted_iota(jnp.int32, kr.shape, 1)

    @pl.when(i < CTX_TILES)
    def _():
        qr_ref[...] = qr.astype(BF16)
        kr_ref[...] = jnp.where(lane < ROPE_DIM, kr, 0.0)

    @pl.when(i >= CTX_TILES)
    def _():
        tq = qr * tabq_ref[...]
        qr_ref[...] = (tq + pltpu.roll(tq, N_HEADS * LANES - ROPE_DIM, 1)).astype(BF16)
        tk = kr * tabk_ref[...]
        kr_ref[...] = jnp.where(lane < ROPE_DIM, tk + pltpu.roll(tk, ROPE_DIM, 1), 0.0)


def _qkv_call(layer, x, mod, g_mix, wdq, gq, wdkv, gkv, wkr, wuqn, wuqr, tabq, tabk):
    hd = N_HEADS * LANES

    def lat_tile(i):
        return jnp.maximum(i - CTX_TILES, 0) % LAT_TILES_PER_SEQ

    def full(a):
        return pl.BlockSpec(a.shape, lambda i: (0,) * a.ndim)

    return pl.pallas_call(
        _qkv_kernel,
        out_shape=(jax.ShapeDtypeStruct((N_TOK, hd), BF16),
                   jax.ShapeDtypeStruct((N_TOK, hd), BF16),
                   jax.ShapeDtypeStruct((N_TOK, KV_RANK), F32),
                   jax.ShapeDtypeStruct((N_TOK, LANES), F32)),
        grid=(N_TILES,),
        in_specs=[
            pl.BlockSpec((TM, D_MODEL), lambda i: (i, 0)),
            _mod_spec(layer),
            _row_spec(D_MODEL),
            full(wdq), _row_spec(Q_RANK), full(wdkv), _row_spec(KV_RANK), full(wkr), full(wuqn), full(wuqr),
            pl.BlockSpec((TM, hd), lambda i: (lat_tile(i), 0)),
            pl.BlockSpec((TM, LANES), lambda i: (lat_tile(i), 0)),
        ],
        out_specs=(pl.BlockSpec((TM, hd), lambda i: (i, 0)),
                   pl.BlockSpec((TM, hd), lambda i: (i, 0)),
                   pl.BlockSpec((TM, KV_RANK), lambda i: (i, 0)),
                   pl.BlockSpec((TM, LANES), lambda i: (i, 0))),
        compiler_params=_params(48),
        name=f"mla_qkv{layer}",
    )(x, mod, g_mix, wdq, gq, wdkv, gkv, wkr, wuqn, wuqr, tabq, tabk)


HEAD_K = 2 * LANES


def _attn_kernel(*refs, n_new, n_cache):
    if n_cache:
        (qn_ref, qr_ref, ckv_ref, kr_ref, cckv_ref, ckr_ref, wuk_ref, wuv_ref, _, o_ref, kcat, vs) = refs
    else:
        (qn_ref, qr_ref, ckv_ref, kr_ref, wuk_ref, wuv_ref, _, o_ref, kcat, vs) = refs
        cckv_ref = ckr_ref = None

    @pl.when(pl.program_id(1) == 0)
    def _():
        def expand(src_ref, kr_src_ref, n_rows, base):
            for c in range(n_rows // TM):
                rows = slice(c * TM, (c + 1) * TM)
                dst = slice(base + c * TM, base + (c + 1) * TM)
                lat = src_ref[rows, :].astype(BF16)
                kn = jnp.dot(lat, wuk_ref[...], preferred_element_type=F32).astype(BF16)
                vs[dst, :] = jnp.dot(lat, wuv_ref[...], preferred_element_type=F32).astype(BF16)
                krb = kr_src_ref[rows, :].astype(BF16)
                for hh in range(N_HEADS):
                    kcat[dst, hh * HEAD_K:hh * HEAD_K + LANES] = kn[:, hh * LANES:(hh + 1) * LANES]
                    kcat[dst, hh * HEAD_K + LANES:(hh + 1) * HEAD_K] = krb

        expand(ckv_ref, kr_ref, n_new, 0)
        if n_cache:
            expand(cckv_ref, ckr_ref, n_cache, n_new)

    for hh in range(N_HEADS):
        hs = slice(hh * LANES, (hh + 1) * LANES)
        q = jnp.concatenate([qn_ref[:, hs], qr_ref[:, hs]], axis=1)
        s = lax.dot_general(q, kcat[:, hh * HEAD_K:(hh + 1) * HEAD_K], (((1,), (1,)), ((), ())),
                            preferred_element_type=F32) * ATTN_SCALE
        p = jnp.exp(s - jnp.max(s, axis=1, keepdims=True))
        den = jnp.sum(p, axis=1, keepdims=True)
        o = jnp.dot(p.astype(BF16), vs[:, hs], preferred_element_type=F32) / den
        o_ref[:, hs] = o.astype(BF16)


def _attn_call(layer, qn, qr, ckv, kr, wuk, wuv, o_prev, cache=None):
    hd = N_HEADS * LANES
    if cache is None:
        n_b, n_q, n_new, n_cache, tile0 = BATCH, SEQ // TM, SEQ, 0, 0
    else:
        n_b, n_q, n_new, n_cache, tile0 = DEC_BATCH, DEC_SEQ // TM, DEC_SEQ, PAST_LEN, CTX_TILES
    seq_blk0 = tile0 * TM // n_new
    m = layer // N_MIXERS

    def q_map(b, j):
        return (tile0 + b * n_q + j, 0)

    in_specs = [
        pl.BlockSpec((TM, hd), q_map),
        pl.BlockSpec((TM, hd), q_map),
        pl.BlockSpec((n_new, KV_RANK), lambda b, j: (seq_blk0 + b, 0)),
        pl.BlockSpec((n_new, LANES), lambda b, j: (seq_blk0 + b, 0)),
    ]
    args = [qn, qr, ckv, kr]
    if cache is not None:
        in_specs += [
            pl.BlockSpec((None, None, n_cache, KV_RANK), lambda b, j: (b, m, 0, 0)),
            pl.BlockSpec((None, None, n_cache, LANES), lambda b, j: (b, m, 0, 0)),
        ]
        args += list(cache)
    in_specs += [
        pl.BlockSpec((KV_RANK, hd), lambda b, j: (0, 0)),
        pl.BlockSpec((KV_RANK, hd), lambda b, j: (0, 0)),
        pl.BlockSpec(memory_space=pl.ANY),
    ]
    args += [wuk, wuv, o_prev]
    n_keys = n_new + n_cache
    return pl.pallas_call(
        functools.partial(_attn_kernel, n_new=n_new, n_cache=n_cache),
        out_shape=jax.ShapeDtypeStruct((N_TOK, hd), BF16),
        grid=(n_b, n_q),
        in_specs=in_specs,
        out_specs=pl.BlockSpec((TM, hd), q_map),
        scratch_shapes=[pltpu.VMEM((n_keys, N_HEADS * HEAD_K), BF16), pltpu.VMEM((n_keys, hd), BF16)],
        input_output_aliases={len(args) - 1: 0},
        compiler_params=_params(56, 2),
        name=f"mla_attn{layer}_{'lat' if cache is not None else 'ctx'}",
    )(*args)


def _attn_out_kernel(o_ref, wo_ref, x_ref, mod_ref, gffn_ref, wr_ref, br_ref, xnew_ref, h2_ref, route_ref):
    mix = jnp.dot(o_ref[...], wo_ref[...], preferred_element_type=F32)
    _mixer_tail(x_ref[...], mix, mod_ref, gffn_ref, wr_ref, br_ref, xnew_ref, h2_ref, route_ref)


def _attn_out_call(layer, o, wo, x, mod, g_ffn, wr, br):
    hd = N_HEADS * V_DIM
    return pl.pallas_call(
        _attn_out_kernel,
        out_shape=_tail_out_shapes(),
        grid=(N_TILES,),
        in_specs=[
            pl.BlockSpec((TM, hd), lambda i: (i, 0)),
            pl.BlockSpec((hd, D_MODEL), lambda i: (0, 0)),
            pl.BlockSpec((TM, D_MODEL), lambda i: (i, 0)),
            _mod_spec(layer),
            _row_spec(D_MODEL),
            pl.BlockSpec((D_MODEL, ROUTE_LANES), lambda i: (0, 0)),
            _row_spec(ROUTE_LANES),
        ],
        out_specs=_tail_out_specs(),
        compiler_params=_params(48),
        name=f"mla_out{layer}",
    )(o, wo, x, mod, g_ffn, wr, br)


def _gmm_kernel(te_ref, tf_ref, nt_ref, tok_ref, dst_ref, h2_hbm, roww_ref, wg_ref, wu_ref, wd_ref, y_hbm,
                xbuf, ybuf, wg_s, wu_s, wd_s, sem_in, sem_out):
    i = pl.program_id(0)
    nt = nt_ref[0]

    def gather_copy(tile, r):
        slot = tile % 2
        tok = tok_ref[tile * TM + r]
        return pltpu.make_async_copy(h2_hbm.at[pl.ds(tok, 1)], xbuf.at[slot, pl.ds(r, 1)], sem_in.at[slot])

    def scatter_copy(tile, r):
        slot = tile % 2
        dst = dst_ref[tile * TM + r]
        return pltpu.make_async_copy(ybuf.at[slot, pl.ds(r, 1)], y_hbm.at[pl.ds(dst, 1)], sem_out.at[slot])

    def for_rows(fn):
        def body(r, carry):
            fn(r)
            return carry
        lax.fori_loop(0, TM, body, 0, unroll=8)

    @pl.when(i == 0)
    def _():
        for_rows(lambda r: gather_copy(i, r).start())

    @pl.when(i + 1 < nt)
    def _():
        for_rows(lambda r: gather_copy(i + 1, r).start())

    @pl.when((i >= 2) & (i - 2 < nt))
    def _():
        for_rows(lambda r: scatter_copy(i - 2, r).wait())

    @pl.when(i < nt)
    def _():
        @pl.when(tf_ref[i] == 1)
        def _():
            wg_s[...] = wg_ref[...].astype(BF16)
            wu_s[...] = wu_ref[...].astype(BF16)
            wd_s[...] = wd_ref[...].astype(BF16)

        for_rows(lambda r: gather_copy(i, r).wait())
        slot = i % 2
        x = xbuf[slot].astype(BF16)
        gate = jnp.dot(x, wg_s[...], preferred_element_type=F32)
        up = jnp.dot(x, wu_s[...], preferred_element_type=F32)
        hid = (gate * jax.nn.sigmoid(gate)) * up * roww_ref[...]
        ybuf[slot] = jnp.dot(hid.astype(BF16), wd_s[...], preferred_element_type=F32)
        for_rows(lambda r: scatter_copy(i, r).start())


def _gmm_call(layer, h2, meta, w_gate, w_up, w_down):
    tile_expert, tile_first, num_tiles, row_tok, row_dst, row_w = meta

    def w_spec(shape):
        return pl.BlockSpec((None, None) + shape, lambda i, te, tf, nt, tok, dst: (layer, te[i], 0, 0))

    grid_spec = pltpu.PrefetchScalarGridSpec(
        num_scalar_prefetch=5,
        grid=(GMM_TILES,),
        in_specs=[
            pl.BlockSpec(memory_space=pl.ANY),
            pl.BlockSpec((TM, 1), lambda i, *_: (i, 0)),
            w_spec((D_MODEL, EXPERT_HIDDEN)),
            w_spec((D_MODEL, EXPERT_HIDDEN)),
            w_spec((EXPERT_HIDDEN, D_MODEL)),
        ],
        out_specs=pl.BlockSpec(memory_space=pl.ANY),
        scratch_shapes=[
            pltpu.VMEM((2, TM, D_MODEL), F32),
            pltpu.VMEM((2, TM, D_MODEL), F32),
            pltpu.VMEM((D_MODEL, EXPERT_HIDDEN), BF16),
            pltpu.VMEM((D_MODEL, EXPERT_HIDDEN), BF16),
            pltpu.VMEM((EXPERT_HIDDEN, D_MODEL), BF16),
            pltpu.SemaphoreType.DMA((2,)),
            pltpu.SemaphoreType.DMA((2,)),
        ],
    )
    return pl.pallas_call(
        _gmm_kernel,
        out_shape=jax.ShapeDtypeStruct((Y_ROWS, D_MODEL), F32),
        grid_spec=grid_spec,
        compiler_params=_params(56),
        name=f"moe_gmm{layer}",
    )(tile_expert, tile_first, num_tiles, row_tok, row_dst, h2, row_w, w_gate, w_up, w_down)


def _moe_meta(route):
    eid = jnp.concatenate([route[:, 0], route[:, 1]]).astype(jnp.int32)
    wts = jnp.concatenate([route[:, 2], route[:, 3]])
    _, order = lax.sort_key_val(eid, jnp.arange(N_SLOTS, dtype=jnp.int32))
    counts = jnp.sum((eid[:, None] == jnp.arange(N_EXPERTS, dtype=jnp.int32)[None, :]).astype(jnp.int32), axis=0)
    ptiles = (counts + TM - 1) // TM
    pstart = (jnp.cumsum(ptiles) - ptiles) * TM
    ustart = jnp.cumsum(counts) - counts
    num_tiles = jnp.sum(ptiles).reshape(1).astype(jnp.int32)

    p = jnp.arange(GMM_TILES * TM, dtype=jnp.int32)
    e_p = jnp.sum((p[:, None] >= pstart[None, :]).astype(jnp.int32), axis=1) - 1
    off = p - pstart[e_p]
    valid = off < counts[e_p]
    slot = order[jnp.clip(ustart[e_p] + off, 0, N_SLOTS - 1)]
    row_tok = jnp.where(valid, jnp.where(slot >= N_TOK, slot - N_TOK, slot), 0).astype(jnp.int32)
    row_dst = jnp.where(valid, slot, N_SLOTS + p % (2 * TM)).astype(jnp.int32)
    row_w = jnp.where(valid, wts[slot], 0.0).reshape(GMM_TILES * TM, 1)
    tile_expert = e_p[::TM].astype(jnp.int32)
    tile_first = jnp.concatenate(
        [jnp.ones((1,), jnp.int32), (tile_expert[1:] != tile_expert[:-1]).astype(jnp.int32)])
    return tile_expert, tile_first, num_tiles, row_tok, row_dst, row_w


def _combine_kernel(x_ref, ya_ref, yb_ref, mod_ref, gfin_ref, o_ref, *, final):
    x = x_ref[...] + mod_ref[5:6, :] * (ya_ref[...] + yb_ref[...])
    if final:
        x = x * lax.rsqrt(jnp.mean(x * x, axis=-1, keepdims=True) + EPS) * gfin_ref[...]
    o_ref[...] = x


def _combine_call(layer, x_new, y, mod, g_final, final):
    return pl.pallas_call(
        functools.partial(_combine_kernel, final=final),
        out_shape=jax.ShapeDtypeStruct((N_TOK, D_MODEL), F32),
        grid=(N_TILES,),
        in_specs=[
            pl.BlockSpec((TM, D_MODEL), lambda i: (i, 0)),
            pl.BlockSpec((TM, D_MODEL), lambda i: (i, 0)),
            pl.BlockSpec((TM, D_MODEL), lambda i: (N_TILES + i, 0)),
            _mod_spec(layer),
            _row_spec(D_MODEL),
        ],
        out_specs=pl.BlockSpec((TM, D_MODEL), lambda i: (i, 0)),
        compiler_params=_params(40),
        name=f"moe_combine{layer}",
    )(x_new, y, y, mod, g_final)


def _rope_tables():
    rows = DEC_SEQ // GRID_W
    row_ids = jnp.repeat(jnp.arange(rows), GRID_W).astype(F32)
    col_ids = jnp.tile(jnp.arange(GRID_W), rows).astype(F32)
    freqs = ROPE_THETA ** (-jnp.arange(ROPE_QUARTER, dtype=F32) / ROPE_QUARTER)
    ar, ac = row_ids[:, None] * freqs[None, :], col_ids[:, None] * freqs[None, :]
    cos = jnp.concatenate([jnp.cos(ar), jnp.cos(ar), jnp.cos(ac), jnp.cos(ac)], axis=1)
    sin = jnp.concatenate([-jnp.sin(ar), jnp.sin(ar), -jnp.sin(ac), jnp.sin(ac)], axis=1)
    return jnp.concatenate([cos, sin], axis=1)


def _swap_partners(w):
    q = ROPE_QUARTER
    return jnp.concatenate([w[..., q:2 * q], w[..., 0:q], w[..., 3 * q:4 * q], w[..., 2 * q:3 * q]], axis=-1)


def kernel(x_prompt, x_sample, cache_ckv, cache_krope, c, c_ctx, g_mix, g_ffn, w_ada, b_ada, w_pool, pool_scale,
           w_dq, g_q, w_uq, w_dkv, g_kv, w_uk, w_uv, w_o, w_router_grp, b_router_grp, w_router_exp,
           b_router_exp, w_gate, w_up, w_down, g_final):
    n_mla = DEPTH // N_MIXERS
    x = jnp.concatenate([x_prompt.reshape(N_CTX, D_MODEL), x_sample.reshape(N_LAT, D_MODEL)], axis=0)

    cond = jnp.concatenate([c_ctx[None, :], c, jnp.zeros((COND_ROWS - 1 - DEC_BATCH, D_MODEL), F32)], axis=0)
    mod = _ada_call(cond, w_ada, b_ada).reshape(DEPTH, COND_ROWS, N_MOD, D_MODEL)

    wr = jnp.concatenate([w_router_exp, w_router_grp,
                          jnp.zeros((DEPTH, D_MODEL, ROUTE_LANES - N_EXPERTS - N_EXPERT_GROUPS), F32)],
                         axis=2).astype(BF16)
    br = jnp.concatenate([b_router_exp, b_router_grp,
                          jnp.zeros((DEPTH, ROUTE_LANES - N_EXPERTS - N_EXPERT_GROUPS), F32)], axis=1)
    band = jnp.asarray(_pool_band(), dtype=BF16)
    wpool = w_pool.astype(BF16)
    wdq = w_dq.astype(BF16)
    wdkv = w_dkv[:, :, :KV_RANK].astype(BF16)
    w_kr = w_dkv[:, :, KV_RANK:]
    wkr = jnp.concatenate([w_kr, _swap_partners(w_kr)], axis=-1).astype(BF16)
    wuqn = w_uq[..., :NOPE_DIM].reshape(n_mla, Q_RANK, N_HEADS * NOPE_DIM).astype(BF16)
    w_qr = w_uq[..., NOPE_DIM:]
    wuqr = jnp.concatenate([w_qr, _swap_partners(w_qr)], axis=-1).reshape(n_mla, Q_RANK, N_HEADS * LANES)
    wuqr = wuqr.astype(BF16)
    wuk = w_uk.reshape(n_mla, KV_RANK, N_HEADS * NOPE_DIM).astype(BF16)
    wuv = w_uv.reshape(n_mla, KV_RANK, N_HEADS * V_DIM).astype(BF16)
    wo = w_o.astype(BF16)
    tabk = _rope_tables()
    tabq = jnp.tile(tabk, (1, N_HEADS))
    cache_kr = jnp.pad(cache_krope, ((0, 0), (0, 0), (0, 0), (0, LANES - ROPE_DIM)))

    def row(a):
        return a.reshape(1, -1)

    ckv_ctx, kr_ctx = [], []
    for layer in range(DEPTH):
        if layer % N_MIXERS == 0:
            p = layer // N_MIXERS
            x_new, h2, route = _pool_call(layer, x, mod, row(g_mix[layer]), row(g_ffn[layer]), band, wpool[p],
                                          row(pool_scale[p]), wr[layer], row(br[layer]))
        else:
            m = layer // N_MIXERS
            qn, qr, ckv, kr = _qkv_call(layer, x, mod, row(g_mix[layer]), wdq[m], row(g_q[m]), wdkv[m],
                                        row(g_kv[m]), wkr[m], wuqn[m], wuqr[m], tabq, tabk)
            ckv_ctx.append(ckv[:N_CTX].reshape(BATCH, SEQ, KV_RANK))
            kr_ctx.append(kr[:N_CTX, :ROPE_DIM].reshape(BATCH, SEQ, ROPE_DIM))
            o = jnp.zeros((N_TOK, N_HEADS * V_DIM), BF16)
            o = _attn_call(layer, qn, qr, ckv, kr, wuk[m], wuv[m], o)
            o = _attn_call(layer, qn, qr, ckv, kr, wuk[m], wuv[m], o, cache=(cache_ckv, cache_kr))
            x_new, h2, route = _attn_out_call(layer, o, wo[m], x, mod, row(g_ffn[layer]), wr[layer],
                                              row(br[layer]))
        y = _gmm_call(layer, h2, _moe_meta(route), w_gate, w_up, w_down)
        x = _combine_call(layer, x_new, y, mod, row(g_final), final=(layer == DEPTH - 1))

    y_prompt = x[:N_CTX].reshape(BATCH, SEQ, D_MODEL)
    y_sample = x[N_CTX:].reshape(DEC_BATCH, DEC_SEQ, D_MODEL)
    return y_prompt, y_sample, jnp.stack(ckv_ctx, axis=1), jnp.stack(kr_ctx, axis=1)
```

```python
import functools

import numpy as np
import jax
import jax.numpy as jnp
from jax import lax
from jax.experimental import pallas as pl
from jax.experimental.pallas import tpu as pltpu

D_MODEL = 2048
BATCH = 32
SEQ = 256
DEPTH = 4
DEC_BATCH = 2
DEC_SEQ = 1024
PAST_LEN = 512
GRID_W = 64
N_MIXERS = 2
POOL_WINDOWS = (2, 4, 8, 16)
N_POOL_GROUPS = 4
POOL_CH = D_MODEL // N_POOL_GROUPS
N_HEADS = 16
Q_RANK = 512
KV_RANK = 512
NOPE_DIM = 128
ROPE_DIM = 64
V_DIM = 128
ROPE_QUARTER = ROPE_DIM // 4
ROPE_THETA = 10000.0
ATTN_SCALE = (NOPE_DIM + ROPE_DIM) ** -0.5
N_EXPERT_GROUPS = 4
EXPERTS_PER_GROUP = 8
N_EXPERTS = N_EXPERT_GROUPS * EXPERTS_PER_GROUP
EXPERT_HIDDEN = 512
N_MOD = 6
EPS = 1e-6

LANES = 128
SUBLANES = 8
TM = 256
N_CTX = BATCH * SEQ
N_LAT = DEC_BATCH * DEC_SEQ
N_TOK = N_CTX + N_LAT
CTX_TILES = N_CTX // TM
LAT_TILES = N_LAT // TM
N_TILES = CTX_TILES + LAT_TILES
LAT_TILES_PER_SEQ = DEC_SEQ // TM
COND_ROWS = SUBLANES
POOL_HALO = max(POOL_WINDOWS) // 2
POOL_K = TM + LANES
ROUTE_LANES = LANES
N_SLOTS = 2 * N_TOK
SORT_TILES = N_SLOTS // TM
MAX_WORK = SORT_TILES + N_EXPERTS - 1
MIB = 1024 * 1024

F32 = jnp.float32
BF16 = jnp.bfloat16


def _mod_row(tile):
    return jnp.where(tile < CTX_TILES, 0, 1 + jnp.maximum(tile - CTX_TILES, 0) // LAT_TILES_PER_SEQ)


def _params(vmem_mib, n_axes=1):
    return pltpu.CompilerParams(
        dimension_semantics=("arbitrary",) * n_axes,
        vmem_limit_bytes=vmem_mib * MIB,
    )


def _norm_mod(x, g, shift, scale):
    y = x * lax.rsqrt(jnp.mean(x * x, axis=-1, keepdims=True) + EPS)
    return (y * g) * (1.0 + scale) + shift


def _route(h_bf, wr_ref, br_ref, seen_ref):
    logits = jnp.dot(h_bf, wr_ref[...], preferred_element_type=F32) + br_ref[...]
    lane_i = lax.broadcasted_iota(jnp.int32, logits.shape, 1)
    lane = lane_i.astype(F32)
    neg = -jnp.inf
    far = float(ROUTE_LANES)
    gmask = (lane_i >= N_EXPERTS) & (lane_i < N_EXPERTS + N_EXPERT_GROUPS)
    gl = jnp.where(gmask, logits, neg)
    gmax = jnp.max(gl, axis=1, keepdims=True)
    grp = jnp.min(jnp.where(gl == gmax, lane, far), axis=1, keepdims=True) - N_EXPERTS
    p_grp = 1.0 / jnp.sum(jnp.exp(gl - gmax), axis=1, keepdims=True)
    lane_grp = lax.shift_right_logical(lane_i, EXPERTS_PER_GROUP.bit_length() - 1).astype(F32)
    emask = lane_grp == grp
    el = jnp.where(emask, logits, neg)
    m1 = jnp.max(el, axis=1, keepdims=True)
    i1 = jnp.min(jnp.where(el == m1, lane, far), axis=1, keepdims=True)
    el2 = jnp.where(lane == i1, neg, el)
    m2 = jnp.max(el2, axis=1, keepdims=True)
    i2 = jnp.min(jnp.where((el2 == m2) & emask & (lane != i1), lane, far), axis=1, keepdims=True)
    e2 = jnp.exp(m2 - m1)
    p1 = p_grp / (1.0 + e2)
    p2 = p1 * e2
    n_rows = logits.shape[0]
    earlier = (lax.broadcasted_iota(jnp.int32, (n_rows, n_rows), 0)
               > lax.broadcasted_iota(jnp.int32, (n_rows, n_rows), 1)).astype(BF16)
    hot1 = (lane == i1).astype(F32)
    hot2 = (lane == i2).astype(F32)
    before1 = jnp.dot(earlier, hot1.astype(BF16), preferred_element_type=F32)
    before2 = jnp.dot(earlier, hot2.astype(BF16), preferred_element_type=F32)
    cnt1 = jnp.sum(hot1, axis=0, keepdims=True)
    cnt2 = jnp.sum(hot2, axis=0, keepdims=True)
    seen = seen_ref[...]
    rank1 = jnp.sum(hot1 * (seen + before1), axis=1, keepdims=True)
    rank2 = jnp.sum(hot2 * (seen + cnt1 + before2), axis=1, keepdims=True)
    seen_ref[...] = seen + cnt1 + cnt2

    out = jnp.where(lane == 0, i1, 0.0)
    for k, col in enumerate((i2, p1, p2, rank1, rank2), start=1):
        out = jnp.where(lane == k, col, out)
    return out


def _mixer_tail(x, mix_out, mod_ref, gffn_ref, wr_ref, br_ref, xnew_ref, h2_ref, route_ref, cnt_ref, seen_ref):
    @pl.when(pl.program_id(0) == 0)
    def _():
        seen_ref[...] = jnp.zeros_like(seen_ref)

    x_new = x + mod_ref[2:3, :] * mix_out
    xnew_ref[...] = x_new
    h2 = _norm_mod(x_new, gffn_ref[...], mod_ref[3:4, :], mod_ref[4:5, :])
    h2_ref[...] = h2
    route_ref[...] = _route(h2.astype(BF16), wr_ref, br_ref, seen_ref)
    cnt_ref[...] = seen_ref[...].astype(jnp.int32)


ADA_TN = 1024


def _ada_kernel(cond_ref, w_ref, b_ref, o_ref):
    c = cond_ref[...]
    s = c * jax.nn.sigmoid(c)
    o_ref[...] = jnp.dot(s.astype(BF16), w_ref[...].astype(BF16), preferred_element_type=F32) + b_ref[...]


def _ada_call(cond, w_ada, b_ada):
    nj = (N_MOD * D_MODEL) // ADA_TN
    return pl.pallas_call(
        _ada_kernel,
        out_shape=jax.ShapeDtypeStruct((DEPTH, COND_ROWS, N_MOD * D_MODEL), F32),
        grid=(DEPTH, nj),
        in_specs=[
            pl.BlockSpec((COND_ROWS, D_MODEL), lambda l, j: (0, 0)),
            pl.BlockSpec((None, D_MODEL, ADA_TN), lambda l, j: (l, 0, j)),
            pl.BlockSpec((None, 1, ADA_TN), lambda l, j: (l, 0, j)),
        ],
        out_specs=pl.BlockSpec((None, COND_ROWS, ADA_TN), lambda l, j: (l, 0, j)),
        compiler_params=_params(40, 2),
        name="ada_mod",
    )(cond, w_ada, b_ada.reshape(DEPTH, 1, N_MOD * D_MODEL))


def _pool_band():
    col = np.arange(POOL_K)
    rel = np.where(col < TM, col, np.where(col < TM + POOL_HALO, col - TM - POOL_HALO, col - POOL_HALO))
    ok = col < TM + 2 * POOL_HALO
    row = np.arange(TM)[:, None]
    mats = []
    for w in POOL_WINDOWS:
        mats.append((rel[None, :] >= row - w // 2) & (rel[None, :] < row + (w - w // 2)) & ok[None, :])
    return np.stack(mats).astype(np.float32)


def _pool_kernel(x_ref, xp_ref, xn_ref, mod_ref, gmix_ref, gffn_ref, band_ref, wpool_ref, pscale_ref,
                 wr_ref, br_ref, xnew_ref, h2_ref, route_ref, cnt_ref, seen_ref):
    i = pl.program_id(0)
    lat = i >= CTX_TILES
    tile_pos = jnp.where(lat, jnp.maximum(i - CTX_TILES, 0) % LAT_TILES_PER_SEQ, 0)
    seq_len = jnp.where(lat, DEC_SEQ, SEQ)
    has_prev = tile_pos > 0
    has_next = (tile_pos + 1) * TM < seq_len

    g = gmix_ref[...]
    shift, scale = mod_ref[0:1, :], mod_ref[1:2, :]
    x = x_ref[...]
    h = _norm_mod(x, g, shift, scale)
    hp = jnp.where(has_prev, _norm_mod(xp_ref[...], g, shift, scale), 0.0)
    hn = jnp.where(has_next, _norm_mod(xn_ref[...], g, shift, scale), 0.0)
    hall = jnp.concatenate(
        [h, hp, hn, jnp.zeros((POOL_K - TM - 2 * POOL_HALO, D_MODEL), F32)], axis=0)
    hi = hall.astype(BF16)
    lo = (hall - hi.astype(F32)).astype(BF16)

    pos = tile_pos * TM + lax.broadcasted_iota(jnp.int32, (TM, 1), 0)
    outs = []
    for gi, w in enumerate(POOL_WINDOWS):
        sl = slice(gi * POOL_CH, (gi + 1) * POOL_CH)
        band = band_ref[gi]
        wsum = (jnp.dot(band, hi[:, sl], preferred_element_type=F32)
                + jnp.dot(band, lo[:, sl], preferred_element_type=F32))
        cnt = jnp.minimum(pos + (w - w // 2), seq_len) - jnp.maximum(pos - w // 2, 0)
        diff = wsum / cnt.astype(F32) - h[:, sl]
        outs.append(jnp.dot(diff.astype(BF16), wpool_ref[gi], preferred_element_type=F32))
    mix = jnp.concatenate(outs, axis=1) * pscale_ref[...]
    _mixer_tail(x, mix, mod_ref, gffn_ref, wr_ref, br_ref, xnew_ref, h2_ref, route_ref, cnt_ref, seen_ref)


def _tail_out_shapes():
    return (jax.ShapeDtypeStruct((N_TOK, D_MODEL), F32),
            jax.ShapeDtypeStruct((N_TOK, D_MODEL), F32),
            jax.ShapeDtypeStruct((N_TOK, ROUTE_LANES), F32),
            jax.ShapeDtypeStruct((1, ROUTE_LANES), jnp.int32))


def _tail_out_specs():
    return (pl.BlockSpec((TM, D_MODEL), lambda i: (i, 0)),
            pl.BlockSpec((TM, D_MODEL), lambda i: (i, 0)),
            pl.BlockSpec((TM, ROUTE_LANES), lambda i: (i, 0)),
            pl.BlockSpec((1, ROUTE_LANES), lambda i: (0, 0)))


def _tail_scratch():
    return [pltpu.VMEM((1, ROUTE_LANES), F32)]


def _row_spec(width):
    return pl.BlockSpec((1, width), lambda i: (0, 0))


def _mod_spec(layer):
    return pl.BlockSpec((None, None, N_MOD, D_MODEL), lambda i: (layer, _mod_row(i), 0, 0))


def _pool_call(layer, x, mod, g_mix, g_ffn, band, wpool, pscale, wr, br):
    halo_blocks = TM // POOL_HALO
    last_halo = N_TOK // POOL_HALO - 1
    return pl.pallas_call(
        _pool_kernel,
        out_shape=_tail_out_shapes(),
        grid=(N_TILES,),
        in_specs=[
            pl.BlockSpec((TM, D_MODEL), lambda i: (i, 0)),
            pl.BlockSpec((POOL_HALO, D_MODEL), lambda i: (jnp.maximum(i * halo_blocks - 1, 0), 0)),
            pl.BlockSpec((POOL_HALO, D_MODEL), lambda i: (jnp.minimum((i + 1) * halo_blocks, last_halo), 0)),
            _mod_spec(layer),
            _row_spec(D_MODEL),
            _row_spec(D_MODEL),
            pl.BlockSpec((N_POOL_GROUPS, TM, POOL_K), lambda i: (0, 0, 0)),
            pl.BlockSpec((N_POOL_GROUPS, POOL_CH, POOL_CH), lambda i: (0, 0, 0)),
            _row_spec(D_MODEL),
            pl.BlockSpec((D_MODEL, ROUTE_LANES), lambda i: (0, 0)),
            _row_spec(ROUTE_LANES),
        ],
        out_specs=_tail_out_specs(),
        scratch_shapes=_tail_scratch(),
        compiler_params=_params(48),
        name=f"pool_layer{layer}",
    )(x, x, x, mod, g_mix, g_ffn, band, wpool, pscale, wr, br)


def _qkv_kernel(x_ref, mod_ref, gmix_ref, wdq_ref, gq_ref, wdkv_ref, gkv_ref, wkr_ref, wuqn_ref, wuqr_ref,
                tabq_ref, tabk_ref, qn_ref, qr_ref, ckv_ref, kr_ref):
    i = pl.program_id(0)
    h = _norm_mod(x_ref[...], gmix_ref[...], mod_ref[0:1, :], mod_ref[1:2, :]).astype(BF16)

    def rms(v, g):
        return v * lax.rsqrt(jnp.mean(v * v, axis=-1, keepdims=True) + EPS) * g

    cq = rms(jnp.dot(h, wdq_ref[...], preferred_element_type=F32), gq_ref[...]).astype(BF16)
    ckv_ref[...] = rms(jnp.dot(h, wdkv_ref[...], preferred_element_type=F32), gkv_ref[...])
    qn_ref[...] = jnp.dot(cq, wuqn_ref[...], preferred_element_type=F32).astype(BF16)
    qr = jnp.dot(cq, wuqr_ref[...], preferred_element_type=F32)
    kr = jnp.dot(h, wkr_ref[...], preferred_element_type=F32)
    lane = lax.broadcasted_iota(jnp.int32, kr.shape, 1)

    @pl.when(i < CTX_TILES)
    def _():
        qr_ref[...] = qr.astype(BF16)
        kr_ref[...] = jnp.where(lane < ROPE_DIM, kr, 0.0)

    @pl.when(i >= CTX_TILES)
    def _():
        tq = qr * tabq_ref[...]
        qr_ref[...] = (tq + pltpu.roll(tq, N_HEADS * LANES - ROPE_DIM, 1)).astype(BF16)
        tk = kr * tabk_ref[...]
        kr_ref[...] = jnp.where(lane < ROPE_DIM, tk + pltpu.roll(tk, ROPE_DIM, 1), 0.0)


def _qkv_call(layer, x, mod, g_mix, wdq, gq, wdkv, gkv, wkr, wuqn, wuqr, tabq, tabk):
    hd = N_HEADS * LANES

    def lat_tile(i):
        return jnp.maximum(i - CTX_TILES, 0) % LAT_TILES_PER_SEQ

    def full(a):
        return pl.BlockSpec(a.shape, lambda i: (0,) * a.ndim)

    return pl.pallas_call(
        _qkv_kernel,
        out_shape=(jax.ShapeDtypeStruct((N_TOK, hd), BF16),
                   jax.ShapeDtypeStruct((N_TOK, hd), BF16),
                   jax.ShapeDtypeStruct((N_TOK, KV_RANK), F32),
                   jax.ShapeDtypeStruct((N_TOK, LANES), F32)),
        grid=(N_TILES,),
        in_specs=[
            pl.BlockSpec((TM, D_MODEL), lambda i: (i, 0)),
            _mod_spec(layer),
            _row_spec(D_MODEL),
            full(wdq), _row_spec(Q_RANK), full(wdkv), _row_spec(KV_RANK), full(wkr), full(wuqn), full(wuqr),
            pl.BlockSpec((TM, hd), lambda i: (lat_tile(i), 0)),
            pl.BlockSpec((TM, LANES), lambda i: (lat_tile(i), 0)),
        ],
        out_specs=(pl.BlockSpec((TM, hd), lambda i: (i, 0)),
                   pl.BlockSpec((TM, hd), lambda i: (i, 0)),
                   pl.BlockSpec((TM, KV_RANK), lambda i: (i, 0)),
                   pl.BlockSpec((TM, LANES), lambda i: (i, 0))),
        compiler_params=_params(48),
        name=f"mla_qkv{layer}",
    )(x, mod, g_mix, wdq, gq, wdkv, gkv, wkr, wuqn, wuqr, tabq, tabk)


HEAD_K = 2 * LANES


def _attn_kernel(*refs, n_new, n_cache):
    if n_cache:
        (qn_ref, qr_ref, ckv_ref, kr_ref, cckv_ref, ckr_ref, wuk_ref, wuv_ref, _, o_ref, kcat, vs) = refs
    else:
        (qn_ref, qr_ref, ckv_ref, kr_ref, wuk_ref, wuv_ref, _, o_ref, kcat, vs) = refs
        cckv_ref = ckr_ref = None

    @pl.when(pl.program_id(1) == 0)
    def _():
        def expand(src_ref, kr_src_ref, n_rows, base):
            for c in range(n_rows // TM):
                rows = slice(c * TM, (c + 1) * TM)
                dst = slice(base + c * TM, base + (c + 1) * TM)
                lat = src_ref[rows, :].astype(BF16)
                kn = jnp.dot(lat, wuk_ref[...], preferred_element_type=F32).astype(BF16)
                vs[dst, :] = jnp.dot(lat, wuv_ref[...], preferred_element_type=F32).astype(BF16)
                krb = kr_src_ref[rows, :].astype(BF16)
                for hh in range(N_HEADS):
                    kcat[dst, hh * HEAD_K:hh * HEAD_K + LANES] = kn[:, hh * LANES:(hh + 1) * LANES]
                    kcat[dst, hh * HEAD_K + LANES:(hh + 1) * HEAD_K] = krb

        expand(ckv_ref, kr_ref, n_new, 0)
        if n_cache:
            expand(cckv_ref, ckr_ref, n_cache, n_new)

    for hh in range(N_HEADS):
        hs = slice(hh * LANES, (hh + 1) * LANES)
        q = jnp.concatenate([qn_ref[:, hs], qr_ref[:, hs]], axis=1)
        s = lax.dot_general(q, kcat[:, hh * HEAD_K:(hh + 1) * HEAD_K], (((1,), (1,)), ((), ())),
                            preferred_element_type=F32) * ATTN_SCALE
        p = jnp.exp(s - jnp.max(s, axis=1, keepdims=True))
        den = jnp.sum(p, axis=1, keepdims=True)
        o = jnp.dot(p.astype(BF16), vs[:, hs], preferred_element_type=F32) / den
        o_ref[:, hs] = o.astype(BF16)


def _attn_call(layer, qn, qr, ckv, kr, wuk, wuv, o_prev, cache=None):
    hd = N_HEADS * LANES
    if cache is None:
        n_b, n_q, n_new, n_cache, tile0 = BATCH, SEQ // TM, SEQ, 0, 0
    else:
        n_b, n_q, n_new, n_cache, tile0 = DEC_BATCH, DEC_SEQ // TM, DEC_SEQ, PAST_LEN, CTX_TILES
    seq_blk0 = tile0 * TM // n_new
    m = layer // N_MIXERS

    def q_map(b, j):
        return (tile0 + b * n_q + j, 0)

    in_specs = [
        pl.BlockSpec((TM, hd), q_map),
        pl.BlockSpec((TM, hd), q_map),
        pl.BlockSpec((n_new, KV_RANK), lambda b, j: (seq_blk0 + b, 0)),
        pl.BlockSpec((n_new, LANES), lambda b, j: (seq_blk0 + b, 0)),
    ]
    args = [qn, qr, ckv, kr]
    if cache is not None:
        in_specs += [
            pl.BlockSpec((None, None, n_cache, KV_RANK), lambda b, j: (b, m, 0, 0)),
            pl.BlockSpec((None, None, n_cache, LANES), lambda b, j: (b, m, 0, 0)),
        ]
        args += list(cache)
    in_specs += [
        pl.BlockSpec((KV_RANK, hd), lambda b, j: (0, 0)),
        pl.BlockSpec((KV_RANK, hd), lambda b, j: (0, 0)),
        pl.BlockSpec(memory_space=pl.ANY),
    ]
    args += [wuk, wuv, o_prev]
    n_keys = n_new + n_cache
    return pl.pallas_call(
        functools.partial(_attn_kernel, n_new=n_new, n_cache=n_cache),
        out_shape=jax.ShapeDtypeStruct((N_TOK, hd), BF16),
        grid=(n_b, n_q),
        in_specs=in_specs,
        out_specs=pl.BlockSpec((TM, hd), q_map),
        scratch_shapes=[pltpu.VMEM((n_keys, N_HEADS * HEAD_K), BF16), pltpu.VMEM((n_keys, hd), BF16)],
        input_output_aliases={len(args) - 1: 0},
        compiler_params=_params(56, 2),
        name=f"mla_attn{layer}_{'lat' if cache is not None else 'ctx'}",
    )(*args)


def _attn_out_kernel(o_ref, wo_ref, x_ref, mod_ref, gffn_ref, wr_ref, br_ref, xnew_ref, h2_ref, route_ref,
                     cnt_ref, seen_ref):
    mix = jnp.dot(o_ref[...], wo_ref[...], preferred_element_type=F32)
    _mixer_tail(x_ref[...], mix, mod_ref, gffn_ref, wr_ref, br_ref, xnew_ref, h2_ref, route_ref, cnt_ref,
                seen_ref)


def _attn_out_call(layer, o, wo, x, mod, g_ffn, wr, br):
    hd = N_HEADS * V_DIM
    return pl.pallas_call(
        _attn_out_kernel,
        out_shape=_tail_out_shapes(),
        grid=(N_TILES,),
        in_specs=[
            pl.BlockSpec((TM, hd), lambda i: (i, 0)),
            pl.BlockSpec((hd, D_MODEL), lambda i: (0, 0)),
            pl.BlockSpec((TM, D_MODEL), lambda i: (i, 0)),
            _mod_spec(layer),
            _row_spec(D_MODEL),
            pl.BlockSpec((D_MODEL, ROUTE_LANES), lambda i: (0, 0)),
            _row_spec(ROUTE_LANES),
        ],
        out_specs=_tail_out_specs(),
        scratch_shapes=_tail_scratch(),
        compiler_params=_params(48),
        name=f"mla_out{layer}",
    )(o, wo, x, mod, g_ffn, wr, br)


def _plan_kernel(cnt_ref, route_ref, pos_ref, wt_ref, we_ref, wlo_ref, whi_ref, nw_ref, start_ref):
    @pl.when(pl.program_id(0) == 0)
    def _():
        lane = lax.broadcasted_iota(jnp.int32, (1, ROUTE_LANES), 1)
        shift = TM.bit_length() - 1

        def per_expert(e, carry):
            n_work, lo, last_e, starts = carry
            cnt = cnt_ref[0, e]
            hi = lo + cnt
            first_tile = lax.shift_right_logical(lo, shift)
            end_tile = jnp.where(cnt > 0, lax.shift_right_logical(hi + (TM - 1), shift), first_tile)

            def per_tile(t, w):
                wt_ref[w] = t
                we_ref[w] = e
                wlo_ref[w] = lo
                whi_ref[w] = hi
                return w + 1

            n_work = lax.fori_loop(first_tile, end_tile, per_tile, n_work)
            starts = jnp.where(lane == e, lo.astype(F32), starts)
            return n_work, hi, jnp.where(cnt > 0, e, last_e), starts

        n_work, _, last_e, starts = lax.fori_loop(
            0, N_EXPERTS, per_expert,
            (jnp.int32(0), jnp.int32(0), jnp.int32(0), jnp.zeros((1, ROUTE_LANES), F32)))
        nw_ref[0] = n_work
        start_ref[...] = starts

        def fill(w, carry):
            wt_ref[w] = SORT_TILES - 1
            we_ref[w] = last_e
            wlo_ref[w] = 0
            whi_ref[w] = 0
            return carry

        lax.fori_loop(n_work, MAX_WORK, fill, 0)

    r = route_ref[...]
    lane = lax.broadcasted_iota(jnp.int32, r.shape, 1).astype(F32)
    starts = start_ref[...]
    pos1 = jnp.sum(jnp.where(lane == r[:, 0:1], starts, 0.0), axis=1, keepdims=True) + r[:, 4:5]
    pos2 = jnp.sum(jnp.where(lane == r[:, 1:2], starts, 0.0), axis=1, keepdims=True) + r[:, 5:6]
    pos = jnp.where(lane == 0.0, pos1, jnp.where(lane == 1.0, pos2, 0.0))
    pos_ref[...] = pos.T[0:SUBLANES, :].astype(jnp.int32)


def _plan_call(layer, counts, route):
    smem = pl.BlockSpec(memory_space=pltpu.SMEM)
    work = jax.ShapeDtypeStruct((MAX_WORK,), jnp.int32)
    return pl.pallas_call(
        _plan_kernel,
        out_shape=(jax.ShapeDtypeStruct((SUBLANES, N_TOK), jnp.int32), work, work, work, work,
                   jax.ShapeDtypeStruct((1,), jnp.int32)),
        grid=(N_TILES,),
        in_specs=[smem, pl.BlockSpec((TM, ROUTE_LANES), lambda i: (i, 0))],
        out_specs=(pl.BlockSpec((SUBLANES, TM), lambda i: (0, i)), smem, smem, smem, smem, smem),
        scratch_shapes=[pltpu.VMEM((1, ROUTE_LANES), F32)],
        compiler_params=_params(16),
        name=f"moe_plan{layer}",
    )(counts, route)


def _dispatch_kernel(pos_ref, h2_hbm, xs_hbm, sem):
    i = pl.program_id(0)

    def issue(r, carry):
        t = i * TM + r
        src = h2_hbm.at[pl.ds(t, 1)]
        pltpu.make_async_copy(src, xs_hbm.at[pl.ds(pos_ref[t], 1)], sem.at[i % 2]).start()
        pltpu.make_async_copy(src, xs_hbm.at[pl.ds(pos_ref[N_TOK + t], 1)], sem.at[i % 2]).start()
        return carry

    lax.fori_loop(0, TM, issue, 0, unroll=8)

    def drain(slot):
        for _ in range(2):
            pltpu.make_async_copy(h2_hbm.at[pl.ds(0, TM)], xs_hbm.at[pl.ds(0, TM)], sem.at[slot]).wait()

    @pl.when(i > 0)
    def _():
        drain((i - 1) % 2)

    @pl.when(i == N_TILES - 1)
    def _():
        drain(i % 2)


def _dispatch_call(layer, pos, h2):
    return pl.pallas_call(
        _dispatch_kernel,
        out_shape=jax.ShapeDtypeStruct((N_SLOTS, D_MODEL), F32),
        grid_spec=pltpu.PrefetchScalarGridSpec(
            num_scalar_prefetch=1,
            grid=(N_TILES,),
            in_specs=[pl.BlockSpec(memory_space=pl.ANY)],
            out_specs=pl.BlockSpec(memory_space=pl.ANY),
            scratch_shapes=[pltpu.SemaphoreType.DMA((2,))],
        ),
        compiler_params=_params(16),
        name=f"moe_dispatch{layer}",
    )(pos, h2)


def _gmm_kernel(wt_ref, we_ref, wlo_ref, whi_ref, nw_ref, xs_ref, wg_ref, wu_ref, wd_ref, ys_ref,
                wg_s, wu_s, wd_s):
    w = pl.program_id(0)
    prev = jnp.maximum(w - 1, 0)
    new_expert = (w == 0) | (we_ref[w] != we_ref[prev])
    new_tile = (w == 0) | (wt_ref[w] != wt_ref[prev])

    @pl.when(w < nw_ref[0])
    def _():
        @pl.when(new_expert)
        def _():
            wg_s[...] = wg_ref[...].astype(BF16)
            wu_s[...] = wu_ref[...].astype(BF16)
            wd_s[...] = wd_ref[...].astype(BF16)

        x = xs_ref[...].astype(BF16)
        gate = jnp.dot(x, wg_s[...], preferred_element_type=F32)
        up = jnp.dot(x, wu_s[...], preferred_element_type=F32)
        row = wt_ref[w] * TM + lax.broadcasted_iota(jnp.int32, (TM, 1), 0)
        mine = (row >= wlo_ref[w]) & (row < whi_ref[w])
        hid = jnp.where(mine, (gate * jax.nn.sigmoid(gate)) * up, 0.0)
        y = jnp.dot(hid.astype(BF16), wd_s[...], preferred_element_type=F32)

        @pl.when(new_tile)
        def _():
            ys_ref[...] = y

        @pl.when(jnp.logical_not(new_tile))
        def _():
            ys_ref[...] += y


def _gmm_call(layer, xs, work, w_gate, w_up, w_down):
    def w_spec(shape):
        return pl.BlockSpec((None, None) + shape, lambda w, wt, we, wlo, whi, nw: (layer, we[w], 0, 0))

    def tile_spec():
        return pl.BlockSpec((TM, D_MODEL), lambda w, wt, we, wlo, whi, nw: (wt[w], 0))

    grid_spec = pltpu.PrefetchScalarGridSpec(
        num_scalar_prefetch=5,
        grid=(MAX_WORK,),
        in_specs=[
            tile_spec(),
            w_spec((D_MODEL, EXPERT_HIDDEN)),
            w_spec((D_MODEL, EXPERT_HIDDEN)),
            w_spec((EXPERT_HIDDEN, D_MODEL)),
        ],
        out_specs=tile_spec(),
        scratch_shapes=[
            pltpu.VMEM((D_MODEL, EXPERT_HIDDEN), BF16),
            pltpu.VMEM((D_MODEL, EXPERT_HIDDEN), BF16),
            pltpu.VMEM((EXPERT_HIDDEN, D_MODEL), BF16),
        ],
    )
    return pl.pallas_call(
        _gmm_kernel,
        out_shape=jax.ShapeDtypeStruct((N_SLOTS, D_MODEL), F32),
        grid_spec=grid_spec,
        compiler_params=_params(48),
        name=f"moe_gmm{layer}",
    )(*work, xs, w_gate, w_up, w_down)


def _combine_kernel(pos_ref, x_ref, route_ref, mod_ref, gfin_ref, ys_hbm, o_ref, buf1, buf2, sem1, sem2, *,
                    final):
    i = pl.program_id(0)

    def fetch(tile):
        slot = tile % 2

        def issue(r, carry):
            t = tile * TM + r
            pltpu.make_async_copy(ys_hbm.at[pl.ds(pos_ref[t], 1)], buf1.at[slot, pl.ds(r, 1)],
                                  sem1.at[slot]).start()
            pltpu.make_async_copy(ys_hbm.at[pl.ds(pos_ref[N_TOK + t], 1)], buf2.at[slot, pl.ds(r, 1)],
                                  sem2.at[slot]).start()
            return carry

        lax.fori_loop(0, TM, issue, 0, unroll=8)

    @pl.when(i == 0)
    def _():
        fetch(i)

    @pl.when(i + 1 < N_TILES)
    def _():
        fetch(i + 1)

    slot = i % 2
    pltpu.make_async_copy(ys_hbm.at[pl.ds(0, TM)], buf1.at[slot], sem1.at[slot]).wait()
    pltpu.make_async_copy(ys_hbm.at[pl.ds(0, TM)], buf2.at[slot], sem2.at[slot]).wait()
    r = route_ref[...]
    moe = r[:, 2:3] * buf1[slot] + r[:, 3:4] * buf2[slot]
    x = x_ref[...] + mod_ref[5:6, :] * moe
    if final:
        x = x * lax.rsqrt(jnp.mean(x * x, axis=-1, keepdims=True) + EPS) * gfin_ref[...]
    o_ref[...] = x


def _combine_call(layer, pos, x_new, route, ys, mod, g_final, final):
    def tok_spec(width):
        return pl.BlockSpec((TM, width), lambda i, pos: (i, 0))

    grid_spec = pltpu.PrefetchScalarGridSpec(
        num_scalar_prefetch=1,
        grid=(N_TILES,),
        in_specs=[
            tok_spec(D_MODEL),
            tok_spec(ROUTE_LANES),
            pl.BlockSpec((None, None, N_MOD, D_MODEL), lambda i, pos: (layer, _mod_row(i), 0, 0)),
            pl.BlockSpec((1, D_MODEL), lambda i, pos: (0, 0)),
            pl.BlockSpec(memory_space=pl.ANY),
        ],
        out_specs=tok_spec(D_MODEL),
        scratch_shapes=[
            pltpu.VMEM((2, TM, D_MODEL), F32),
            pltpu.VMEM((2, TM, D_MODEL), F32),
            pltpu.SemaphoreType.DMA((2,)),
            pltpu.SemaphoreType.DMA((2,)),
        ],
    )
    return pl.pallas_call(
        functools.partial(_combine_kernel, final=final),
        out_shape=jax.ShapeDtypeStruct((N_TOK, D_MODEL), F32),
        grid_spec=grid_spec,
        compiler_params=_params(40),
        name=f"moe_combine{layer}",
    )(pos, x_new, route, mod, g_final, ys)


def _rope_tables():
    rows = DEC_SEQ // GRID_W
    row_ids = jnp.repeat(jnp.arange(rows), GRID_W).astype(F32)
    col_ids = jnp.tile(jnp.arange(GRID_W), rows).astype(F32)
    freqs = ROPE_THETA ** (-jnp.arange(ROPE_QUARTER, dtype=F32) / ROPE_QUARTER)
    ar, ac = row_ids[:, None] * freqs[None, :], col_ids[:, None] * freqs[None, :]
    cos = jnp.concatenate([jnp.cos(ar), jnp.cos(ar), jnp.cos(ac), jnp.cos(ac)], axis=1)
    sin = jnp.concatenate([-jnp.sin(ar), jnp.sin(ar), -jnp.sin(ac), jnp.sin(ac)], axis=1)
    return jnp.concatenate([cos, sin], axis=1)


def _swap_partners(w):
    q = ROPE_QUARTER
    return jnp.concatenate([w[..., q:2 * q], w[..., 0:q], w[..., 3 * q:4 * q], w[..., 2 * q:3 * q]], axis=-1)


def kernel(x_prompt, x_sample, cache_ckv, cache_krope, c, c_ctx, g_mix, g_ffn, w_ada, b_ada, w_pool, pool_scale,
           w_dq, g_q, w_uq, w_dkv, g_kv, w_uk, w_uv, w_o, w_router_grp, b_router_grp, w_router_exp,
           b_router_exp, w_gate, w_up, w_down, g_final):
    n_mla = DEPTH // N_MIXERS
    x = jnp.concatenate([x_prompt.reshape(N_CTX, D_MODEL), x_sample.reshape(N_LAT, D_MODEL)], axis=0)

    cond = jnp.concatenate([c_ctx[None, :], c, jnp.zeros((COND_ROWS - 1 - DEC_BATCH, D_MODEL), F32)], axis=0)
    mod = _ada_call(cond, w_ada, b_ada).reshape(DEPTH, COND_ROWS, N_MOD, D_MODEL)

    wr = jnp.concatenate([w_router_exp, w_router_grp,
                          jnp.zeros((DEPTH, D_MODEL, ROUTE_LANES - N_EXPERTS - N_EXPERT_GROUPS), F32)],
                         axis=2).astype(BF16)
    br = jnp.concatenate([b_router_exp, b_router_grp,
                          jnp.zeros((DEPTH, ROUTE_LANES - N_EXPERTS - N_EXPERT_GROUPS), F32)], axis=1)
    band = jnp.asarray(_pool_band(), dtype=BF16)
    wpool = w_pool.astype(BF16)
    wdq = w_dq.astype(BF16)
    wdkv = w_dkv[:, :, :KV_RANK].astype(BF16)
    w_kr = w_dkv[:, :, KV_RANK:]
    wkr = jnp.concatenate([w_kr, _swap_partners(w_kr)], axis=-1).astype(BF16)
    wuqn = w_uq[..., :NOPE_DIM].reshape(n_mla, Q_RANK, N_HEADS * NOPE_DIM).astype(BF16)
    w_qr = w_uq[..., NOPE_DIM:]
    wuqr = jnp.concatenate([w_qr, _swap_partners(w_qr)], axis=-1).reshape(n_mla, Q_RANK, N_HEADS * LANES)
    wuqr = wuqr.astype(BF16)
    wuk = w_uk.reshape(n_mla, KV_RANK, N_HEADS * NOPE_DIM).astype(BF16)
    wuv = w_uv.reshape(n_mla, KV_RANK, N_HEADS * V_DIM).astype(BF16)
    wo = w_o.astype(BF16)
    tabk = _rope_tables()
    tabq = jnp.tile(tabk, (1, N_HEADS))
    cache_kr = jnp.pad(cache_krope, ((0, 0), (0, 0), (0, 0), (0, LANES - ROPE_DIM)))

    def row(a):
        return a.reshape(1, -1)

    ckv_ctx, kr_ctx = [], []
    for layer in range(DEPTH):
        if layer % N_MIXERS == 0:
            p = layer // N_MIXERS
            x_new, h2, route, counts = _pool_call(layer, x, mod, row(g_mix[layer]), row(g_ffn[layer]), band,
                                                  wpool[p], row(pool_scale[p]), wr[layer], row(br[layer]))
        else:
            m = layer // N_MIXERS
            qn, qr, ckv, kr = _qkv_call(layer, x, mod, row(g_mix[layer]), wdq[m], row(g_q[m]), wdkv[m],
                                        row(g_kv[m]), wkr[m], wuqn[m], wuqr[m], tabq, tabk)
            ckv_ctx.append(ckv[:N_CTX].reshape(BATCH, SEQ, KV_RANK))
            kr_ctx.append(kr[:N_CTX, :ROPE_DIM].reshape(BATCH, SEQ, ROPE_DIM))
            o = jnp.zeros((N_TOK, N_HEADS * V_DIM), BF16)
            o = _attn_call(layer, qn, qr, ckv, kr, wuk[m], wuv[m], o)
            o = _attn_call(layer, qn, qr, ckv, kr, wuk[m], wuv[m], o, cache=(cache_ckv, cache_kr))
            x_new, h2, route, counts = _attn_out_call(layer, o, wo[m], x, mod, row(g_ffn[layer]), wr[layer],
                                                      row(br[layer]))
        pos8, *work = _plan_call(layer, counts, route)
        pos = pos8[:2].reshape(N_SLOTS)
        xs = _dispatch_call(layer, pos, h2)
        ys = _gmm_call(layer, xs, work, w_gate, w_up, w_down)
        x = _combine_call(layer, pos, x_new, route, ys, mod, row(g_final), final=(layer == DEPTH - 1))

    y_prompt = x[:N_CTX].reshape(BATCH, SEQ, D_MODEL)
    y_sample = x[N_CTX:].reshape(DEC_BATCH, DEC_SEQ, D_MODEL)
    return y_prompt, y_sample, jnp.stack(ckv_ctx, axis=1), jnp.stack(kr_ctx, axis=1)
```

```python
import functools

import numpy as np
import jax
import jax.numpy as jnp
from jax import lax
from jax.experimental import pallas as pl
from jax.experimental.pallas import tpu as pltpu

D_MODEL = 2048
BATCH = 32
SEQ = 256
DEPTH = 4
DEC_BATCH = 2
DEC_SEQ = 1024
PAST_LEN = 512
GRID_W = 64
N_MIXERS = 2
POOL_WINDOWS = (2, 4, 8, 16)
N_POOL_GROUPS = 4
POOL_CH = D_MODEL // N_POOL_GROUPS
N_HEADS = 16
Q_RANK = 512
KV_RANK = 512
NOPE_DIM = 128
ROPE_DIM = 64
V_DIM = 128
ROPE_QUARTER = ROPE_DIM // 4
ROPE_THETA = 10000.0
ATTN_SCALE = (NOPE_DIM + ROPE_DIM) ** -0.5
N_EXPERT_GROUPS = 4
EXPERTS_PER_GROUP = 8
N_EXPERTS = N_EXPERT_GROUPS * EXPERTS_PER_GROUP
EXPERT_HIDDEN = 512
N_MOD = 6
EPS = 1e-6

LANES = 128
SUBLANES = 8
TM = 256
N_CTX = BATCH * SEQ
N_LAT = DEC_BATCH * DEC_SEQ
N_TOK = N_CTX + N_LAT
CTX_TILES = N_CTX // TM
LAT_TILES = N_LAT // TM
N_TILES = CTX_TILES + LAT_TILES
LAT_TILES_PER_SEQ = DEC_SEQ // TM
COND_ROWS = SUBLANES
POOL_HALO = max(POOL_WINDOWS) // 2
POOL_K = TM + LANES
ROUTE_LANES = LANES
N_SLOTS = 2 * N_TOK
SORT_TILES = N_SLOTS // TM
MAX_WORK = SORT_TILES + N_EXPERTS - 1
PACKED_D = D_MODEL // 2
MIB = 1024 * 1024

F32 = jnp.float32
BF16 = jnp.bfloat16


def _mod_row(tile):
    return jnp.where(tile < CTX_TILES, 0, 1 + jnp.maximum(tile - CTX_TILES, 0) // LAT_TILES_PER_SEQ)


def _params(vmem_mib, n_axes=1):
    return pltpu.CompilerParams(
        dimension_semantics=("arbitrary",) * n_axes,
        vmem_limit_bytes=vmem_mib * MIB,
    )


def _norm_mod(x, g, shift, scale):
    y = x * lax.rsqrt(jnp.mean(x * x, axis=-1, keepdims=True) + EPS)
    return (y * g) * (1.0 + scale) + shift


def _pack_halves(v):
    c = v.shape[1] // 2
    lo = lax.bitcast_convert_type(v[:, :c].astype(F32), jnp.uint32)
    hi = lax.bitcast_convert_type(v[:, c:].astype(F32), jnp.uint32)
    return lax.shift_right_logical(lo, jnp.uint32(16)) | hi


def _unpack_halves(u):
    lo = lax.bitcast_convert_type(lax.shift_left(u, jnp.uint32(16)), F32)
    hi = lax.bitcast_convert_type(u & jnp.uint32(0xFFFF0000), F32)
    return jnp.concatenate([lo.astype(BF16), hi.astype(BF16)], axis=1)


def _route(h_bf, wr_ref, br_ref, seen_ref):
    logits = jnp.dot(h_bf, wr_ref[...], preferred_element_type=F32) + br_ref[...]
    lane_i = lax.broadcasted_iota(jnp.int32, logits.shape, 1)
    lane = lane_i.astype(F32)
    neg = -jnp.inf
    far = float(ROUTE_LANES)
    gmask = (lane_i >= N_EXPERTS) & (lane_i < N_EXPERTS + N_EXPERT_GROUPS)
    gl = jnp.where(gmask, logits, neg)
    gmax = jnp.max(gl, axis=1, keepdims=True)
    grp = jnp.min(jnp.where(gl == gmax, lane, far), axis=1, keepdims=True) - N_EXPERTS
    p_grp = 1.0 / jnp.sum(jnp.exp(gl - gmax), axis=1, keepdims=True)
    lane_grp = lax.shift_right_logical(lane_i, EXPERTS_PER_GROUP.bit_length() - 1).astype(F32)
    emask = lane_grp == grp
    el = jnp.where(emask, logits, neg)
    m1 = jnp.max(el, axis=1, keepdims=True)
    i1 = jnp.min(jnp.where(el == m1, lane, far), axis=1, keepdims=True)
    el2 = jnp.where(lane == i1, neg, el)
    m2 = jnp.max(el2, axis=1, keepdims=True)
    i2 = jnp.min(jnp.where((el2 == m2) & emask & (lane != i1), lane, far), axis=1, keepdims=True)
    e2 = jnp.exp(m2 - m1)
    p1 = p_grp / (1.0 + e2)
    p2 = p1 * e2
    n_rows = logits.shape[0]
    earlier = (lax.broadcasted_iota(jnp.int32, (n_rows, n_rows), 0)
               > lax.broadcasted_iota(jnp.int32, (n_rows, n_rows), 1)).astype(BF16)
    hot1 = (lane == i1).astype(F32)
    hot2 = (lane == i2).astype(F32)
    before1 = jnp.dot(earlier, hot1.astype(BF16), preferred_element_type=F32)
    before2 = jnp.dot(earlier, hot2.astype(BF16), preferred_element_type=F32)
    cnt1 = jnp.sum(hot1, axis=0, keepdims=True)
    cnt2 = jnp.sum(hot2, axis=0, keepdims=True)
    seen = seen_ref[...]
    rank1 = jnp.sum(hot1 * (seen + before1), axis=1, keepdims=True)
    rank2 = jnp.sum(hot2 * (seen + cnt1 + before2), axis=1, keepdims=True)
    seen_ref[...] = seen + cnt1 + cnt2

    out = jnp.where(lane == 0, i1, 0.0)
    for k, col in enumerate((i2, p1, p2, rank1, rank2), start=1):
        out = jnp.where(lane == k, col, out)
    return out


def _mixer_tail(x, mix_out, mod_ref, gffn_ref, wr_ref, br_ref, xnew_ref, h2_ref, route_ref, cnt_ref, seen_ref):
    @pl.when(pl.program_id(0) == 0)
    def _():
        seen_ref[...] = jnp.zeros_like(seen_ref)

    x_new = x + mod_ref[2:3, :] * mix_out
    xnew_ref[...] = x_new
    h2 = _norm_mod(x_new, gffn_ref[...], mod_ref[3:4, :], mod_ref[4:5, :]).astype(BF16)
    h2_ref[...] = _pack_halves(h2)
    route_ref[...] = _route(h2, wr_ref, br_ref, seen_ref)
    cnt_ref[...] = seen_ref[...].astype(jnp.int32)


ADA_TN = 1024


def _ada_kernel(cond_ref, w_ref, b_ref, o_ref):
    c = cond_ref[...]
    s = c * jax.nn.sigmoid(c)
    o_ref[...] = jnp.dot(s.astype(BF16), w_ref[...].astype(BF16), preferred_element_type=F32) + b_ref[...]


def _ada_call(cond, w_ada, b_ada):
    nj = (N_MOD * D_MODEL) // ADA_TN
    return pl.pallas_call(
        _ada_kernel,
        out_shape=jax.ShapeDtypeStruct((DEPTH, COND_ROWS, N_MOD * D_MODEL), F32),
        grid=(DEPTH, nj),
        in_specs=[
            pl.BlockSpec((COND_ROWS, D_MODEL), lambda l, j: (0, 0)),
            pl.BlockSpec((None, D_MODEL, ADA_TN), lambda l, j: (l, 0, j)),
            pl.BlockSpec((None, 1, ADA_TN), lambda l, j: (l, 0, j)),
        ],
        out_specs=pl.BlockSpec((None, COND_ROWS, ADA_TN), lambda l, j: (l, 0, j)),
        compiler_params=_params(40, 2),
        name="ada_mod",
    )(cond, w_ada, b_ada.reshape(DEPTH, 1, N_MOD * D_MODEL))


def _pool_band():
    col = np.arange(POOL_K)
    rel = np.where(col < TM, col, np.where(col < TM + POOL_HALO, col - TM - POOL_HALO, col - POOL_HALO))
    ok = col < TM + 2 * POOL_HALO
    row = np.arange(TM)[:, None]
    mats = []
    for w in POOL_WINDOWS:
        mats.append((rel[None, :] >= row - w // 2) & (rel[None, :] < row + (w - w // 2)) & ok[None, :])
    return np.stack(mats).astype(np.float32)


def _pool_kernel(*refs, split_streams):
    if split_streams:
        xc_ref, xl_ref, *refs = refs
    else:
        x_ref, *refs = refs
    (xp_ref, xn_ref, mod_ref, gmix_ref, gffn_ref, band_ref, wpool_ref, pscale_ref, wr_ref, br_ref,
     xnew_ref, h2_ref, route_ref, cnt_ref, seen_ref) = refs
    i = pl.program_id(0)
    lat = i >= CTX_TILES
    tile_pos = jnp.where(lat, jnp.maximum(i - CTX_TILES, 0) % LAT_TILES_PER_SEQ, 0)
    seq_len = jnp.where(lat, DEC_SEQ, SEQ)
    has_prev = tile_pos > 0
    has_next = (tile_pos + 1) * TM < seq_len

    g = gmix_ref[...]
    shift, scale = mod_ref[0:1, :], mod_ref[1:2, :]
    x = jnp.where(lat, xl_ref[...], xc_ref[...]) if split_streams else x_ref[...]
    h = _norm_mod(x, g, shift, scale)
    hp = jnp.where(has_prev, _norm_mod(xp_ref[...], g, shift, scale), 0.0)
    hn = jnp.where(has_next, _norm_mod(xn_ref[...], g, shift, scale), 0.0)
    hall = jnp.concatenate(
        [h, hp, hn, jnp.zeros((POOL_K - TM - 2 * POOL_HALO, D_MODEL), F32)], axis=0)
    hi = hall.astype(BF16)
    lo = (hall - hi.astype(F32)).astype(BF16)

    pos = tile_pos * TM + lax.broadcasted_iota(jnp.int32, (TM, 1), 0)
    outs = []
    for gi, w in enumerate(POOL_WINDOWS):
        sl = slice(gi * POOL_CH, (gi + 1) * POOL_CH)
        band = band_ref[gi]
        wsum = (jnp.dot(band, hi[:, sl], preferred_element_type=F32)
                + jnp.dot(band, lo[:, sl], preferred_element_type=F32))
        cnt = jnp.minimum(pos + (w - w // 2), seq_len) - jnp.maximum(pos - w // 2, 0)
        diff = wsum / cnt.astype(F32) - h[:, sl]
        outs.append(jnp.dot(diff.astype(BF16), wpool_ref[gi], preferred_element_type=F32))
    mix = jnp.concatenate(outs, axis=1) * pscale_ref[...]
    _mixer_tail(x, mix, mod_ref, gffn_ref, wr_ref, br_ref, xnew_ref, h2_ref, route_ref, cnt_ref, seen_ref)


def _tail_out_shapes():
    return (jax.ShapeDtypeStruct((N_TOK, D_MODEL), F32),
            jax.ShapeDtypeStruct((N_TOK, PACKED_D), jnp.uint32),
            jax.ShapeDtypeStruct((N_TOK, ROUTE_LANES), F32),
            jax.ShapeDtypeStruct((1, ROUTE_LANES), jnp.int32))


def _tail_out_specs():
    return (pl.BlockSpec((TM, D_MODEL), lambda i: (i, 0)),
            pl.BlockSpec((TM, PACKED_D), lambda i: (i, 0)),
            pl.BlockSpec((TM, ROUTE_LANES), lambda i: (i, 0)),
            pl.BlockSpec((1, ROUTE_LANES), lambda i: (0, 0)))


def _tail_scratch():
    return [pltpu.VMEM((1, ROUTE_LANES), F32)]


def _row_spec(width):
    return pl.BlockSpec((1, width), lambda i: (0, 0))


def _mod_spec(layer):
    return pl.BlockSpec((None, None, N_MOD, D_MODEL), lambda i: (layer, _mod_row(i), 0, 0))


def _pool_call(layer, xs, mod, g_mix, g_ffn, band, wpool, pscale, wr, br):
    halo_blocks = TM // POOL_HALO
    split_streams = len(xs) == 2
    tile0 = CTX_TILES if split_streams else 0
    last_halo = xs[-1].shape[0] // POOL_HALO - 1

    def prev_map(i):
        return (jnp.clip((i - tile0) * halo_blocks - 1, 0, last_halo), 0)

    def next_map(i):
        return (jnp.clip((i - tile0 + 1) * halo_blocks, 0, last_halo), 0)

    if split_streams:
        main_specs = [pl.BlockSpec((TM, D_MODEL), lambda i: (jnp.minimum(i, CTX_TILES - 1), 0)),
                      pl.BlockSpec((TM, D_MODEL), lambda i: (jnp.maximum(i - CTX_TILES, 0), 0))]
    else:
        main_specs = [pl.BlockSpec((TM, D_MODEL), lambda i: (i, 0))]
    return pl.pallas_call(
        functools.partial(_pool_kernel, split_streams=split_streams),
        out_shape=_tail_out_shapes(),
        grid=(N_TILES,),
        in_specs=main_specs + [
            pl.BlockSpec((POOL_HALO, D_MODEL), prev_map),
            pl.BlockSpec((POOL_HALO, D_MODEL), next_map),
            _mod_spec(layer),
            _row_spec(D_MODEL),
            _row_spec(D_MODEL),
            pl.BlockSpec((N_POOL_GROUPS, TM, POOL_K), lambda i: (0, 0, 0)),
            pl.BlockSpec((N_POOL_GROUPS, POOL_CH, POOL_CH), lambda i: (0, 0, 0)),
            _row_spec(D_MODEL),
            pl.BlockSpec((D_MODEL, ROUTE_LANES), lambda i: (0, 0)),
            _row_spec(ROUTE_LANES),
        ],
        out_specs=_tail_out_specs(),
        scratch_shapes=_tail_scratch(),
        compiler_params=_params(48),
        name=f"pool_layer{layer}",
    )(*xs, xs[-1], xs[-1], mod, g_mix, g_ffn, band, wpool, pscale, wr, br)


def _qkv_kernel(x_ref, mod_ref, gmix_ref, wdq_ref, gq_ref, wdkv_ref, gkv_ref, wkr_ref, wuqn_ref, wuqr_ref,
                tabq_ref, tabk_ref, qn_ref, qr_ref, ckv_ref, kr_ref):
    i = pl.program_id(0)
    h = _norm_mod(x_ref[...], gmix_ref[...], mod_ref[0:1, :], mod_ref[1:2, :]).astype(BF16)

    def rms(v, g):
        return v * lax.rsqrt(jnp.mean(v * v, axis=-1, keepdims=True) + EPS) * g

    cq = rms(jnp.dot(h, wdq_ref[...], preferred_element_type=F32), gq_ref[...]).astype(BF16)
    ckv_ref[...] = rms(jnp.dot(h, wdkv_ref[...], preferred_element_type=F32), gkv_ref[...])
    qn_ref[...] = jnp.dot(cq, wuqn_ref[...], preferred_element_type=F32).astype(BF16)
    qr = jnp.dot(cq, wuqr_ref[...], preferred_element_type=F32)
    kr = jnp.dot(h, wkr_ref[...], preferred_element_type=F32)
    lane = lax.broadcasted_iota(jnp.int32, kr.shape, 1)

    @pl.when(i < CTX_TILES)
    def _():
        qr_ref[...] = qr.astype(BF16)
        kr_ref[...] = jnp.where(lane < ROPE_DIM, kr, 0.0)

    @pl.when(i >= CTX_TILES)
    def _():
        tq = qr * tabq_ref[...]
        qr_ref[...] = (tq + pltpu.roll(tq, N_HEADS * LANES - ROPE_DIM, 1)).astype(BF16)
        tk = kr * tabk_ref[...]
        kr_ref[...] = jnp.where(lane < ROPE_DIM, tk + pltpu.roll(tk, ROPE_DIM, 1), 0.0)


def _qkv_call(layer, x, mod, g_mix, wdq, gq, wdkv, gkv, wkr, wuqn, wuqr, tabq, tabk):
    hd = N_HEADS * LANES

    def lat_tile(i):
        return jnp.maximum(i - CTX_TILES, 0) % LAT_TILES_PER_SEQ

    def full(a):
        return pl.BlockSpec(a.shape, lambda i: (0,) * a.ndim)

    return pl.pallas_call(
        _qkv_kernel,
        out_shape=(jax.ShapeDtypeStruct((N_TOK, hd), BF16),
                   jax.ShapeDtypeStruct((N_TOK, hd), BF16),
                   jax.ShapeDtypeStruct((N_TOK, KV_RANK), F32),
                   jax.ShapeDtypeStruct((N_TOK, LANES), F32)),
        grid=(N_TILES,),
        in_specs=[
            pl.BlockSpec((TM, D_MODEL), lambda i: (i, 0)),
            _mod_spec(layer),
            _row_spec(D_MODEL),
            full(wdq), _row_spec(Q_RANK), full(wdkv), _row_spec(KV_RANK), full(wkr), full(wuqn), full(wuqr),
            pl.BlockSpec((TM, hd), lambda i: (lat_tile(i), 0)),
            pl.BlockSpec((TM, LANES), lambda i: (lat_tile(i), 0)),
        ],
        out_specs=(pl.BlockSpec((TM, hd), lambda i: (i, 0)),
                   pl.BlockSpec((TM, hd), lambda i: (i, 0)),
                   pl.BlockSpec((TM, KV_RANK), lambda i: (i, 0)),
                   pl.BlockSpec((TM, LANES), lambda i: (i, 0))),
        compiler_params=_params(48),
        name=f"mla_qkv{layer}",
    )(x, mod, g_mix, wdq, gq, wdkv, gkv, wkr, wuqn, wuqr, tabq, tabk)


HEAD_K = 2 * LANES


def _attn_kernel(*refs, n_new, n_cache):
    if n_cache:
        (qn_ref, qr_ref, ckv_ref, kr_ref, cckv_ref, ckr_ref, wuk_ref, wuv_ref, _, o_ref, kcat, vs) = refs
    else:
        (qn_ref, qr_ref, ckv_ref, kr_ref, wuk_ref, wuv_ref, _, o_ref, kcat, vs) = refs
        cckv_ref = ckr_ref = None

    @pl.when(pl.program_id(1) == 0)
    def _():
        def expand(src_ref, kr_src_ref, n_rows, base):
            for c in range(n_rows // TM):
                rows = slice(c * TM, (c + 1) * TM)
                dst = slice(base + c * TM, base + (c + 1) * TM)
                lat = src_ref[rows, :].astype(BF16)
                kn = jnp.dot(lat, wuk_ref[...], preferred_element_type=F32).astype(BF16)
                vs[dst, :] = jnp.dot(lat, wuv_ref[...], preferred_element_type=F32).astype(BF16)
                krb = kr_src_ref[rows, :].astype(BF16)
                for hh in range(N_HEADS):
                    kcat[dst, hh * HEAD_K:hh * HEAD_K + LANES] = kn[:, hh * LANES:(hh + 1) * LANES]
                    kcat[dst, hh * HEAD_K + LANES:(hh + 1) * HEAD_K] = krb

        expand(ckv_ref, kr_ref, n_new, 0)
        if n_cache:
            expand(cckv_ref, ckr_ref, n_cache, n_new)

    for hh in range(N_HEADS):
        hs = slice(hh * LANES, (hh + 1) * LANES)
        q = jnp.concatenate([qn_ref[:, hs], qr_ref[:, hs]], axis=1)
        s = lax.dot_general(q, kcat[:, hh * HEAD_K:(hh + 1) * HEAD_K], (((1,), (1,)), ((), ())),
                            preferred_element_type=F32) * ATTN_SCALE
        p = jnp.exp(s - jnp.max(s, axis=1, keepdims=True))
        den = jnp.sum(p, axis=1, keepdims=True)
        o = jnp.dot(p.astype(BF16), vs[:, hs], preferred_element_type=F32) / den
        o_ref[:, hs] = o.astype(BF16)


def _attn_call(layer, qn, qr, ckv, kr, wuk, wuv, o_prev, cache=None):
    hd = N_HEADS * LANES
    if cache is None:
        n_b, n_q, n_new, n_cache, tile0 = BATCH, SEQ // TM, SEQ, 0, 0
    else:
        n_b, n_q, n_new, n_cache, tile0 = DEC_BATCH, DEC_SEQ // TM, DEC_SEQ, PAST_LEN, CTX_TILES
    seq_blk0 = tile0 * TM // n_new
    m = layer // N_MIXERS

    def q_map(b, j):
        return (tile0 + b * n_q + j, 0)

    in_specs = [
        pl.BlockSpec((TM, hd), q_map),
        pl.BlockSpec((TM, hd), q_map),
        pl.BlockSpec((n_new, KV_RANK), lambda b, j: (seq_blk0 + b, 0)),
        pl.BlockSpec((n_new, LANES), lambda b, j: (seq_blk0 + b, 0)),
    ]
    args = [qn, qr, ckv, kr]
    if cache is not None:
        in_specs += [
            pl.BlockSpec((None, None, n_cache, KV_RANK), lambda b, j: (b, m, 0, 0)),
            pl.BlockSpec((None, None, n_cache, LANES), lambda b, j: (b, m, 0, 0)),
        ]
        args += list(cache)
    in_specs += [
        pl.BlockSpec((KV_RANK, hd), lambda b, j: (0, 0)),
        pl.BlockSpec((KV_RANK, hd), lambda b, j: (0, 0)),
        pl.BlockSpec(memory_space=pl.ANY),
    ]
    args += [wuk, wuv, o_prev]
    n_keys = n_new + n_cache
    return pl.pallas_call(
        functools.partial(_attn_kernel, n_new=n_new, n_cache=n_cache),
        out_shape=jax.ShapeDtypeStruct((N_TOK, hd), BF16),
        grid=(n_b, n_q),
        in_specs=in_specs,
        out_specs=pl.BlockSpec((TM, hd), q_map),
        scratch_shapes=[pltpu.VMEM((n_keys, N_HEADS * HEAD_K), BF16), pltpu.VMEM((n_keys, hd), BF16)],
        input_output_aliases={len(args) - 1: 0},
        compiler_params=_params(56, 2),
        name=f"mla_attn{layer}_{'lat' if cache is not None else 'ctx'}",
    )(*args)


def _attn_out_kernel(o_ref, wo_ref, x_ref, mod_ref, gffn_ref, wr_ref, br_ref, xnew_ref, h2_ref, route_ref,
                     cnt_ref, seen_ref):
    mix = jnp.dot(o_ref[...], wo_ref[...], preferred_element_type=F32)
    _mixer_tail(x_ref[...], mix, mod_ref, gffn_ref, wr_ref, br_ref, xnew_ref, h2_ref, route_ref, cnt_ref,
                seen_ref)


def _attn_out_call(layer, o, wo, x, mod, g_ffn, wr, br):
    hd = N_HEADS * V_DIM
    return pl.pallas_call(
        _attn_out_kernel,
        out_shape=_tail_out_shapes(),
        grid=(N_TILES,),
        in_specs=[
            pl.BlockSpec((TM, hd), lambda i: (i, 0)),
            pl.BlockSpec((hd, D_MODEL), lambda i: (0, 0)),
            pl.BlockSpec((TM, D_MODEL), lambda i: (i, 0)),
            _mod_spec(layer),
            _row_spec(D_MODEL),
            pl.BlockSpec((D_MODEL, ROUTE_LANES), lambda i: (0, 0)),
            _row_spec(ROUTE_LANES),
        ],
        out_specs=_tail_out_specs(),
        scratch_shapes=_tail_scratch(),
        compiler_params=_params(48),
        name=f"mla_out{layer}",
    )(o, wo, x, mod, g_ffn, wr, br)


def _plan_kernel(cnt_ref, route_ref, pos_ref, wt_ref, we_ref, wlo_ref, whi_ref, nw_ref, start_ref):
    @pl.when(pl.program_id(0) == 0)
    def _():
        lane = lax.broadcasted_iota(jnp.int32, (1, ROUTE_LANES), 1)
        shift = TM.bit_length() - 1

        def per_expert(e, carry):
            n_work, lo, last_e, starts = carry
            cnt = cnt_ref[0, e]
            hi = lo + cnt
            first_tile = lax.shift_right_logical(lo, shift)
            end_tile = jnp.where(cnt > 0, lax.shift_right_logical(hi + (TM - 1), shift), first_tile)

            def per_tile(t, w):
                wt_ref[w] = t
                we_ref[w] = e
                wlo_ref[w] = lo
                whi_ref[w] = hi
                return w + 1

            n_work = lax.fori_loop(first_tile, end_tile, per_tile, n_work)
            starts = jnp.where(lane == e, lo.astype(F32), starts)
            return n_work, hi, jnp.where(cnt > 0, e, last_e), starts

        n_work, _, last_e, starts = lax.fori_loop(
            0, N_EXPERTS, per_expert,
            (jnp.int32(0), jnp.int32(0), jnp.int32(0), jnp.zeros((1, ROUTE_LANES), F32)))
        nw_ref[0] = n_work
        start_ref[...] = starts

        def fill(w, carry):
            wt_ref[w] = SORT_TILES - 1
            we_ref[w] = last_e
            wlo_ref[w] = 0
            whi_ref[w] = 0
            return carry

        lax.fori_loop(n_work, MAX_WORK, fill, 0)

    r = route_ref[...]
    lane = lax.broadcasted_iota(jnp.int32, r.shape, 1).astype(F32)
    starts = start_ref[...]
    pos1 = jnp.sum(jnp.where(lane == r[:, 0:1], starts, 0.0), axis=1, keepdims=True) + r[:, 4:5]
    pos2 = jnp.sum(jnp.where(lane == r[:, 1:2], starts, 0.0), axis=1, keepdims=True) + r[:, 5:6]
    pos = jnp.where(lane == 0.0, pos1, jnp.where(lane == 1.0, pos2, 0.0))
    pos_ref[...] = pos.T[0:SUBLANES, :].astype(jnp.int32)


def _plan_call(layer, counts, route):
    smem = pl.BlockSpec(memory_space=pltpu.SMEM)
    work = jax.ShapeDtypeStruct((MAX_WORK,), jnp.int32)
    return pl.pallas_call(
        _plan_kernel,
        out_shape=(jax.ShapeDtypeStruct((SUBLANES, N_TOK), jnp.int32), work, work, work, work,
                   jax.ShapeDtypeStruct((1,), jnp.int32)),
        grid=(N_TILES,),
        in_specs=[smem, pl.BlockSpec((TM, ROUTE_LANES), lambda i: (i, 0))],
        out_specs=(pl.BlockSpec((SUBLANES, TM), lambda i: (0, i)), smem, smem, smem, smem, smem),
        scratch_shapes=[pltpu.VMEM((1, ROUTE_LANES), F32)],
        compiler_params=_params(16),
        name=f"moe_plan{layer}",
    )(counts, route)


DISPATCH_BUFS = 3


def _dispatch_kernel(pos_ref, h2_hbm, xs_hbm, buf, sem_in, sem_out):
    i = pl.program_id(0)

    def load(tile):
        slot = tile % DISPATCH_BUFS
        return pltpu.make_async_copy(h2_hbm.at[pl.ds(tile * TM, TM)], buf.at[slot], sem_in.at[slot])

    def drain_scatter(tile):
        slot = tile % DISPATCH_BUFS
        for _ in range(2):
            pltpu.make_async_copy(buf.at[slot], xs_hbm.at[pl.ds(0, TM)], sem_out.at[slot]).wait()

    @pl.when(i == 0)
    def _():
        load(i).start()

    @pl.when(i >= DISPATCH_BUFS - 1)
    def _():
        drain_scatter(i - (DISPATCH_BUFS - 1))

    @pl.when(i + 1 < N_TILES)
    def _():
        load(i + 1).start()

    load(i).wait()
    slot = i % DISPATCH_BUFS

    def issue(r, carry):
        t = i * TM + r
        src = buf.at[slot, pl.ds(r, 1)]
        pltpu.make_async_copy(src, xs_hbm.at[pl.ds(pos_ref[t], 1)], sem_out.at[slot]).start()
        pltpu.make_async_copy(src, xs_hbm.at[pl.ds(pos_ref[N_TOK + t], 1)], sem_out.at[slot]).start()
        return carry

    lax.fori_loop(0, TM, issue, 0, unroll=8)

    @pl.when(i == N_TILES - 1)
    def _():
        for back in range(DISPATCH_BUFS - 2, -1, -1):
            drain_scatter(i - back)


def _dispatch_call(layer, pos, h2):
    return pl.pallas_call(
        _dispatch_kernel,
        out_shape=jax.ShapeDtypeStruct((N_SLOTS, PACKED_D), jnp.uint32),
        grid_spec=pltpu.PrefetchScalarGridSpec(
            num_scalar_prefetch=1,
            grid=(N_TILES,),
            in_specs=[pl.BlockSpec(memory_space=pl.ANY)],
            out_specs=pl.BlockSpec(memory_space=pl.ANY),
            scratch_shapes=[
                pltpu.VMEM((DISPATCH_BUFS, TM, PACKED_D), jnp.uint32),
                pltpu.SemaphoreType.DMA((DISPATCH_BUFS,)),
                pltpu.SemaphoreType.DMA((DISPATCH_BUFS,)),
            ],
        ),
        compiler_params=_params(16),
        name=f"moe_dispatch{layer}",
    )(pos, h2)


def _gmm_kernel(wt_ref, we_ref, wlo_ref, whi_ref, nw_ref, xs_ref, wg_ref, wu_ref, wd_ref, ys_ref,
                wg_s, wu_s, wd_s):
    w = pl.program_id(0)
    prev = jnp.maximum(w - 1, 0)
    new_expert = (w == 0) | (we_ref[w] != we_ref[prev])
    new_tile = (w == 0) | (wt_ref[w] != wt_ref[prev])

    @pl.when(w < nw_ref[0])
    def _():
        @pl.when(new_expert)
        def _():
            wg_s[...] = wg_ref[...].astype(BF16)
            wu_s[...] = wu_ref[...].astype(BF16)
            wd_s[...] = wd_ref[...].astype(BF16)

        x = _unpack_halves(xs_ref[...])
        gate = jnp.dot(x, wg_s[...], preferred_element_type=F32)
        up = jnp.dot(x, wu_s[...], preferred_element_type=F32)
        row = wt_ref[w] * TM + lax.broadcasted_iota(jnp.int32, (TM, 1), 0)
        mine = (row >= wlo_ref[w]) & (row < whi_ref[w])
        hid = jnp.where(mine, (gate * jax.nn.sigmoid(gate)) * up, 0.0)
        y = jnp.dot(hid.astype(BF16), wd_s[...], preferred_element_type=F32)

        @pl.when(new_tile)
        def _():
            ys_ref[...] = y

        @pl.when(jnp.logical_not(new_tile))
        def _():
            ys_ref[...] += y


def _gmm_call(layer, xs, work, w_gate, w_up, w_down):
    def w_spec(shape):
        return pl.BlockSpec((None, None) + shape, lambda w, wt, we, wlo, whi, nw: (layer, we[w], 0, 0))

    def tile_spec(width=D_MODEL):
        return pl.BlockSpec((TM, width), lambda w, wt, we, wlo, whi, nw: (wt[w], 0))

    grid_spec = pltpu.PrefetchScalarGridSpec(
        num_scalar_prefetch=5,
        grid=(MAX_WORK,),
        in_specs=[
            tile_spec(PACKED_D),
            w_spec((D_MODEL, EXPERT_HIDDEN)),
            w_spec((D_MODEL, EXPERT_HIDDEN)),
            w_spec((EXPERT_HIDDEN, D_MODEL)),
        ],
        out_specs=tile_spec(),
        scratch_shapes=[
            pltpu.VMEM((D_MODEL, EXPERT_HIDDEN), BF16),
            pltpu.VMEM((D_MODEL, EXPERT_HIDDEN), BF16),
            pltpu.VMEM((EXPERT_HIDDEN, D_MODEL), BF16),
        ],
    )
    return pl.pallas_call(
        _gmm_kernel,
        out_shape=jax.ShapeDtypeStruct((N_SLOTS, D_MODEL), F32),
        grid_spec=grid_spec,
        compiler_params=_params(48),
        name=f"moe_gmm{layer}",
    )(*work, xs, w_gate, w_up, w_down)


def _combine_kernel(pos_ref, x_ref, route_ref, mod_ref, gfin_ref, ys_hbm, *rest, final):
    if final:
        octx_ref, olat_ref, buf1, buf2, sem1, sem2 = rest
    else:
        o_ref, buf1, buf2, sem1, sem2 = rest
    i = pl.program_id(0)

    def fetch(tile):
        slot = tile % 2

        def issue(r, carry):
            t = tile * TM + r
            pltpu.make_async_copy(ys_hbm.at[pl.ds(pos_ref[t], 1)], buf1.at[slot, pl.ds(r, 1)],
                                  sem1.at[slot]).start()
            pltpu.make_async_copy(ys_hbm.at[pl.ds(pos_ref[N_TOK + t], 1)], buf2.at[slot, pl.ds(r, 1)],
                                  sem2.at[slot]).start()
            return carry

        lax.fori_loop(0, TM, issue, 0, unroll=8)

    @pl.when(i == 0)
    def _():
        fetch(i)

    @pl.when(i + 1 < N_TILES)
    def _():
        fetch(i + 1)

    slot = i % 2
    pltpu.make_async_copy(ys_hbm.at[pl.ds(0, TM)], buf1.at[slot], sem1.at[slot]).wait()
    pltpu.make_async_copy(ys_hbm.at[pl.ds(0, TM)], buf2.at[slot], sem2.at[slot]).wait()
    r = route_ref[...]
    moe = r[:, 2:3] * buf1[slot] + r[:, 3:4] * buf2[slot]
    x = x_ref[...] + mod_ref[5:6, :] * moe
    if not final:
        o_ref[...] = x
        return
    y = x * lax.rsqrt(jnp.mean(x * x, axis=-1, keepdims=True) + EPS) * gfin_ref[...]

    @pl.when(i < CTX_TILES)
    def _():
        octx_ref[...] = y

    @pl.when(i >= CTX_TILES)
    def _():
        olat_ref[...] = y


def _combine_call(layer, pos, x_new, route, ys, mod, g_final, final):
    def tok_spec(width):
        return pl.BlockSpec((TM, width), lambda i, pos: (i, 0))

    if final:
        out_shape = (jax.ShapeDtypeStruct((N_CTX, D_MODEL), F32), jax.ShapeDtypeStruct((N_LAT, D_MODEL), F32))
        out_specs = (pl.BlockSpec((TM, D_MODEL), lambda i, pos: (jnp.minimum(i, CTX_TILES - 1), 0)),
                     pl.BlockSpec((TM, D_MODEL), lambda i, pos: (jnp.maximum(i - CTX_TILES, 0), 0)))
    else:
        out_shape = jax.ShapeDtypeStruct((N_TOK, D_MODEL), F32)
        out_specs = tok_spec(D_MODEL)

    grid_spec = pltpu.PrefetchScalarGridSpec(
        num_scalar_prefetch=1,
        grid=(N_TILES,),
        in_specs=[
            tok_spec(D_MODEL),
            tok_spec(ROUTE_LANES),
            pl.BlockSpec((None, None, N_MOD, D_MODEL), lambda i, pos: (layer, _mod_row(i), 0, 0)),
            pl.BlockSpec((1, D_MODEL), lambda i, pos: (0, 0)),
            pl.BlockSpec(memory_space=pl.ANY),
        ],
        out_specs=out_specs,
        scratch_shapes=[
            pltpu.VMEM((2, TM, D_MODEL), F32),
            pltpu.VMEM((2, TM, D_MODEL), F32),
            pltpu.SemaphoreType.DMA((2,)),
            pltpu.SemaphoreType.DMA((2,)),
        ],
    )
    return pl.pallas_call(
        functools.partial(_combine_kernel, final=final),
        out_shape=out_shape,
        grid_spec=grid_spec,
        compiler_params=_params(40),
        name=f"moe_combine{layer}",
    )(pos, x_new, route, mod, g_final, ys)


def _rope_tables():
    rows = DEC_SEQ // GRID_W
    row_ids = jnp.repeat(jnp.arange(rows), GRID_W).astype(F32)
    col_ids = jnp.tile(jnp.arange(GRID_W), rows).astype(F32)
    freqs = ROPE_THETA ** (-jnp.arange(ROPE_QUARTER, dtype=F32) / ROPE_QUARTER)
    ar, ac = row_ids[:, None] * freqs[None, :], col_ids[:, None] * freqs[None, :]
    cos = jnp.concatenate([jnp.cos(ar), jnp.cos(ar), jnp.cos(ac), jnp.cos(ac)], axis=1)
    sin = jnp.concatenate([-jnp.sin(ar), jnp.sin(ar), -jnp.sin(ac), jnp.sin(ac)], axis=1)
    return jnp.concatenate([cos, sin], axis=1)


def _swap_partners(w):
    q = ROPE_QUARTER
    return jnp.concatenate([w[..., q:2 * q], w[..., 0:q], w[..., 3 * q:4 * q], w[..., 2 * q:3 * q]], axis=-1)


def kernel(x_prompt, x_sample, cache_ckv, cache_krope, c, c_ctx, g_mix, g_ffn, w_ada, b_ada, w_pool, pool_scale,
           w_dq, g_q, w_uq, w_dkv, g_kv, w_uk, w_uv, w_o, w_router_grp, b_router_grp, w_router_exp,
           b_router_exp, w_gate, w_up, w_down, g_final):
    n_mla = DEPTH // N_MIXERS
    x = (x_prompt.reshape(N_CTX, D_MODEL), x_sample.reshape(N_LAT, D_MODEL))

    cond = jnp.concatenate([c_ctx[None, :], c, jnp.zeros((COND_ROWS - 1 - DEC_BATCH, D_MODEL), F32)], axis=0)
    mod = _ada_call(cond, w_ada, b_ada).reshape(DEPTH, COND_ROWS, N_MOD, D_MODEL)

    wr = jnp.concatenate([w_router_exp, w_router_grp,
                          jnp.zeros((DEPTH, D_MODEL, ROUTE_LANES - N_EXPERTS - N_EXPERT_GROUPS), F32)],
                         axis=2).astype(BF16)
    br = jnp.concatenate([b_router_exp, b_router_grp,
                          jnp.zeros((DEPTH, ROUTE_LANES - N_EXPERTS - N_EXPERT_GROUPS), F32)], axis=1)
    band = jnp.asarray(_pool_band(), dtype=BF16)
    wpool = w_pool.astype(BF16)
    wdq = w_dq.astype(BF16)
    wdkv = w_dkv[:, :, :KV_RANK].astype(BF16)
    w_kr = w_dkv[:, :, KV_RANK:]
    wkr = jnp.concatenate([w_kr, _swap_partners(w_kr)], axis=-1).astype(BF16)
    wuqn = w_uq[..., :NOPE_DIM].reshape(n_mla, Q_RANK, N_HEADS * NOPE_DIM).astype(BF16)
    w_qr = w_uq[..., NOPE_DIM:]
    wuqr = jnp.concatenate([w_qr, _swap_partners(w_qr)], axis=-1).reshape(n_mla, Q_RANK, N_HEADS * LANES)
    wuqr = wuqr.astype(BF16)
    wuk = w_uk.reshape(n_mla, KV_RANK, N_HEADS * NOPE_DIM).astype(BF16)
    wuv = w_uv.reshape(n_mla, KV_RANK, N_HEADS * V_DIM).astype(BF16)
    wo = w_o.astype(BF16)
    tabk = _rope_tables()
    tabq = jnp.tile(tabk, (1, N_HEADS))
    cache_kr = jnp.pad(cache_krope, ((0, 0), (0, 0), (0, 0), (0, LANES - ROPE_DIM)))

    def row(a):
        return a.reshape(1, -1)

    ckv_ctx, kr_ctx = [], []
    for layer in range(DEPTH):
        if layer % N_MIXERS == 0:
            p = layer // N_MIXERS
            streams = x if isinstance(x, tuple) else (x,)
            x_new, h2, route, counts = _pool_call(layer, streams, mod, row(g_mix[layer]), row(g_ffn[layer]),
                                                  band, wpool[p], row(pool_scale[p]), wr[layer], row(br[layer]))
        else:
            m = layer // N_MIXERS
            qn, qr, ckv, kr = _qkv_call(layer, x, mod, row(g_mix[layer]), wdq[m], row(g_q[m]), wdkv[m],
                                        row(g_kv[m]), wkr[m], wuqn[m], wuqr[m], tabq, tabk)
            ckv_ctx.append(ckv[:N_CTX].reshape(BATCH, SEQ, KV_RANK))
            kr_ctx.append(kr[:N_CTX, :ROPE_DIM].reshape(BATCH, SEQ, ROPE_DIM))
            o = jnp.zeros((N_TOK, N_HEADS * V_DIM), BF16)
            o = _attn_call(layer, qn, qr, ckv, kr, wuk[m], wuv[m], o)
            o = _attn_call(layer, qn, qr, ckv, kr, wuk[m], wuv[m], o, cache=(cache_ckv, cache_kr))
            x_new, h2, route, counts = _attn_out_call(layer, o, wo[m], x, mod, row(g_ffn[layer]), wr[layer],
                                                      row(br[layer]))
        pos8, *work = _plan_call(layer, counts, route)
        pos = pos8[:2].reshape(N_SLOTS)
        xs = _dispatch_call(layer, pos, h2)
        ys = _gmm_call(layer, xs, work, w_gate, w_up, w_down)
        x = _combine_call(layer, pos, x_new, route, ys, mod, row(g_final), final=(layer == DEPTH - 1))

    y_prompt = x[0].reshape(BATCH, SEQ, D_MODEL)
    y_sample = x[1].reshape(DEC_BATCH, DEC_SEQ, D_MODEL)
    return y_prompt, y_sample, jnp.stack(ckv_ctx, axis=1), jnp.stack(kr_ctx, axis=1)
```

```python
import functools

import numpy as np
import jax
import jax.numpy as jnp
from jax import lax
from jax.experimental import pallas as pl
from jax.experimental.pallas import tpu as pltpu

D_MODEL = 2048
BATCH = 32
SEQ = 256
DEPTH = 4
DEC_BATCH = 2
DEC_SEQ = 1024
PAST_LEN = 512
GRID_W = 64
N_MIXERS = 2
POOL_WINDOWS = (2, 4, 8, 16)
N_POOL_GROUPS = 4
POOL_CH = D_MODEL // N_POOL_GROUPS
N_HEADS = 16
Q_RANK = 512
KV_RANK = 512
NOPE_DIM = 128
ROPE_DIM = 64
V_DIM = 128
ROPE_QUARTER = ROPE_DIM // 4
ROPE_THETA = 10000.0
ATTN_SCALE = (NOPE_DIM + ROPE_DIM) ** -0.5
N_EXPERT_GROUPS = 4
EXPERTS_PER_GROUP = 8
N_EXPERTS = N_EXPERT_GROUPS * EXPERTS_PER_GROUP
EXPERT_HIDDEN = 512
N_MOD = 6
EPS = 1e-6

LANES = 128
SUBLANES = 8
TM = 256
N_CTX = BATCH * SEQ
N_LAT = DEC_BATCH * DEC_SEQ
N_TOK = N_CTX + N_LAT
CTX_TILES = N_CTX // TM
LAT_TILES = N_LAT // TM
N_TILES = CTX_TILES + LAT_TILES
LAT_TILES_PER_SEQ = DEC_SEQ // TM
COND_ROWS = SUBLANES
POOL_HALO = max(POOL_WINDOWS) // 2
POOL_K = TM + LANES
ROUTE_LANES = LANES
N_SLOTS = 2 * N_TOK
SORT_TILES = N_SLOTS // TM
MAX_WORK = SORT_TILES + N_EXPERTS - 1
MIB = 1024 * 1024

F32 = jnp.float32
BF16 = jnp.bfloat16


def _mod_row(tile):
    return jnp.where(tile < CTX_TILES, 0, 1 + jnp.maximum(tile - CTX_TILES, 0) // LAT_TILES_PER_SEQ)


def _params(vmem_mib, n_axes=1):
    return pltpu.CompilerParams(
        dimension_semantics=("arbitrary",) * n_axes,
        vmem_limit_bytes=vmem_mib * MIB,
    )


def _norm_mod(x, g, shift, scale):
    y = x * lax.rsqrt(jnp.mean(x * x, axis=-1, keepdims=True) + EPS)
    return (y * g) * (1.0 + scale) + shift


def _route(h_bf, wr_ref, br_ref, seen_ref):
    logits = jnp.dot(h_bf, wr_ref[...], preferred_element_type=F32) + br_ref[...]
    lane_i = lax.broadcasted_iota(jnp.int32, logits.shape, 1)
    lane = lane_i.astype(F32)
    neg = -jnp.inf
    far = float(ROUTE_LANES)
    gmask = (lane_i >= N_EXPERTS) & (lane_i < N_EXPERTS + N_EXPERT_GROUPS)
    gl = jnp.where(gmask, logits, neg)
    gmax = jnp.max(gl, axis=1, keepdims=True)
    grp = jnp.min(jnp.where(gl == gmax, lane, far), axis=1, keepdims=True) - N_EXPERTS
    p_grp = 1.0 / jnp.sum(jnp.exp(gl - gmax), axis=1, keepdims=True)
    lane_grp = lax.shift_right_logical(lane_i, EXPERTS_PER_GROUP.bit_length() - 1).astype(F32)
    emask = lane_grp == grp
    el = jnp.where(emask, logits, neg)
    m1 = jnp.max(el, axis=1, keepdims=True)
    i1 = jnp.min(jnp.where(el == m1, lane, far), axis=1, keepdims=True)
    el2 = jnp.where(lane == i1, neg, el)
    m2 = jnp.max(el2, axis=1, keepdims=True)
    i2 = jnp.min(jnp.where((el2 == m2) & emask & (lane != i1), lane, far), axis=1, keepdims=True)
    e2 = jnp.exp(m2 - m1)
    p1 = p_grp / (1.0 + e2)
    p2 = p1 * e2
    n_rows = logits.shape[0]
    earlier = (lax.broadcasted_iota(jnp.int32, (n_rows, n_rows), 0)
               > lax.broadcasted_iota(jnp.int32, (n_rows, n_rows), 1)).astype(BF16)
    hot1 = (lane == i1).astype(F32)
    hot2 = (lane == i2).astype(F32)
    before1 = jnp.dot(earlier, hot1.astype(BF16), preferred_element_type=F32)
    before2 = jnp.dot(earlier, hot2.astype(BF16), preferred_element_type=F32)
    cnt1 = jnp.sum(hot1, axis=0, keepdims=True)
    cnt2 = jnp.sum(hot2, axis=0, keepdims=True)
    seen = seen_ref[...]
    rank1 = jnp.sum(hot1 * (seen + before1), axis=1, keepdims=True)
    rank2 = jnp.sum(hot2 * (seen + cnt1 + before2), axis=1, keepdims=True)
    seen_ref[...] = seen + cnt1 + cnt2

    out = jnp.where(lane == 0, i1, 0.0)
    for k, col in enumerate((i2, p1, p2, rank1, rank2), start=1):
        out = jnp.where(lane == k, col, out)
    return out


def _mixer_tail(x, mix_out, mod_ref, gffn_ref, wr_ref, br_ref, xnew_ref, h2_ref, route_ref, cnt_ref, seen_ref):
    @pl.when(pl.program_id(0) == 0)
    def _():
        seen_ref[...] = jnp.zeros_like(seen_ref)

    x_new = x + mod_ref[2:3, :] * mix_out
    xnew_ref[...] = x_new
    h2 = _norm_mod(x_new, gffn_ref[...], mod_ref[3:4, :], mod_ref[4:5, :])
    h2_ref[...] = h2
    route_ref[...] = _route(h2.astype(BF16), wr_ref, br_ref, seen_ref)
    cnt_ref[...] = seen_ref[...].astype(jnp.int32)


ADA_TN = 1024


def _ada_kernel(cond_ref, w_ref, b_ref, o_ref):
    c = cond_ref[...]
    s = c * jax.nn.sigmoid(c)
    o_ref[...] = jnp.dot(s.astype(BF16), w_ref[...].astype(BF16), preferred_element_type=F32) + b_ref[...]


def _ada_call(cond, w_ada, b_ada):
    nj = (N_MOD * D_MODEL) // ADA_TN
    return pl.pallas_call(
        _ada_kernel,
        out_shape=jax.ShapeDtypeStruct((DEPTH, COND_ROWS, N_MOD * D_MODEL), F32),
        grid=(DEPTH, nj),
        in_specs=[
            pl.BlockSpec((COND_ROWS, D_MODEL), lambda l, j: (0, 0)),
            pl.BlockSpec((None, D_MODEL, ADA_TN), lambda l, j: (l, 0, j)),
            pl.BlockSpec((None, 1, ADA_TN), lambda l, j: (l, 0, j)),
        ],
        out_specs=pl.BlockSpec((None, COND_ROWS, ADA_TN), lambda l, j: (l, 0, j)),
        compiler_params=_params(40, 2),
        name="ada_mod",
    )(cond, w_ada, b_ada.reshape(DEPTH, 1, N_MOD * D_MODEL))


def _pool_band():
    col = np.arange(POOL_K)
    rel = np.where(col < TM, col, np.where(col < TM + POOL_HALO, col - TM - POOL_HALO, col - POOL_HALO))
    ok = col < TM + 2 * POOL_HALO
    row = np.arange(TM)[:, None]
    mats = []
    for w in POOL_WINDOWS:
        mats.append((rel[None, :] >= row - w // 2) & (rel[None, :] < row + (w - w // 2)) & ok[None, :])
    return np.stack(mats).astype(np.float32)


def _pool_kernel(*refs, split_streams):
    if split_streams:
        xc_ref, xl_ref, *refs = refs
    else:
        x_ref, *refs = refs
    (xp_ref, xn_ref, mod_ref, gmix_ref, gffn_ref, band_ref, wpool_ref, pscale_ref, wr_ref, br_ref,
     xnew_ref, h2_ref, route_ref, cnt_ref, seen_ref) = refs
    i = pl.program_id(0)
    lat = i >= CTX_TILES
    tile_pos = jnp.where(lat, jnp.maximum(i - CTX_TILES, 0) % LAT_TILES_PER_SEQ, 0)
    seq_len = jnp.where(lat, DEC_SEQ, SEQ)
    has_prev = tile_pos > 0
    has_next = (tile_pos + 1) * TM < seq_len

    g = gmix_ref[...]
    shift, scale = mod_ref[0:1, :], mod_ref[1:2, :]
    x = jnp.where(lat, xl_ref[...], xc_ref[...]) if split_streams else x_ref[...]
    h = _norm_mod(x, g, shift, scale)
    hp = jnp.where(has_prev, _norm_mod(xp_ref[...], g, shift, scale), 0.0)
    hn = jnp.where(has_next, _norm_mod(xn_ref[...], g, shift, scale), 0.0)
    hall = jnp.concatenate(
        [h, hp, hn, jnp.zeros((POOL_K - TM - 2 * POOL_HALO, D_MODEL), F32)], axis=0)
    hi = hall.astype(BF16)
    lo = (hall - hi.astype(F32)).astype(BF16)

    pos = tile_pos * TM + lax.broadcasted_iota(jnp.int32, (TM, 1), 0)
    outs = []
    for gi, w in enumerate(POOL_WINDOWS):
        sl = slice(gi * POOL_CH, (gi + 1) * POOL_CH)
        band = band_ref[gi]
        wsum = (jnp.dot(band, hi[:, sl], preferred_element_type=F32)
                + jnp.dot(band, lo[:, sl], preferred_element_type=F32))
        cnt = jnp.minimum(pos + (w - w // 2), seq_len) - jnp.maximum(pos - w // 2, 0)
        diff = wsum / cnt.astype(F32) - h[:, sl]
        outs.append(jnp.dot(diff.astype(BF16), wpool_ref[gi], preferred_element_type=F32))
    mix = jnp.concatenate(outs, axis=1) * pscale_ref[...]
    _mixer_tail(x, mix, mod_ref, gffn_ref, wr_ref, br_ref, xnew_ref, h2_ref, route_ref, cnt_ref, seen_ref)


def _tail_out_shapes():
    return (jax.ShapeDtypeStruct((N_TOK, D_MODEL), F32),
            jax.ShapeDtypeStruct((N_TOK, D_MODEL), F32),
            jax.ShapeDtypeStruct((N_TOK, ROUTE_LANES), F32),
            jax.ShapeDtypeStruct((1, ROUTE_LANES), jnp.int32))


def _tail_out_specs():
    return (pl.BlockSpec((TM, D_MODEL), lambda i: (i, 0)),
            pl.BlockSpec((TM, D_MODEL), lambda i: (i, 0)),
            pl.BlockSpec((TM, ROUTE_LANES), lambda i: (i, 0)),
            pl.BlockSpec((1, ROUTE_LANES), lambda i: (0, 0)))


def _tail_scratch():
    return [pltpu.VMEM((1, ROUTE_LANES), F32)]


def _row_spec(width):
    return pl.BlockSpec((1, width), lambda i: (0, 0))


def _mod_spec(layer):
    return pl.BlockSpec((None, None, N_MOD, D_MODEL), lambda i: (layer, _mod_row(i), 0, 0))


def _pool_call(layer, xs, mod, g_mix, g_ffn, band, wpool, pscale, wr, br):
    halo_blocks = TM // POOL_HALO
    split_streams = len(xs) == 2
    tile0 = CTX_TILES if split_streams else 0
    last_halo = xs[-1].shape[0] // POOL_HALO - 1

    def prev_map(i):
        return (jnp.clip((i - tile0) * halo_blocks - 1, 0, last_halo), 0)

    def next_map(i):
        return (jnp.clip((i - tile0 + 1) * halo_blocks, 0, last_halo), 0)

    if split_streams:
        main_specs = [pl.BlockSpec((TM, D_MODEL), lambda i: (jnp.minimum(i, CTX_TILES - 1), 0)),
                      pl.BlockSpec((TM, D_MODEL), lambda i: (jnp.maximum(i - CTX_TILES, 0), 0))]
    else:
        main_specs = [pl.BlockSpec((TM, D_MODEL), lambda i: (i, 0))]
    return pl.pallas_call(
        functools.partial(_pool_kernel, split_streams=split_streams),
        out_shape=_tail_out_shapes(),
        grid=(N_TILES,),
        in_specs=main_specs + [
            pl.BlockSpec((POOL_HALO, D_MODEL), prev_map),
            pl.BlockSpec((POOL_HALO, D_MODEL), next_map),
            _mod_spec(layer),
            _row_spec(D_MODEL),
            _row_spec(D_MODEL),
            pl.BlockSpec((N_POOL_GROUPS, TM, POOL_K), lambda i: (0, 0, 0)),
            pl.BlockSpec((N_POOL_GROUPS, POOL_CH, POOL_CH), lambda i: (0, 0, 0)),
            _row_spec(D_MODEL),
            pl.BlockSpec((D_MODEL, ROUTE_LANES), lambda i: (0, 0)),
            _row_spec(ROUTE_LANES),
        ],
        out_specs=_tail_out_specs(),
        scratch_shapes=_tail_scratch(),
        compiler_params=_params(48),
        name=f"pool_layer{layer}",
    )(*xs, xs[-1], xs[-1], mod, g_mix, g_ffn, band, wpool, pscale, wr, br)


def _qkv_kernel(x_ref, mod_ref, gmix_ref, wdq_ref, gq_ref, wdkv_ref, gkv_ref, wkr_ref, wuqn_ref, wuqr_ref,
                tabq_ref, tabk_ref, qn_ref, qr_ref, ckv_ref, kr_ref):
    i = pl.program_id(0)
    h = _norm_mod(x_ref[...], gmix_ref[...], mod_ref[0:1, :], mod_ref[1:2, :]).astype(BF16)

    def rms(v, g):
        return v * lax.rsqrt(jnp.mean(v * v, axis=-1, keepdims=True) + EPS) * g

    cq = rms(jnp.dot(h, wdq_ref[...], preferred_element_type=F32), gq_ref[...]).astype(BF16)
    ckv_ref[...] = rms(jnp.dot(h, wdkv_ref[...], preferred_element_type=F32), gkv_ref[...])
    qn_ref[...] = jnp.dot(cq, wuqn_ref[...], preferred_element_type=F32).astype(BF16)
    qr = jnp.dot(cq, wuqr_ref[...], preferred_element_type=F32)
    kr = jnp.dot(h, wkr_ref[...], preferred_element_type=F32)
    lane = lax.broadcasted_iota(jnp.int32, kr.shape, 1)

    @pl.when(i < CTX_TILES)
    def _():
        qr_ref[...] = qr.astype(BF16)
        kr_ref[...] = jnp.where(lane < ROPE_DIM, kr, 0.0)

    @pl.when(i >= CTX_TILES)
    def _():
        tq = qr * tabq_ref[...]
        qr_ref[...] = (tq + pltpu.roll(tq, N_HEADS * LANES - ROPE_DIM, 1)).astype(BF16)
        tk = kr * tabk_ref[...]
        kr_ref[...] = jnp.where(lane < ROPE_DIM, tk + pltpu.roll(tk, ROPE_DIM, 1), 0.0)


def _qkv_call(layer, x, mod, g_mix, wdq, gq, wdkv, gkv, wkr, wuqn, wuqr, tabq, tabk):
    hd = N_HEADS * LANES

    def lat_tile(i):
        return jnp.maximum(i - CTX_TILES, 0) % LAT_TILES_PER_SEQ

    def full(a):
        return pl.BlockSpec(a.shape, lambda i: (0,) * a.ndim)

    return pl.pallas_call(
        _qkv_kernel,
        out_shape=(jax.ShapeDtypeStruct((N_TOK, hd), BF16),
                   jax.ShapeDtypeStruct((N_TOK, hd), BF16),
                   jax.ShapeDtypeStruct((N_TOK, KV_RANK), F32),
                   jax.ShapeDtypeStruct((N_TOK, LANES), F32)),
        grid=(N_TILES,),
        in_specs=[
            pl.BlockSpec((TM, D_MODEL), lambda i: (i, 0)),
            _mod_spec(layer),
            _row_spec(D_MODEL),
            full(wdq), _row_spec(Q_RANK), full(wdkv), _row_spec(KV_RANK), full(wkr), full(wuqn), full(wuqr),
            pl.BlockSpec((TM, hd), lambda i: (lat_tile(i), 0)),
            pl.BlockSpec((TM, LANES), lambda i: (lat_tile(i), 0)),
        ],
        out_specs=(pl.BlockSpec((TM, hd), lambda i: (i, 0)),
                   pl.BlockSpec((TM, hd), lambda i: (i, 0)),
                   pl.BlockSpec((TM, KV_RANK), lambda i: (i, 0)),
                   pl.BlockSpec((TM, LANES), lambda i: (i, 0))),
        compiler_params=_params(48),
        name=f"mla_qkv{layer}",
    )(x, mod, g_mix, wdq, gq, wdkv, gkv, wkr, wuqn, wuqr, tabq, tabk)


HEAD_K = 2 * LANES


def _attn_kernel(*refs, n_new, n_cache):
    if n_cache:
        (qn_ref, qr_ref, ckv_ref, kr_ref, cckv_ref, ckr_ref, wuk_ref, wuv_ref, _, o_ref, kcat, vs) = refs
    else:
        (qn_ref, qr_ref, ckv_ref, kr_ref, wuk_ref, wuv_ref, _, o_ref, kcat, vs) = refs
        cckv_ref = ckr_ref = None

    @pl.when(pl.program_id(1) == 0)
    def _():
        def expand(src_ref, kr_src_ref, n_rows, base):
            for c in range(n_rows // TM):
                rows = slice(c * TM, (c + 1) * TM)
                dst = slice(base + c * TM, base + (c + 1) * TM)
                lat = src_ref[rows, :].astype(BF16)
                kn = jnp.dot(lat, wuk_ref[...], preferred_element_type=F32).astype(BF16)
                vs[dst, :] = jnp.dot(lat, wuv_ref[...], preferred_element_type=F32).astype(BF16)
                krb = kr_src_ref[rows, :].astype(BF16)
                for hh in range(N_HEADS):
                    kcat[dst, hh * HEAD_K:hh * HEAD_K + LANES] = kn[:, hh * LANES:(hh + 1) * LANES]
                    kcat[dst, hh * HEAD_K + LANES:(hh + 1) * HEAD_K] = krb

        expand(ckv_ref, kr_ref, n_new, 0)
        if n_cache:
            expand(cckv_ref, ckr_ref, n_cache, n_new)

    for hh in range(N_HEADS):
        hs = slice(hh * LANES, (hh + 1) * LANES)
        q = jnp.concatenate([qn_ref[:, hs], qr_ref[:, hs]], axis=1)
        s = lax.dot_general(q, kcat[:, hh * HEAD_K:(hh + 1) * HEAD_K], (((1,), (1,)), ((), ())),
                            preferred_element_type=F32) * ATTN_SCALE
        p = jnp.exp(s - jnp.max(s, axis=1, keepdims=True))
        den = jnp.sum(p, axis=1, keepdims=True)
        o = jnp.dot(p.astype(BF16), vs[:, hs], preferred_element_type=F32) / den
        o_ref[:, hs] = o.astype(BF16)


def _attn_call(layer, qn, qr, ckv, kr, wuk, wuv, o_prev, cache=None):
    hd = N_HEADS * LANES
    if cache is None:
        n_b, n_q, n_new, n_cache, tile0 = BATCH, SEQ // TM, SEQ, 0, 0
    else:
        n_b, n_q, n_new, n_cache, tile0 = DEC_BATCH, DEC_SEQ // TM, DEC_SEQ, PAST_LEN, CTX_TILES
    seq_blk0 = tile0 * TM // n_new
    m = layer // N_MIXERS

    def q_map(b, j):
        return (tile0 + b * n_q + j, 0)

    in_specs = [
        pl.BlockSpec((TM, hd), q_map),
        pl.BlockSpec((TM, hd), q_map),
        pl.BlockSpec((n_new, KV_RANK), lambda b, j: (seq_blk0 + b, 0)),
        pl.BlockSpec((n_new, LANES), lambda b, j: (seq_blk0 + b, 0)),
    ]
    args = [qn, qr, ckv, kr]
    if cache is not None:
        in_specs += [
            pl.BlockSpec((None, None, n_cache, KV_RANK), lambda b, j: (b, m, 0, 0)),
            pl.BlockSpec((None, None, n_cache, LANES), lambda b, j: (b, m, 0, 0)),
        ]
        args += list(cache)
    in_specs += [
        pl.BlockSpec((KV_RANK, hd), lambda b, j: (0, 0)),
        pl.BlockSpec((KV_RANK, hd), lambda b, j: (0, 0)),
        pl.BlockSpec(memory_space=pl.ANY),
    ]
    args += [wuk, wuv, o_prev]
    n_keys = n_new + n_cache
    return pl.pallas_call(
        functools.partial(_attn_kernel, n_new=n_new, n_cache=n_cache),
        out_shape=jax.ShapeDtypeStruct((N_TOK, hd), BF16),
        grid=(n_b, n_q),
        in_specs=in_specs,
        out_specs=pl.BlockSpec((TM, hd), q_map),
        scratch_shapes=[pltpu.VMEM((n_keys, N_HEADS * HEAD_K), BF16), pltpu.VMEM((n_keys, hd), BF16)],
        input_output_aliases={len(args) - 1: 0},
        compiler_params=_params(56, 2),
        name=f"mla_attn{layer}_{'lat' if cache is not None else 'ctx'}",
    )(*args)


def _attn_out_kernel(o_ref, wo_ref, x_ref, mod_ref, gffn_ref, wr_ref, br_ref, xnew_ref, h2_ref, route_ref,
                     cnt_ref, seen_ref):
    mix = jnp.dot(o_ref[...], wo_ref[...], preferred_element_type=F32)
    _mixer_tail(x_ref[...], mix, mod_ref, gffn_ref, wr_ref, br_ref, xnew_ref, h2_ref, route_ref, cnt_ref,
                seen_ref)


def _attn_out_call(layer, o, wo, x, mod, g_ffn, wr, br):
    hd = N_HEADS * V_DIM
    return pl.pallas_call(
        _attn_out_kernel,
        out_shape=_tail_out_shapes(),
        grid=(N_TILES,),
        in_specs=[
            pl.BlockSpec((TM, hd), lambda i: (i, 0)),
            pl.BlockSpec((hd, D_MODEL), lambda i: (0, 0)),
            pl.BlockSpec((TM, D_MODEL), lambda i: (i, 0)),
            _mod_spec(layer),
            _row_spec(D_MODEL),
            pl.BlockSpec((D_MODEL, ROUTE_LANES), lambda i: (0, 0)),
            _row_spec(ROUTE_LANES),
        ],
        out_specs=_tail_out_specs(),
        scratch_shapes=_tail_scratch(),
        compiler_params=_params(48),
        name=f"mla_out{layer}",
    )(o, wo, x, mod, g_ffn, wr, br)


def _plan_kernel(cnt_ref, route_ref, pos_ref, wt_ref, we_ref, wlo_ref, whi_ref, nw_ref, start_ref):
    @pl.when(pl.program_id(0) == 0)
    def _():
        lane = lax.broadcasted_iota(jnp.int32, (1, ROUTE_LANES), 1)
        shift = TM.bit_length() - 1

        def per_expert(e, carry):
            n_work, lo, last_e, starts = carry
            cnt = cnt_ref[0, e]
            hi = lo + cnt
            first_tile = lax.shift_right_logical(lo, shift)
            end_tile = jnp.where(cnt > 0, lax.shift_right_logical(hi + (TM - 1), shift), first_tile)

            def per_tile(t, w):
                wt_ref[w] = t
                we_ref[w] = e
                wlo_ref[w] = lo
                whi_ref[w] = hi
                return w + 1

            n_work = lax.fori_loop(first_tile, end_tile, per_tile, n_work)
            starts = jnp.where(lane == e, lo.astype(F32), starts)
            return n_work, hi, jnp.where(cnt > 0, e, last_e), starts

        n_work, _, last_e, starts = lax.fori_loop(
            0, N_EXPERTS, per_expert,
            (jnp.int32(0), jnp.int32(0), jnp.int32(0), jnp.zeros((1, ROUTE_LANES), F32)))
        nw_ref[0] = n_work
        start_ref[...] = starts

        def fill(w, carry):
            wt_ref[w] = SORT_TILES - 1
            we_ref[w] = last_e
            wlo_ref[w] = 0
            whi_ref[w] = 0
            return carry

        lax.fori_loop(n_work, MAX_WORK, fill, 0)

    r = route_ref[...]
    lane = lax.broadcasted_iota(jnp.int32, r.shape, 1).astype(F32)
    starts = start_ref[...]
    pos1 = jnp.sum(jnp.where(lane == r[:, 0:1], starts, 0.0), axis=1, keepdims=True) + r[:, 4:5]
    pos2 = jnp.sum(jnp.where(lane == r[:, 1:2], starts, 0.0), axis=1, keepdims=True) + r[:, 5:6]
    pos = jnp.where(lane == 0.0, pos1, jnp.where(lane == 1.0, pos2, 0.0))
    pos_ref[...] = pos.T[0:SUBLANES, :].astype(jnp.int32)


def _plan_call(layer, counts, route):
    smem = pl.BlockSpec(memory_space=pltpu.SMEM)
    work = jax.ShapeDtypeStruct((MAX_WORK,), jnp.int32)
    return pl.pallas_call(
        _plan_kernel,
        out_shape=(jax.ShapeDtypeStruct((SUBLANES, N_TOK), jnp.int32), work, work, work, work,
                   jax.ShapeDtypeStruct((1,), jnp.int32)),
        grid=(N_TILES,),
        in_specs=[smem, pl.BlockSpec((TM, ROUTE_LANES), lambda i: (i, 0))],
        out_specs=(pl.BlockSpec((SUBLANES, TM), lambda i: (0, i)), smem, smem, smem, smem, smem),
        scratch_shapes=[pltpu.VMEM((1, ROUTE_LANES), F32)],
        compiler_params=_params(16),
        name=f"moe_plan{layer}",
    )(counts, route)


DISPATCH_BUFS = 3


def _dispatch_kernel(pos_ref, h2_hbm, xs_hbm, buf, sem_in, sem_out):
    i = pl.program_id(0)

    def load(tile):
        slot = tile % DISPATCH_BUFS
        return pltpu.make_async_copy(h2_hbm.at[pl.ds(tile * TM, TM)], buf.at[slot], sem_in.at[slot])

    def drain_scatter(tile):
        slot = tile % DISPATCH_BUFS
        for _ in range(2):
            pltpu.make_async_copy(buf.at[slot], xs_hbm.at[pl.ds(0, TM)], sem_out.at[slot]).wait()

    @pl.when(i == 0)
    def _():
        load(i).start()

    @pl.when(i >= DISPATCH_BUFS - 1)
    def _():
        drain_scatter(i - (DISPATCH_BUFS - 1))

    @pl.when(i + 1 < N_TILES)
    def _():
        load(i + 1).start()

    load(i).wait()
    slot = i % DISPATCH_BUFS

    def issue(r, carry):
        t = i * TM + r
        src = buf.at[slot, pl.ds(r, 1)]
        pltpu.make_async_copy(src, xs_hbm.at[pl.ds(pos_ref[t], 1)], sem_out.at[slot]).start()
        pltpu.make_async_copy(src, xs_hbm.at[pl.ds(pos_ref[N_TOK + t], 1)], sem_out.at[slot]).start()
        return carry

    lax.fori_loop(0, TM, issue, 0, unroll=8)

    @pl.when(i == N_TILES - 1)
    def _():
        for back in range(DISPATCH_BUFS - 2, -1, -1):
            drain_scatter(i - back)


def _dispatch_call(layer, pos, h2):
    return pl.pallas_call(
        _dispatch_kernel,
        out_shape=jax.ShapeDtypeStruct((N_SLOTS, D_MODEL), F32),
        grid_spec=pltpu.PrefetchScalarGridSpec(
            num_scalar_prefetch=1,
            grid=(N_TILES,),
            in_specs=[pl.BlockSpec(memory_space=pl.ANY)],
            out_specs=pl.BlockSpec(memory_space=pl.ANY),
            scratch_shapes=[
                pltpu.VMEM((DISPATCH_BUFS, TM, D_MODEL), F32),
                pltpu.SemaphoreType.DMA((DISPATCH_BUFS,)),
                pltpu.SemaphoreType.DMA((DISPATCH_BUFS,)),
            ],
        ),
        compiler_params=_params(16),
        name=f"moe_dispatch{layer}",
    )(pos, h2)


def _gmm_kernel(wt_ref, we_ref, wlo_ref, whi_ref, nw_ref, xs_ref, wg_hbm, wu_hbm, wd_hbm, ys_ref,
                wg_f, wu_f, wd_f, wg_s, wu_s, wd_s, sem, nexp_ref, *, layer):
    w = pl.program_id(0)
    n_work = nw_ref[0]
    prev = jnp.maximum(w - 1, 0)
    new_expert = (w == 0) | (we_ref[w] != we_ref[prev])
    new_tile = (w == 0) | (wt_ref[w] != wt_ref[prev])

    def weight_copies(expert, slot):
        return [pltpu.make_async_copy(hbm.at[layer, expert], buf.at[slot], sem.at[slot, k])
                for k, (hbm, buf) in enumerate(((wg_hbm, wg_f), (wu_hbm, wu_f), (wd_hbm, wd_f)))]

    @pl.when(w == 0)
    def _():
        nexp_ref[0] = 0
        for cp in weight_copies(we_ref[0], 0):
            cp.start()

    @pl.when((w < n_work) & new_expert)
    def _():
        slot = nexp_ref[0] % 2
        nexp_ref[0] += 1
        nxt = lax.while_loop(
            lambda k: (k < n_work) & (we_ref[jnp.minimum(k, MAX_WORK - 1)] == we_ref[w]),
            lambda k: k + 1, w + 1)

        @pl.when(nxt < n_work)
        def _():
            for cp in weight_copies(we_ref[jnp.minimum(nxt, MAX_WORK - 1)], 1 - slot):
                cp.start()

        for cp in weight_copies(we_ref[w], slot):
            cp.wait()
        wg_s[...] = wg_f[slot].astype(BF16)
        wu_s[...] = wu_f[slot].astype(BF16)
        wd_s[...] = wd_f[slot].astype(BF16)

    @pl.when(w < n_work)
    def _():
        x = xs_ref[...].astype(BF16)
        gate = jnp.dot(x, wg_s[...], preferred_element_type=F32)
        up = jnp.dot(x, wu_s[...], preferred_element_type=F32)
        row = wt_ref[w] * TM + lax.broadcasted_iota(jnp.int32, (TM, 1), 0)
        mine = (row >= wlo_ref[w]) & (row < whi_ref[w])
        hid = jnp.where(mine, (gate * jax.nn.sigmoid(gate)) * up, 0.0)
        y = jnp.dot(hid.astype(BF16), wd_s[...], preferred_element_type=F32)

        @pl.when(new_tile)
        def _():
            ys_ref[...] = y

        @pl.when(jnp.logical_not(new_tile))
        def _():
            ys_ref[...] += y


def _gmm_call(layer, xs, work, w_gate, w_up, w_down):
    def tile_spec():
        return pl.BlockSpec((TM, D_MODEL), lambda w, wt, we, wlo, whi, nw: (wt[w], 0))

    up_shape, down_shape = (D_MODEL, EXPERT_HIDDEN), (EXPERT_HIDDEN, D_MODEL)
    hbm = pl.BlockSpec(memory_space=pl.ANY)
    grid_spec = pltpu.PrefetchScalarGridSpec(
        num_scalar_prefetch=5,
        grid=(MAX_WORK,),
        in_specs=[tile_spec(), hbm, hbm, hbm],
        out_specs=tile_spec(),
        scratch_shapes=[
            pltpu.VMEM((2,) + up_shape, F32),
            pltpu.VMEM((2,) + up_shape, F32),
            pltpu.VMEM((2,) + down_shape, F32),
            pltpu.VMEM(up_shape, BF16),
            pltpu.VMEM(up_shape, BF16),
            pltpu.VMEM(down_shape, BF16),
            pltpu.SemaphoreType.DMA((2, 3)),
            pltpu.SMEM((1,), jnp.int32),
        ],
    )
    return pl.pallas_call(
        functools.partial(_gmm_kernel, layer=layer),
        out_shape=jax.ShapeDtypeStruct((N_SLOTS, D_MODEL), F32),
        grid_spec=grid_spec,
        compiler_params=_params(48),
        name=f"moe_gmm{layer}",
    )(*work, xs, w_gate, w_up, w_down)


def _combine_kernel(pos_ref, x_ref, route_ref, mod_ref, gfin_ref, ys_hbm, *rest, final):
    if final:
        octx_ref, olat_ref, buf1, buf2, sem1, sem2 = rest
    else:
        o_ref, buf1, buf2, sem1, sem2 = rest
    i = pl.program_id(0)

    def fetch(tile):
        slot = tile % 2

        def issue(r, carry):
            t = tile * TM + r
            pltpu.make_async_copy(ys_hbm.at[pl.ds(pos_ref[t], 1)], buf1.at[slot, pl.ds(r, 1)],
                                  sem1.at[slot]).start()
            pltpu.make_async_copy(ys_hbm.at[pl.ds(pos_ref[N_TOK + t], 1)], buf2.at[slot, pl.ds(r, 1)],
                                  sem2.at[slot]).start()
            return carry

        lax.fori_loop(0, TM, issue, 0, unroll=8)

    @pl.when(i == 0)
    def _():
        fetch(i)

    @pl.when(i + 1 < N_TILES)
    def _():
        fetch(i + 1)

    slot = i % 2
    pltpu.make_async_copy(ys_hbm.at[pl.ds(0, TM)], buf1.at[slot], sem1.at[slot]).wait()
    pltpu.make_async_copy(ys_hbm.at[pl.ds(0, TM)], buf2.at[slot], sem2.at[slot]).wait()
    r = route_ref[...]
    moe = r[:, 2:3] * buf1[slot] + r[:, 3:4] * buf2[slot]
    x = x_ref[...] + mod_ref[5:6, :] * moe
    if not final:
        o_ref[...] = x
        return
    y = x * lax.rsqrt(jnp.mean(x * x, axis=-1, keepdims=True) + EPS) * gfin_ref[...]

    @pl.when(i < CTX_TILES)
    def _():
        octx_ref[...] = y

    @pl.when(i >= CTX_TILES)
    def _():
        olat_ref[...] = y


def _combine_call(layer, pos, x_new, route, ys, mod, g_final, final):
    def tok_spec(width):
        return pl.BlockSpec((TM, width), lambda i, pos: (i, 0))

    if final:
        out_shape = (jax.ShapeDtypeStruct((N_CTX, D_MODEL), F32), jax.ShapeDtypeStruct((N_LAT, D_MODEL), F32))
        out_specs = (pl.BlockSpec((TM, D_MODEL), lambda i, pos: (jnp.minimum(i, CTX_TILES - 1), 0)),
                     pl.BlockSpec((TM, D_MODEL), lambda i, pos: (jnp.maximum(i - CTX_TILES, 0), 0)))
    else:
        out_shape = jax.ShapeDtypeStruct((N_TOK, D_MODEL), F32)
        out_specs = tok_spec(D_MODEL)

    grid_spec = pltpu.PrefetchScalarGridSpec(
        num_scalar_prefetch=1,
        grid=(N_TILES,),
        in_specs=[
            tok_spec(D_MODEL),
            tok_spec(ROUTE_LANES),
            pl.BlockSpec((None, None, N_MOD, D_MODEL), lambda i, pos: (layer, _mod_row(i), 0, 0)),
            pl.BlockSpec((1, D_MODEL), lambda i, pos: (0, 0)),
            pl.BlockSpec(memory_space=pl.ANY),
        ],
        out_specs=out_specs,
        scratch_shapes=[
            pltpu.VMEM((2, TM, D_MODEL), F32),
            pltpu.VMEM((2, TM, D_MODEL), F32),
            pltpu.SemaphoreType.DMA((2,)),
            pltpu.SemaphoreType.DMA((2,)),
        ],
    )
    return pl.pallas_call(
        functools.partial(_combine_kernel, final=final),
        out_shape=out_shape,
        grid_spec=grid_spec,
        compiler_params=_params(40),
        name=f"moe_combine{layer}",
    )(pos, x_new, route, mod, g_final, ys)


def _rope_tables():
    rows = DEC_SEQ // GRID_W
    row_ids = jnp.repeat(jnp.arange(rows), GRID_W).astype(F32)
    col_ids = jnp.tile(jnp.arange(GRID_W), rows).astype(F32)
    freqs = ROPE_THETA ** (-jnp.arange(ROPE_QUARTER, dtype=F32) / ROPE_QUARTER)
    ar, ac = row_ids[:, None] * freqs[None, :], col_ids[:, None] * freqs[None, :]
    cos = jnp.concatenate([jnp.cos(ar), jnp.cos(ar), jnp.cos(ac), jnp.cos(ac)], axis=1)
    sin = jnp.concatenate([-jnp.sin(ar), jnp.sin(ar), -jnp.sin(ac), jnp.sin(ac)], axis=1)
    return jnp.concatenate([cos, sin], axis=1)


def _swap_partners(w):
    q = ROPE_QUARTER
    return jnp.concatenate([w[..., q:2 * q], w[..., 0:q], w[..., 3 * q:4 * q], w[..., 2 * q:3 * q]], axis=-1)


def kernel(x_prompt, x_sample, cache_ckv, cache_krope, c, c_ctx, g_mix, g_ffn, w_ada, b_ada, w_pool, pool_scale,
           w_dq, g_q, w_uq, w_dkv, g_kv, w_uk, w_uv, w_o, w_router_grp, b_router_grp, w_router_exp,
           b_router_exp, w_gate, w_up, w_down, g_final):
    n_mla = DEPTH // N_MIXERS
    x = (x_prompt.reshape(N_CTX, D_MODEL), x_sample.reshape(N_LAT, D_MODEL))

    cond = jnp.concatenate([c_ctx[None, :], c, jnp.zeros((COND_ROWS - 1 - DEC_BATCH, D_MODEL), F32)], axis=0)
    mod = _ada_call(cond, w_ada, b_ada).reshape(DEPTH, COND_ROWS, N_MOD, D_MODEL)

    wr = jnp.concatenate([w_router_exp, w_router_grp,
                          jnp.zeros((DEPTH, D_MODEL, ROUTE_LANES - N_EXPERTS - N_EXPERT_GROUPS), F32)],
                         axis=2).astype(BF16)
    br = jnp.concatenate([b_router_exp, b_router_grp,
                          jnp.zeros((DEPTH, ROUTE_LANES - N_EXPERTS - N_EXPERT_GROUPS), F32)], axis=1)
    band = jnp.asarray(_pool_band(), dtype=BF16)
    wpool = w_pool.astype(BF16)
    wdq = w_dq.astype(BF16)
    wdkv = w_dkv[:, :, :KV_RANK].astype(BF16)
    w_kr = w_dkv[:, :, KV_RANK:]
    wkr = jnp.concatenate([w_kr, _swap_partners(w_kr)], axis=-1).astype(BF16)
    wuqn = w_uq[..., :NOPE_DIM].reshape(n_mla, Q_RANK, N_HEADS * NOPE_DIM).astype(BF16)
    w_qr = w_uq[..., NOPE_DIM:]
    wuqr = jnp.concatenate([w_qr, _swap_partners(w_qr)], axis=-1).reshape(n_mla, Q_RANK, N_HEADS * LANES)
    wuqr = wuqr.astype(BF16)
    wuk = w_uk.reshape(n_mla, KV_RANK, N_HEADS * NOPE_DIM).astype(BF16)
    wuv = w_uv.reshape(n_mla, KV_RANK, N_HEADS * V_DIM).astype(BF16)
    wo = w_o.astype(BF16)
    tabk = _rope_tables()
    tabq = jnp.tile(tabk, (1, N_HEADS))
    cache_kr = jnp.pad(cache_krope, ((0, 0), (0, 0), (0, 0), (0, LANES - ROPE_DIM)))

    def row(a):
        return a.reshape(1, -1)

    ckv_ctx, kr_ctx = [], []
    for layer in range(DEPTH):
        if layer % N_MIXERS == 0:
            p = layer // N_MIXERS
            streams = x if isinstance(x, tuple) else (x,)
            x_new, h2, route, counts = _pool_call(layer, streams, mod, row(g_mix[layer]), row(g_ffn[layer]),
                                                  band, wpool[p], row(pool_scale[p]), wr[layer], row(br[layer]))
        else:
            m = layer // N_MIXERS
            qn, qr, ckv, kr = _qkv_call(layer, x, mod, row(g_mix[layer]), wdq[m], row(g_q[m]), wdkv[m],
                                        row(g_kv[m]), wkr[m], wuqn[m], wuqr[m], tabq, tabk)
            ckv_ctx.append(ckv[:N_CTX].reshape(BATCH, SEQ, KV_RANK))
            kr_ctx.append(kr[:N_CTX, :ROPE_DIM].reshape(BATCH, SEQ, ROPE_DIM))
            o = jnp.zeros((N_TOK, N_HEADS * V_DIM), BF16)
            o = _attn_call(layer, qn, qr, ckv, kr, wuk[m], wuv[m], o)
            o = _attn_call(layer, qn, qr, ckv, kr, wuk[m], wuv[m], o, cache=(cache_ckv, cache_kr))
            x_new, h2, route, counts = _attn_out_call(layer, o, wo[m], x, mod, row(g_ffn[layer]), wr[layer],
                                                      row(br[layer]))
        pos8, *work = _plan_call(layer, counts, route)
        pos = pos8[:2].reshape(N_SLOTS)
        xs = _dispatch_call(layer, pos, h2)
        ys = _gmm_call(layer, xs, work, w_gate, w_up, w_down)
        x = _combine_call(layer, pos, x_new, route, ys, mod, row(g_final), final=(layer == DEPTH - 1))

    y_prompt = x[0].reshape(BATCH, SEQ, D_MODEL)
    y_sample = x[1].reshape(DEC_BATCH, DEC_SEQ, D_MODEL)
    return y_prompt, y_sample, jnp.stack(ckv_ctx, axis=1), jnp.stack(kr_ctx, axis=1)
```

```python
import functools

import numpy as np
import jax
import jax.numpy as jnp
from jax import lax
from jax.experimental import pallas as pl
from jax.experimental.pallas import tpu as pltpu

D_MODEL = 2048
BATCH = 32
SEQ = 256
DEPTH = 4
DEC_BATCH = 2
DEC_SEQ = 1024
PAST_LEN = 512
GRID_W = 64
N_MIXERS = 2
POOL_WINDOWS = (2, 4, 8, 16)
N_POOL_GROUPS = 4
POOL_CH = D_MODEL // N_POOL_GROUPS
N_HEADS = 16
Q_RANK = 512
KV_RANK = 512
NOPE_DIM = 128
ROPE_DIM = 64
V_DIM = 128
ROPE_QUARTER = ROPE_DIM // 4
ROPE_THETA = 10000.0
ATTN_SCALE = (NOPE_DIM + ROPE_DIM) ** -0.5
N_EXPERT_GROUPS = 4
EXPERTS_PER_GROUP = 8
N_EXPERTS = N_EXPERT_GROUPS * EXPERTS_PER_GROUP
EXPERT_HIDDEN = 512
N_MOD = 6
EPS = 1e-6

LANES = 128
SUBLANES = 8
TM = 256
N_CTX = BATCH * SEQ
N_LAT = DEC_BATCH * DEC_SEQ
N_TOK = N_CTX + N_LAT
CTX_TILES = N_CTX // TM
LAT_TILES = N_LAT // TM
N_TILES = CTX_TILES + LAT_TILES
LAT_TILES_PER_SEQ = DEC_SEQ // TM
COND_ROWS = SUBLANES
POOL_HALO = max(POOL_WINDOWS) // 2
POOL_K = TM + LANES
ROUTE_LANES = LANES
N_SLOTS = 2 * N_TOK
SORT_TILES = N_SLOTS // TM
MAX_WORK = SORT_TILES + N_EXPERTS - 1
MIB = 1024 * 1024

F32 = jnp.float32
BF16 = jnp.bfloat16


def _mod_row(tile):
    return jnp.where(tile < CTX_TILES, 0, 1 + jnp.maximum(tile - CTX_TILES, 0) // LAT_TILES_PER_SEQ)


def _params(vmem_mib, n_axes=1):
    return pltpu.CompilerParams(
        dimension_semantics=("arbitrary",) * n_axes,
        vmem_limit_bytes=vmem_mib * MIB,
    )


def _norm_mod(x, g, shift, scale):
    y = x * lax.rsqrt(jnp.mean(x * x, axis=-1, keepdims=True) + EPS)
    return (y * g) * (1.0 + scale) + shift


def _route(h_bf, wr_ref, br_ref, seen_ref):
    logits = jnp.dot(h_bf, wr_ref[...], preferred_element_type=F32) + br_ref[...]
    lane_i = lax.broadcasted_iota(jnp.int32, logits.shape, 1)
    lane = lane_i.astype(F32)
    neg = -jnp.inf
    far = float(ROUTE_LANES)
    gmask = (lane_i >= N_EXPERTS) & (lane_i < N_EXPERTS + N_EXPERT_GROUPS)
    gl = jnp.where(gmask, logits, neg)
    gmax = jnp.max(gl, axis=1, keepdims=True)
    grp = jnp.min(jnp.where(gl == gmax, lane, far), axis=1, keepdims=True) - N_EXPERTS
    p_grp = 1.0 / jnp.sum(jnp.exp(gl - gmax), axis=1, keepdims=True)
    lane_grp = lax.shift_right_logical(lane_i, EXPERTS_PER_GROUP.bit_length() - 1).astype(F32)
    emask = lane_grp == grp
    el = jnp.where(emask, logits, neg)
    m1 = jnp.max(el, axis=1, keepdims=True)
    i1 = jnp.min(jnp.where(el == m1, lane, far), axis=1, keepdims=True)
    el2 = jnp.where(lane == i1, neg, el)
    m2 = jnp.max(el2, axis=1, keepdims=True)
    i2 = jnp.min(jnp.where((el2 == m2) & emask & (lane != i1), lane, far), axis=1, keepdims=True)
    e2 = jnp.exp(m2 - m1)
    p1 = p_grp / (1.0 + e2)
    p2 = p1 * e2
    n_rows = logits.shape[0]
    earlier = (lax.broadcasted_iota(jnp.int32, (n_rows, n_rows), 0)
               > lax.broadcasted_iota(jnp.int32, (n_rows, n_rows), 1)).astype(BF16)
    hot1 = (lane == i1).astype(F32)
    hot2 = (lane == i2).astype(F32)
    before1 = jnp.dot(earlier, hot1.astype(BF16), preferred_element_type=F32)
    before2 = jnp.dot(earlier, hot2.astype(BF16), preferred_element_type=F32)
    cnt1 = jnp.sum(hot1, axis=0, keepdims=True)
    cnt2 = jnp.sum(hot2, axis=0, keepdims=True)
    seen = seen_ref[...]
    rank1 = jnp.sum(hot1 * (seen + before1), axis=1, keepdims=True)
    rank2 = jnp.sum(hot2 * (seen + cnt1 + before2), axis=1, keepdims=True)
    seen_ref[...] = seen + cnt1 + cnt2

    out = jnp.where(lane == 0, i1, 0.0)
    for k, col in enumerate((i2, p1, p2, rank1, rank2), start=1):
        out = jnp.where(lane == k, col, out)
    return out


def _mixer_tail(x, mix_out, mod_ref, gffn_ref, wr_ref, br_ref, xnew_ref, h2_ref, route_ref, cnt_ref, seen_ref):
    @pl.when(pl.program_id(0) == 0)
    def _():
        seen_ref[...] = jnp.zeros_like(seen_ref)

    x_new = x + mod_ref[2:3, :] * mix_out
    xnew_ref[...] = x_new
    h2 = _norm_mod(x_new, gffn_ref[...], mod_ref[3:4, :], mod_ref[4:5, :])
    h2_ref[...] = h2
    route_ref[...] = _route(h2.astype(BF16), wr_ref, br_ref, seen_ref)
    cnt_ref[...] = seen_ref[...].astype(jnp.int32)


ADA_TN = 1024


def _ada_kernel(cond_ref, w_ref, b_ref, o_ref):
    c = cond_ref[...]
    s = c * jax.nn.sigmoid(c)
    o_ref[...] = jnp.dot(s.astype(BF16), w_ref[...].astype(BF16), preferred_element_type=F32) + b_ref[...]


def _ada_call(cond, w_ada, b_ada):
    nj = (N_MOD * D_MODEL) // ADA_TN
    return pl.pallas_call(
        _ada_kernel,
        out_shape=jax.ShapeDtypeStruct((DEPTH, COND_ROWS, N_MOD * D_MODEL), F32),
        grid=(DEPTH, nj),
        in_specs=[
            pl.BlockSpec((COND_ROWS, D_MODEL), lambda l, j: (0, 0)),
            pl.BlockSpec((None, D_MODEL, ADA_TN), lambda l, j: (l, 0, j)),
            pl.BlockSpec((None, 1, ADA_TN), lambda l, j: (l, 0, j)),
        ],
        out_specs=pl.BlockSpec((None, COND_ROWS, ADA_TN), lambda l, j: (l, 0, j)),
        compiler_params=_params(40, 2),
        name="ada_mod",
    )(cond, w_ada, b_ada.reshape(DEPTH, 1, N_MOD * D_MODEL))


def _pool_band():
    col = np.arange(POOL_K)
    rel = np.where(col < TM, col, np.where(col < TM + POOL_HALO, col - TM - POOL_HALO, col - POOL_HALO))
    ok = col < TM + 2 * POOL_HALO
    row = np.arange(TM)[:, None]
    mats = []
    for w in POOL_WINDOWS:
        mats.append((rel[None, :] >= row - w // 2) & (rel[None, :] < row + (w - w // 2)) & ok[None, :])
    return np.stack(mats).astype(np.float32)


def _pool_kernel(*refs, split_streams):
    if split_streams:
        xc_ref, xl_ref, *refs = refs
    else:
        x_ref, *refs = refs
    (xp_ref, xn_ref, mod_ref, gmix_ref, gffn_ref, band_ref, wpool_ref, pscale_ref, wr_ref, br_ref,
     xnew_ref, h2_ref, route_ref, cnt_ref, seen_ref) = refs
    i = pl.program_id(0)
    lat = i >= CTX_TILES
    tile_pos = jnp.where(lat, jnp.maximum(i - CTX_TILES, 0) % LAT_TILES_PER_SEQ, 0)
    seq_len = jnp.where(lat, DEC_SEQ, SEQ)
    has_prev = tile_pos > 0
    has_next = (tile_pos + 1) * TM < seq_len

    g = gmix_ref[...]
    shift, scale = mod_ref[0:1, :], mod_ref[1:2, :]
    x = jnp.where(lat, xl_ref[...], xc_ref[...]) if split_streams else x_ref[...]
    h = _norm_mod(x, g, shift, scale)
    hp = jnp.where(has_prev, _norm_mod(xp_ref[...], g, shift, scale), 0.0)
    hn = jnp.where(has_next, _norm_mod(xn_ref[...], g, shift, scale), 0.0)
    hall = jnp.concatenate(
        [h, hp, hn, jnp.zeros((POOL_K - TM - 2 * POOL_HALO, D_MODEL), F32)], axis=0)
    hi = hall.astype(BF16)
    lo = (hall - hi.astype(F32)).astype(BF16)

    pos = tile_pos * TM + lax.broadcasted_iota(jnp.int32, (TM, 1), 0)
    outs = []
    for gi, w in enumerate(POOL_WINDOWS):
        sl = slice(gi * POOL_CH, (gi + 1) * POOL_CH)
        band = band_ref[gi]
        wsum = (jnp.dot(band, hi[:, sl], preferred_element_type=F32)
                + jnp.dot(band, lo[:, sl], preferred_element_type=F32))
        cnt = jnp.minimum(pos + (w - w // 2), seq_len) - jnp.maximum(pos - w // 2, 0)
        diff = wsum / cnt.astype(F32) - h[:, sl]
        outs.append(jnp.dot(diff.astype(BF16), wpool_ref[gi], preferred_element_type=F32))
    mix = jnp.concatenate(outs, axis=1) * pscale_ref[...]
    _mixer_tail(x, mix, mod_ref, gffn_ref, wr_ref, br_ref, xnew_ref, h2_ref, route_ref, cnt_ref, seen_ref)


def _tail_out_shapes():
    return (jax.ShapeDtypeStruct((N_TOK, D_MODEL), F32),
            jax.ShapeDtypeStruct((N_TOK, D_MODEL), F32),
            jax.ShapeDtypeStruct((N_TOK, ROUTE_LANES), F32),
            jax.ShapeDtypeStruct((1, ROUTE_LANES), jnp.int32))


def _tail_out_specs():
    return (pl.BlockSpec((TM, D_MODEL), lambda i: (i, 0)),
            pl.BlockSpec((TM, D_MODEL), lambda i: (i, 0)),
            pl.BlockSpec((TM, ROUTE_LANES), lambda i: (i, 0)),
            pl.BlockSpec((1, ROUTE_LANES), lambda i: (0, 0)))


def _tail_scratch():
    return [pltpu.VMEM((1, ROUTE_LANES), F32)]


def _row_spec(width):
    return pl.BlockSpec((1, width), lambda i: (0, 0))


def _mod_spec(layer):
    return pl.BlockSpec((None, None, N_MOD, D_MODEL), lambda i: (layer, _mod_row(i), 0, 0))


def _pool_call(layer, xs, mod, g_mix, g_ffn, band, wpool, pscale, wr, br):
    halo_blocks = TM // POOL_HALO
    split_streams = len(xs) == 2
    tile0 = CTX_TILES if split_streams else 0
    last_halo = xs[-1].shape[0] // POOL_HALO - 1

    def prev_map(i):
        return (jnp.clip((i - tile0) * halo_blocks - 1, 0, last_halo), 0)

    def next_map(i):
        return (jnp.clip((i - tile0 + 1) * halo_blocks, 0, last_halo), 0)

    if split_streams:
        main_specs = [pl.BlockSpec((TM, D_MODEL), lambda i: (jnp.minimum(i, CTX_TILES - 1), 0)),
                      pl.BlockSpec((TM, D_MODEL), lambda i: (jnp.maximum(i - CTX_TILES, 0), 0))]
    else:
        main_specs = [pl.BlockSpec((TM, D_MODEL), lambda i: (i, 0))]
    return pl.pallas_call(
        functools.partial(_pool_kernel, split_streams=split_streams),
        out_shape=_tail_out_shapes(),
        grid=(N_TILES,),
        in_specs=main_specs + [
            pl.BlockSpec((POOL_HALO, D_MODEL), prev_map),
            pl.BlockSpec((POOL_HALO, D_MODEL), next_map),
            _mod_spec(layer),
            _row_spec(D_MODEL),
            _row_spec(D_MODEL),
            pl.BlockSpec((N_POOL_GROUPS, TM, POOL_K), lambda i: (0, 0, 0)),
            pl.BlockSpec((N_POOL_GROUPS, POOL_CH, POOL_CH), lambda i: (0, 0, 0)),
            _row_spec(D_MODEL),
            pl.BlockSpec((D_MODEL, ROUTE_LANES), lambda i: (0, 0)),
            _row_spec(ROUTE_LANES),
        ],
        out_specs=_tail_out_specs(),
        scratch_shapes=_tail_scratch(),
        compiler_params=_params(48),
        name=f"pool_layer{layer}",
    )(*xs, xs[-1], xs[-1], mod, g_mix, g_ffn, band, wpool, pscale, wr, br)


def _qkv_kernel(x_ref, mod_ref, gmix_ref, wdq_ref, gq_ref, wdkv_ref, gkv_ref, wkr_ref, wuqn_ref, wuqr_ref,
                tabq_ref, tabk_ref, _ckv_leaf_in, _kr_leaf_in, qn_ref, qr_ref, ckv_ref, kr_ref, ckv_leaf_ref,
                kr_leaf_ref):
    i = pl.program_id(0)
    h = _norm_mod(x_ref[...], gmix_ref[...], mod_ref[0:1, :], mod_ref[1:2, :]).astype(BF16)

    def rms(v, g):
        return v * lax.rsqrt(jnp.mean(v * v, axis=-1, keepdims=True) + EPS) * g

    cq = rms(jnp.dot(h, wdq_ref[...], preferred_element_type=F32), gq_ref[...]).astype(BF16)
    ckv = rms(jnp.dot(h, wdkv_ref[...], preferred_element_type=F32), gkv_ref[...])
    ckv_ref[...] = ckv
    qn_ref[...] = jnp.dot(cq, wuqn_ref[...], preferred_element_type=F32).astype(BF16)
    qr = jnp.dot(cq, wuqr_ref[...], preferred_element_type=F32)
    kr = jnp.dot(h, wkr_ref[...], preferred_element_type=F32)
    lane = lax.broadcasted_iota(jnp.int32, kr.shape, 1)

    @pl.when(i < CTX_TILES)
    def _():
        qr_ref[...] = qr.astype(BF16)
        kr_ref[...] = jnp.where(lane < ROPE_DIM, kr, 0.0)
        ckv_leaf_ref[...] = ckv
        kr_leaf_ref[...] = kr[:, :ROPE_DIM]

    @pl.when(i >= CTX_TILES)
    def _():
        tq = qr * tabq_ref[...]
        qr_ref[...] = (tq + pltpu.roll(tq, N_HEADS * LANES - ROPE_DIM, 1)).astype(BF16)
        tk = kr * tabk_ref[...]
        kr_ref[...] = jnp.where(lane < ROPE_DIM, tk + pltpu.roll(tk, ROPE_DIM, 1), 0.0)


def _qkv_call(layer, x, mod, g_mix, wdq, gq, wdkv, gkv, wkr, wuqn, wuqr, tabq, tabk, ckv_leaf, kr_leaf):
    m = layer // N_MIXERS

    def leaf_spec(width):
        return pl.BlockSpec((None, None, SEQ, width), lambda i: (jnp.minimum(i, CTX_TILES - 1), m, 0, 0))

    hd = N_HEADS * LANES

    def lat_tile(i):
        return jnp.maximum(i - CTX_TILES, 0) % LAT_TILES_PER_SEQ

    def full(a):
        return pl.BlockSpec(a.shape, lambda i: (0,) * a.ndim)

    return pl.pallas_call(
        _qkv_kernel,
        out_shape=(jax.ShapeDtypeStruct((N_TOK, hd), BF16),
                   jax.ShapeDtypeStruct((N_TOK, hd), BF16),
                   jax.ShapeDtypeStruct((N_TOK, KV_RANK), F32),
                   jax.ShapeDtypeStruct((N_TOK, LANES), F32),
                   jax.ShapeDtypeStruct(ckv_leaf.shape, F32),
                   jax.ShapeDtypeStruct(kr_leaf.shape, F32)),
        grid=(N_TILES,),
        in_specs=[
            pl.BlockSpec((TM, D_MODEL), lambda i: (i, 0)),
            _mod_spec(layer),
            _row_spec(D_MODEL),
            full(wdq), _row_spec(Q_RANK), full(wdkv), _row_spec(KV_RANK), full(wkr), full(wuqn), full(wuqr),
            pl.BlockSpec((TM, hd), lambda i: (lat_tile(i), 0)),
            pl.BlockSpec((TM, LANES), lambda i: (lat_tile(i), 0)),
            pl.BlockSpec(memory_space=pl.ANY),
            pl.BlockSpec(memory_space=pl.ANY),
        ],
        out_specs=(pl.BlockSpec((TM, hd), lambda i: (i, 0)),
                   pl.BlockSpec((TM, hd), lambda i: (i, 0)),
                   pl.BlockSpec((TM, KV_RANK), lambda i: (i, 0)),
                   pl.BlockSpec((TM, LANES), lambda i: (i, 0)),
                   leaf_spec(KV_RANK),
                   leaf_spec(ROPE_DIM)),
        input_output_aliases={12: 4, 13: 5},
        compiler_params=_params(48),
        name=f"mla_qkv{layer}",
    )(x, mod, g_mix, wdq, gq, wdkv, gkv, wkr, wuqn, wuqr, tabq, tabk, ckv_leaf, kr_leaf)


HEAD_K = 2 * LANES


def _attn_kernel(*refs, n_new, n_cache):
    if n_cache:
        (qn_ref, qr_ref, ckv_ref, kr_ref, cckv_ref, ckr_ref, wuk_ref, wuv_ref, o_ref, kcat, vs) = refs
    else:
        (qn_ref, qr_ref, ckv_ref, kr_ref, wuk_ref, wuv_ref, o_ref, kcat, vs) = refs
        cckv_ref = ckr_ref = None

    @pl.when(pl.program_id(1) == 0)
    def _():
        def expand(src_ref, kr_src_ref, n_rows, base):
            for c in range(n_rows // TM):
                rows = slice(c * TM, (c + 1) * TM)
                dst = slice(base + c * TM, base + (c + 1) * TM)
                lat = src_ref[rows, :].astype(BF16)
                kn = jnp.dot(lat, wuk_ref[...], preferred_element_type=F32).astype(BF16)
                vs[dst, :] = jnp.dot(lat, wuv_ref[...], preferred_element_type=F32).astype(BF16)
                krb = kr_src_ref[rows, :].astype(BF16)
                for hh in range(N_HEADS):
                    kcat[dst, hh * HEAD_K:hh * HEAD_K + LANES] = kn[:, hh * LANES:(hh + 1) * LANES]
                    kcat[dst, hh * HEAD_K + LANES:(hh + 1) * HEAD_K] = krb

        expand(ckv_ref, kr_ref, n_new, 0)
        if n_cache:
            expand(cckv_ref, ckr_ref, n_cache, n_new)

    for hh in range(N_HEADS):
        hs = slice(hh * LANES, (hh + 1) * LANES)
        q = jnp.concatenate([qn_ref[:, hs], qr_ref[:, hs]], axis=1)
        s = lax.dot_general(q, kcat[:, hh * HEAD_K:(hh + 1) * HEAD_K], (((1,), (1,)), ((), ())),
                            preferred_element_type=F32) * ATTN_SCALE
        p = jnp.exp(s - jnp.max(s, axis=1, keepdims=True))
        den = jnp.sum(p, axis=1, keepdims=True)
        o = jnp.dot(p.astype(BF16), vs[:, hs], preferred_element_type=F32) / den
        o_ref[:, hs] = o.astype(BF16)


def _attn_call(layer, qn, qr, ckv, kr, wuk, wuv, cache=None):
    hd = N_HEADS * LANES
    if cache is None:
        n_b, n_q, n_new, n_cache, tile0 = BATCH, SEQ // TM, SEQ, 0, 0
    else:
        n_b, n_q, n_new, n_cache, tile0 = DEC_BATCH, DEC_SEQ // TM, DEC_SEQ, PAST_LEN, CTX_TILES
    seq_blk0 = tile0 * TM // n_new
    m = layer // N_MIXERS

    def q_map(b, j):
        return (tile0 + b * n_q + j, 0)

    in_specs = [
        pl.BlockSpec((TM, hd), q_map),
        pl.BlockSpec((TM, hd), q_map),
        pl.BlockSpec((n_new, KV_RANK), lambda b, j: (seq_blk0 + b, 0)),
        pl.BlockSpec((n_new, LANES), lambda b, j: (seq_blk0 + b, 0)),
    ]
    args = [qn, qr, ckv, kr]
    if cache is not None:
        in_specs += [
            pl.BlockSpec((None, None, n_cache, KV_RANK), lambda b, j: (b, m, 0, 0)),
            pl.BlockSpec((None, None, n_cache, LANES), lambda b, j: (b, m, 0, 0)),
        ]
        args += list(cache)
    in_specs += [
        pl.BlockSpec((KV_RANK, hd), lambda b, j: (0, 0)),
        pl.BlockSpec((KV_RANK, hd), lambda b, j: (0, 0)),
    ]
    args += [wuk, wuv]
    n_keys = n_new + n_cache
    return pl.pallas_call(
        functools.partial(_attn_kernel, n_new=n_new, n_cache=n_cache),
        out_shape=jax.ShapeDtypeStruct((n_b * n_q * TM, hd), BF16),
        grid=(n_b, n_q),
        in_specs=in_specs,
        out_specs=pl.BlockSpec((TM, hd), lambda b, j: (b * n_q + j, 0)),
        scratch_shapes=[pltpu.VMEM((n_keys, N_HEADS * HEAD_K), BF16), pltpu.VMEM((n_keys, hd), BF16)],
        compiler_params=_params(56, 2),
        name=f"mla_attn{layer}_{'lat' if cache is not None else 'ctx'}",
    )(*args)


def _attn_out_kernel(octx_ref, olat_ref, wo_ref, x_ref, mod_ref, gffn_ref, wr_ref, br_ref, xnew_ref, h2_ref,
                     route_ref, cnt_ref, seen_ref):
    o = jnp.where(pl.program_id(0) >= CTX_TILES, olat_ref[...], octx_ref[...])
    mix = jnp.dot(o, wo_ref[...], preferred_element_type=F32)
    _mixer_tail(x_ref[...], mix, mod_ref, gffn_ref, wr_ref, br_ref, xnew_ref, h2_ref, route_ref, cnt_ref,
                seen_ref)


def _attn_out_call(layer, o_ctx, o_lat, wo, x, mod, g_ffn, wr, br):
    hd = N_HEADS * V_DIM
    return pl.pallas_call(
        _attn_out_kernel,
        out_shape=_tail_out_shapes(),
        grid=(N_TILES,),
        in_specs=[
            pl.BlockSpec((TM, hd), lambda i: (jnp.minimum(i, CTX_TILES - 1), 0)),
            pl.BlockSpec((TM, hd), lambda i: (jnp.maximum(i - CTX_TILES, 0), 0)),
            pl.BlockSpec((hd, D_MODEL), lambda i: (0, 0)),
            pl.BlockSpec((TM, D_MODEL), lambda i: (i, 0)),
            _mod_spec(layer),
            _row_spec(D_MODEL),
            pl.BlockSpec((D_MODEL, ROUTE_LANES), lambda i: (0, 0)),
            _row_spec(ROUTE_LANES),
        ],
        out_specs=_tail_out_specs(),
        scratch_shapes=_tail_scratch(),
        compiler_params=_params(48),
        name=f"mla_out{layer}",
    )(o_ctx, o_lat, wo, x, mod, g_ffn, wr, br)


def _plan_kernel(cnt_ref, route_ref, pos_ref, wt_ref, we_ref, wlo_ref, whi_ref, nw_ref, start_ref):
    @pl.when(pl.program_id(0) == 0)
    def _():
        lane = lax.broadcasted_iota(jnp.int32, (1, ROUTE_LANES), 1)
        shift = TM.bit_length() - 1

        def per_expert(e, carry):
            n_work, lo, last_e, starts = carry
            cnt = cnt_ref[0, e]
            hi = lo + cnt
            first_tile = lax.shift_right_logical(lo, shift)
            end_tile = jnp.where(cnt > 0, lax.shift_right_logical(hi + (TM - 1), shift), first_tile)

            def per_tile(t, w):
                wt_ref[w] = t
                we_ref[w] = e
                wlo_ref[w] = lo
                whi_ref[w] = hi
                return w + 1

            n_work = lax.fori_loop(first_tile, end_tile, per_tile, n_work)
            starts = jnp.where(lane == e, lo.astype(F32), starts)
            return n_work, hi, jnp.where(cnt > 0, e, last_e), starts

        n_work, _, last_e, starts = lax.fori_loop(
            0, N_EXPERTS, per_expert,
            (jnp.int32(0), jnp.int32(0), jnp.int32(0), jnp.zeros((1, ROUTE_LANES), F32)))
        nw_ref[0] = n_work
        start_ref[...] = starts

        def fill(w, carry):
            wt_ref[w] = SORT_TILES - 1
            we_ref[w] = last_e
            wlo_ref[w] = 0
            whi_ref[w] = 0
            return carry

        lax.fori_loop(n_work, MAX_WORK, fill, 0)

    r = route_ref[...]
    lane = lax.broadcasted_iota(jnp.int32, r.shape, 1).astype(F32)
    starts = start_ref[...]
    pos1 = jnp.sum(jnp.where(lane == r[:, 0:1], starts, 0.0), axis=1, keepdims=True) + r[:, 4:5]
    pos2 = jnp.sum(jnp.where(lane == r[:, 1:2], starts, 0.0), axis=1, keepdims=True) + r[:, 5:6]
    pos = jnp.where(lane == 0.0, pos1, jnp.where(lane == 1.0, pos2, 0.0))
    pos_ref[...] = pos.T[0:SUBLANES, :].astype(jnp.int32)


PLAN_ROWS = 4 * TM


def _plan_call(layer, counts, route):
    smem = pl.BlockSpec(memory_space=pltpu.SMEM)
    work = jax.ShapeDtypeStruct((MAX_WORK,), jnp.int32)
    return pl.pallas_call(
        _plan_kernel,
        out_shape=(jax.ShapeDtypeStruct((SUBLANES, N_TOK), jnp.int32), work, work, work, work,
                   jax.ShapeDtypeStruct((1,), jnp.int32)),
        grid=(N_TOK // PLAN_ROWS,),
        in_specs=[smem, pl.BlockSpec((PLAN_ROWS, ROUTE_LANES), lambda i: (i, 0))],
        out_specs=(pl.BlockSpec((SUBLANES, PLAN_ROWS), lambda i: (0, i)), smem, smem, smem, smem, smem),
        scratch_shapes=[pltpu.VMEM((1, ROUTE_LANES), F32)],
        compiler_params=_params(16),
        name=f"moe_plan{layer}",
    )(counts, route)


DISPATCH_BUFS = 3


def _dispatch_kernel(pos_ref, h2_hbm, xs_hbm, buf, sem_in, sem_out):
    i = pl.program_id(0)

    def load(tile):
        slot = tile % DISPATCH_BUFS
        return pltpu.make_async_copy(h2_hbm.at[pl.ds(tile * TM, TM)], buf.at[slot], sem_in.at[slot])

    def drain_scatter(tile):
        slot = tile % DISPATCH_BUFS
        for _ in range(2):
            pltpu.make_async_copy(buf.at[slot], xs_hbm.at[pl.ds(0, TM)], sem_out.at[slot]).wait()

    @pl.when(i == 0)
    def _():
        load(i).start()

    @pl.when(i >= DISPATCH_BUFS - 1)
    def _():
        drain_scatter(i - (DISPATCH_BUFS - 1))

    @pl.when(i + 1 < N_TILES)
    def _():
        load(i + 1).start()

    load(i).wait()
    slot = i % DISPATCH_BUFS

    def issue(r, carry):
        t = i * TM + r
        src = buf.at[slot, pl.ds(r, 1)]
        for k in range(2):
            pltpu.make_async_copy(src, xs_hbm.at[pl.ds(pos_ref[k * N_TOK + t], 1)],
                                  sem_out.at[slot]).start(priority=k)
        return carry

    lax.fori_loop(0, TM, issue, 0, unroll=8)

    @pl.when(i == N_TILES - 1)
    def _():
        for back in range(DISPATCH_BUFS - 2, -1, -1):
            drain_scatter(i - back)


def _dispatch_call(layer, pos, h2):
    return pl.pallas_call(
        _dispatch_kernel,
        out_shape=jax.ShapeDtypeStruct((N_SLOTS, D_MODEL), F32),
        grid_spec=pltpu.PrefetchScalarGridSpec(
            num_scalar_prefetch=1,
            grid=(N_TILES,),
            in_specs=[pl.BlockSpec(memory_space=pl.ANY)],
            out_specs=pl.BlockSpec(memory_space=pl.ANY),
            scratch_shapes=[
                pltpu.VMEM((DISPATCH_BUFS, TM, D_MODEL), F32),
                pltpu.SemaphoreType.DMA((DISPATCH_BUFS,)),
                pltpu.SemaphoreType.DMA((DISPATCH_BUFS,)),
            ],
        ),
        compiler_params=_params(16),
        name=f"moe_dispatch{layer}",
    )(pos, h2)


def _gmm_kernel(wt_ref, we_ref, wlo_ref, whi_ref, nw_ref, xs_ref, wg_hbm, wu_hbm, wd_hbm, ys_ref,
                wg_f, wu_f, wd_f, wg_s, wu_s, wd_s, sem, nexp_ref, *, layer):
    w = pl.program_id(0)
    n_work = nw_ref[0]
    prev = jnp.maximum(w - 1, 0)
    new_expert = (w == 0) | (we_ref[w] != we_ref[prev])
    new_tile = (w == 0) | (wt_ref[w] != wt_ref[prev])

    def weight_copies(expert, slot):
        return [pltpu.make_async_copy(hbm.at[layer, expert], buf.at[slot], sem.at[slot, k])
                for k, (hbm, buf) in enumerate(((wg_hbm, wg_f), (wu_hbm, wu_f), (wd_hbm, wd_f)))]

    @pl.when(w == 0)
    def _():
        nexp_ref[0] = 0
        for cp in weight_copies(we_ref[0], 0):
            cp.start()

    @pl.when((w < n_work) & new_expert)
    def _():
        slot = nexp_ref[0] % 2
        nexp_ref[0] += 1
        nxt = lax.while_loop(
            lambda k: (k < n_work) & (we_ref[jnp.minimum(k, MAX_WORK - 1)] == we_ref[w]),
            lambda k: k + 1, w + 1)

        @pl.when(nxt < n_work)
        def _():
            for cp in weight_copies(we_ref[jnp.minimum(nxt, MAX_WORK - 1)], 1 - slot):
                cp.start()

        for cp in weight_copies(we_ref[w], slot):
            cp.wait()
        wg_s[...] = wg_f[slot].astype(BF16)
        wu_s[...] = wu_f[slot].astype(BF16)
        wd_s[...] = wd_f[slot].astype(BF16)

    @pl.when(w < n_work)
    def _():
        x = xs_ref[...].astype(BF16)
        gate = jnp.dot(x, wg_s[...], preferred_element_type=F32)
        up = jnp.dot(x, wu_s[...], preferred_element_type=F32)
        row = wt_ref[w] * TM + lax.broadcasted_iota(jnp.int32, (TM, 1), 0)
        mine = (row >= wlo_ref[w]) & (row < whi_ref[w])
        hid = jnp.where(mine, (gate * jax.nn.sigmoid(gate)) * up, 0.0)
        y = jnp.dot(hid.astype(BF16), wd_s[...], preferred_element_type=F32)

        @pl.when(new_tile)
        def _():
            ys_ref[...] = y

        @pl.when(jnp.logical_not(new_tile))
        def _():
            ys_ref[...] += y


def _gmm_call(layer, xs, work, w_gate, w_up, w_down):
    def tile_spec():
        return pl.BlockSpec((TM, D_MODEL), lambda w, wt, we, wlo, whi, nw: (wt[w], 0))

    up_shape, down_shape = (D_MODEL, EXPERT_HIDDEN), (EXPERT_HIDDEN, D_MODEL)
    hbm = pl.BlockSpec(memory_space=pl.ANY)
    grid_spec = pltpu.PrefetchScalarGridSpec(
        num_scalar_prefetch=5,
        grid=(MAX_WORK,),
        in_specs=[tile_spec(), hbm, hbm, hbm],
        out_specs=tile_spec(),
        scratch_shapes=[
            pltpu.VMEM((2,) + up_shape, F32),
            pltpu.VMEM((2,) + up_shape, F32),
            pltpu.VMEM((2,) + down_shape, F32),
            pltpu.VMEM(up_shape, BF16),
            pltpu.VMEM(up_shape, BF16),
            pltpu.VMEM(down_shape, BF16),
            pltpu.SemaphoreType.DMA((2, 3)),
            pltpu.SMEM((1,), jnp.int32),
        ],
    )
    return pl.pallas_call(
        functools.partial(_gmm_kernel, layer=layer),
        out_shape=jax.ShapeDtypeStruct((N_SLOTS, D_MODEL), F32),
        grid_spec=grid_spec,
        compiler_params=_params(48),
        name=f"moe_gmm{layer}",
    )(*work, xs, w_gate, w_up, w_down)


def _combine_kernel(pos_ref, x_ref, route_ref, mod_ref, gfin_ref, ys_hbm, *rest, final):
    if final:
        octx_ref, olat_ref, buf1, buf2, sem1, sem2 = rest
    else:
        o_ref, buf1, buf2, sem1, sem2 = rest
    i = pl.program_id(0)

    def fetch(tile):
        slot = tile % 2

        def issue(r, carry):
            t = tile * TM + r
            pltpu.make_async_copy(ys_hbm.at[pl.ds(pos_ref[t], 1)], buf1.at[slot, pl.ds(r, 1)],
                                  sem1.at[slot]).start(priority=0)
            pltpu.make_async_copy(ys_hbm.at[pl.ds(pos_ref[N_TOK + t], 1)], buf2.at[slot, pl.ds(r, 1)],
                                  sem2.at[slot]).start(priority=1)
            return carry

        lax.fori_loop(0, TM, issue, 0, unroll=8)

    @pl.when(i == 0)
    def _():
        fetch(i)

    @pl.when(i + 1 < N_TILES)
    def _():
        fetch(i + 1)

    slot = i % 2
    pltpu.make_async_copy(ys_hbm.at[pl.ds(0, TM)], buf1.at[slot], sem1.at[slot]).wait()
    pltpu.make_async_copy(ys_hbm.at[pl.ds(0, TM)], buf2.at[slot], sem2.at[slot]).wait()
    r = route_ref[...]
    moe = r[:, 2:3] * buf1[slot] + r[:, 3:4] * buf2[slot]
    x = x_ref[...] + mod_ref[5:6, :] * moe
    if not final:
        o_ref[...] = x
        return
    y = x * lax.rsqrt(jnp.mean(x * x, axis=-1, keepdims=True) + EPS) * gfin_ref[...]

    @pl.when(i < CTX_TILES)
    def _():
        octx_ref[...] = y

    @pl.when(i >= CTX_TILES)
    def _():
        olat_ref[...] = y


def _combine_call(layer, pos, x_new, route, ys, mod, g_final, final):
    def tok_spec(width):
        return pl.BlockSpec((TM, width), lambda i, pos: (i, 0))

    if final:
        out_shape = (jax.ShapeDtypeStruct((N_CTX, D_MODEL), F32), jax.ShapeDtypeStruct((N_LAT, D_MODEL), F32))
        out_specs = (pl.BlockSpec((TM, D_MODEL), lambda i, pos: (jnp.minimum(i, CTX_TILES - 1), 0)),
                     pl.BlockSpec((TM, D_MODEL), lambda i, pos: (jnp.maximum(i - CTX_TILES, 0), 0)))
    else:
        out_shape = jax.ShapeDtypeStruct((N_TOK, D_MODEL), F32)
        out_specs = tok_spec(D_MODEL)

    grid_spec = pltpu.PrefetchScalarGridSpec(
        num_scalar_prefetch=1,
        grid=(N_TILES,),
        in_specs=[
            tok_spec(D_MODEL),
            tok_spec(ROUTE_LANES),
            pl.BlockSpec((None, None, N_MOD, D_MODEL), lambda i, pos: (layer, _mod_row(i), 0, 0)),
            pl.BlockSpec((1, D_MODEL), lambda i, pos: (0, 0)),
            pl.BlockSpec(memory_space=pl.ANY),
        ],
        out_specs=out_specs,
        scratch_shapes=[
            pltpu.VMEM((2, TM, D_MODEL), F32),
            pltpu.VMEM((2, TM, D_MODEL), F32),
            pltpu.SemaphoreType.DMA((2,)),
            pltpu.SemaphoreType.DMA((2,)),
        ],
    )
    return pl.pallas_call(
        functools.partial(_combine_kernel, final=final),
        out_shape=out_shape,
        grid_spec=grid_spec,
        compiler_params=_params(40),
        name=f"moe_combine{layer}",
    )(pos, x_new, route, mod, g_final, ys)


def _rope_tables():
    rows = DEC_SEQ // GRID_W
    row_ids = jnp.repeat(jnp.arange(rows), GRID_W).astype(F32)
    col_ids = jnp.tile(jnp.arange(GRID_W), rows).astype(F32)
    freqs = ROPE_THETA ** (-jnp.arange(ROPE_QUARTER, dtype=F32) / ROPE_QUARTER)
    ar, ac = row_ids[:, None] * freqs[None, :], col_ids[:, None] * freqs[None, :]
    cos = jnp.concatenate([jnp.cos(ar), jnp.cos(ar), jnp.cos(ac), jnp.cos(ac)], axis=1)
    sin = jnp.concatenate([-jnp.sin(ar), jnp.sin(ar), -jnp.sin(ac), jnp.sin(ac)], axis=1)
    return jnp.concatenate([cos, sin], axis=1)


def _swap_partners(w):
    q = ROPE_QUARTER
    return jnp.concatenate([w[..., q:2 * q], w[..., 0:q], w[..., 3 * q:4 * q], w[..., 2 * q:3 * q]], axis=-1)


def kernel(x_prompt, x_sample, cache_ckv, cache_krope, c, c_ctx, g_mix, g_ffn, w_ada, b_ada, w_pool, pool_scale,
           w_dq, g_q, w_uq, w_dkv, g_kv, w_uk, w_uv, w_o, w_router_grp, b_router_grp, w_router_exp,
           b_router_exp, w_gate, w_up, w_down, g_final):
    n_mla = DEPTH // N_MIXERS
    x = (x_prompt.reshape(N_CTX, D_MODEL), x_sample.reshape(N_LAT, D_MODEL))

    cond = jnp.concatenate([c_ctx[None, :], c, jnp.zeros((COND_ROWS - 1 - DEC_BATCH, D_MODEL), F32)], axis=0)
    mod = _ada_call(cond, w_ada, b_ada).reshape(DEPTH, COND_ROWS, N_MOD, D_MODEL)

    wr = jnp.concatenate([w_router_exp, w_router_grp,
                          jnp.zeros((DEPTH, D_MODEL, ROUTE_LANES - N_EXPERTS - N_EXPERT_GROUPS), F32)],
                         axis=2).astype(BF16)
    br = jnp.concatenate([b_router_exp, b_router_grp,
                          jnp.zeros((DEPTH, ROUTE_LANES - N_EXPERTS - N_EXPERT_GROUPS), F32)], axis=1)
    band = jnp.asarray(_pool_band(), dtype=BF16)
    wpool = w_pool.astype(BF16)
    wdq = w_dq.astype(BF16)
    wdkv = w_dkv[:, :, :KV_RANK].astype(BF16)
    w_kr = w_dkv[:, :, KV_RANK:]
    wkr = jnp.concatenate([w_kr, _swap_partners(w_kr)], axis=-1).astype(BF16)
    wuqn = w_uq[..., :NOPE_DIM].reshape(n_mla, Q_RANK, N_HEADS * NOPE_DIM).astype(BF16)
    w_qr = w_uq[..., NOPE_DIM:]
    wuqr = jnp.concatenate([w_qr, _swap_partners(w_qr)], axis=-1).reshape(n_mla, Q_RANK, N_HEADS * LANES)
    wuqr = wuqr.astype(BF16)
    wuk = w_uk.reshape(n_mla, KV_RANK, N_HEADS * NOPE_DIM).astype(BF16)
    wuv = w_uv.reshape(n_mla, KV_RANK, N_HEADS * V_DIM).astype(BF16)
    wo = w_o.astype(BF16)
    tabk = _rope_tables()
    tabq = jnp.tile(tabk, (1, N_HEADS))
    cache_kr = jnp.pad(cache_krope, ((0, 0), (0, 0), (0, 0), (0, LANES - ROPE_DIM)))

    def row(a):
        return a.reshape(1, -1)

    ctx_ckv = jnp.zeros((BATCH, n_mla, SEQ, KV_RANK), F32)
    ctx_krope = jnp.zeros((BATCH, n_mla, SEQ, ROPE_DIM), F32)
    for layer in range(DEPTH):
        if layer % N_MIXERS == 0:
            p = layer // N_MIXERS
            streams = x if isinstance(x, tuple) else (x,)
            x_new, h2, route, counts = _pool_call(layer, streams, mod, row(g_mix[layer]), row(g_ffn[layer]),
                                                  band, wpool[p], row(pool_scale[p]), wr[layer], row(br[layer]))
        else:
            m = layer // N_MIXERS
            qn, qr, ckv, kr, ctx_ckv, ctx_krope = _qkv_call(
                layer, x, mod, row(g_mix[layer]), wdq[m], row(g_q[m]), wdkv[m], row(g_kv[m]), wkr[m], wuqn[m],
                wuqr[m], tabq, tabk, ctx_ckv, ctx_krope)
            o_ctx = _attn_call(layer, qn, qr, ckv, kr, wuk[m], wuv[m])
            o_lat = _attn_call(layer, qn, qr, ckv, kr, wuk[m], wuv[m], cache=(cache_ckv, cache_kr))
            x_new, h2, route, counts = _attn_out_call(layer, o_ctx, o_lat, wo[m], x, mod, row(g_ffn[layer]),
                                                      wr[layer], row(br[layer]))
        pos8, *work = _plan_call(layer, counts, route)
        pos = pos8[:2].reshape(N_SLOTS)
        xs = _dispatch_call(layer, pos, h2)
        ys = _gmm_call(layer, xs, work, w_gate, w_up, w_down)
        x = _combine_call(layer, pos, x_new, route, ys, mod, row(g_final), final=(layer == DEPTH - 1))

    y_prompt = x[0].reshape(BATCH, SEQ, D_MODEL)
    y_sample = x[1].reshape(DEC_BATCH, DEC_SEQ, D_MODEL)
    return y_prompt, y_sample, ctx_ckv, ctx_krope
```

```python
import functools

import numpy as np
import jax
import jax.numpy as jnp
from jax import lax
from jax.experimental import pallas as pl
from jax.experimental.pallas import tpu as pltpu

D_MODEL = 2048
BATCH = 32
SEQ = 256
DEPTH = 4
DEC_BATCH = 2
DEC_SEQ = 1024
PAST_LEN = 512
GRID_W = 64
N_MIXERS = 2
POOL_WINDOWS = (2, 4, 8, 16)
N_POOL_GROUPS = 4
POOL_CH = D_MODEL // N_POOL_GROUPS
N_HEADS = 16
Q_RANK = 512
KV_RANK = 512
NOPE_DIM = 128
ROPE_DIM = 64
V_DIM = 128
ROPE_QUARTER = ROPE_DIM // 4
ROPE_THETA = 10000.0
ATTN_SCALE = (NOPE_DIM + ROPE_DIM) ** -0.5
N_EXPERT_GROUPS = 4
EXPERTS_PER_GROUP = 8
N_EXPERTS = N_EXPERT_GROUPS * EXPERTS_PER_GROUP
EXPERT_HIDDEN = 512
N_MOD = 6
EPS = 1e-6

LANES = 128
SUBLANES = 8
TM = 256
N_CTX = BATCH * SEQ
N_LAT = DEC_BATCH * DEC_SEQ
N_TOK = N_CTX + N_LAT
CTX_TILES = N_CTX // TM
LAT_TILES = N_LAT // TM
N_TILES = CTX_TILES + LAT_TILES
LAT_TILES_PER_SEQ = DEC_SEQ // TM
COND_ROWS = SUBLANES
POOL_HALO = max(POOL_WINDOWS) // 2
POOL_K = TM + LANES
ROUTE_LANES = LANES
N_SLOTS = 2 * N_TOK
SORT_TILES = N_SLOTS // TM
MAX_WORK = SORT_TILES + N_EXPERTS - 1
MIB = 1024 * 1024

F32 = jnp.float32
BF16 = jnp.bfloat16


def _mod_row(tile):
    return jnp.where(tile < CTX_TILES, 0, 1 + jnp.maximum(tile - CTX_TILES, 0) // LAT_TILES_PER_SEQ)


def _params(vmem_mib, n_axes=1):
    return pltpu.CompilerParams(
        dimension_semantics=("arbitrary",) * n_axes,
        vmem_limit_bytes=vmem_mib * MIB,
    )


def _norm_mod(x, g, shift, scale):
    y = x * lax.rsqrt(jnp.mean(x * x, axis=-1, keepdims=True) + EPS)
    return (y * g) * (1.0 + scale) + shift


def _route(h_bf, wr_ref, br_ref, seen_ref):
    logits = jnp.dot(h_bf, wr_ref[...], preferred_element_type=F32) + br_ref[...]
    lane_i = lax.broadcasted_iota(jnp.int32, logits.shape, 1)
    lane = lane_i.astype(F32)
    neg = -jnp.inf
    far = float(ROUTE_LANES)
    gmask = (lane_i >= N_EXPERTS) & (lane_i < N_EXPERTS + N_EXPERT_GROUPS)
    gl = jnp.where(gmask, logits, neg)
    gmax = jnp.max(gl, axis=1, keepdims=True)
    grp = jnp.min(jnp.where(gl == gmax, lane, far), axis=1, keepdims=True) - N_EXPERTS
    p_grp = 1.0 / jnp.sum(jnp.exp(gl - gmax), axis=1, keepdims=True)
    lane_grp = lax.shift_right_logical(lane_i, EXPERTS_PER_GROUP.bit_length() - 1).astype(F32)
    emask = lane_grp == grp
    el = jnp.where(emask, logits, neg)
    m1 = jnp.max(el, axis=1, keepdims=True)
    i1 = jnp.min(jnp.where(el == m1, lane, far), axis=1, keepdims=True)
    el2 = jnp.where(lane == i1, neg, el)
    m2 = jnp.max(el2, axis=1, keepdims=True)
    i2 = jnp.min(jnp.where((el2 == m2) & emask & (lane != i1), lane, far), axis=1, keepdims=True)
    e2 = jnp.exp(m2 - m1)
    p1 = p_grp / (1.0 + e2)
    p2 = p1 * e2
    n_rows = logits.shape[0]
    earlier = (lax.broadcasted_iota(jnp.int32, (n_rows, n_rows), 0)
               > lax.broadcasted_iota(jnp.int32, (n_rows, n_rows), 1)).astype(BF16)
    hot1 = (lane == i1).astype(F32)
    hot2 = (lane == i2).astype(F32)
    before1 = jnp.dot(earlier, hot1.astype(BF16), preferred_element_type=F32)
    before2 = jnp.dot(earlier, hot2.astype(BF16), preferred_element_type=F32)
    cnt1 = jnp.sum(hot1, axis=0, keepdims=True)
    cnt2 = jnp.sum(hot2, axis=0, keepdims=True)
    seen = seen_ref[...]
    rank1 = jnp.sum(hot1 * (seen + before1), axis=1, keepdims=True)
    rank2 = jnp.sum(hot2 * (seen + cnt1 + before2), axis=1, keepdims=True)
    seen_ref[...] = seen + cnt1 + cnt2

    out = jnp.where(lane == 0, i1, 0.0)
    for k, col in enumerate((i2, p1, p2, rank1, rank2), start=1):
        out = jnp.where(lane == k, col, out)
    return out


def _mixer_tail(x, mix_out, mod_ref, gffn_ref, wr_ref, br_ref, xnew_ref, h2_ref, route_ref, cnt_ref, seen_ref):
    @pl.when(pl.program_id(0) == 0)
    def _():
        seen_ref[...] = jnp.zeros_like(seen_ref)

    x_new = x + mod_ref[2:3, :] * mix_out
    xnew_ref[...] = x_new
    h2 = _norm_mod(x_new, gffn_ref[...], mod_ref[3:4, :], mod_ref[4:5, :])
    h2_ref[...] = h2
    route_ref[...] = _route(h2.astype(BF16), wr_ref, br_ref, seen_ref)
    cnt_ref[...] = seen_ref[...].astype(jnp.int32)


ADA_TN = 1024


def _ada_kernel(cond_ref, w_ref, b_ref, o_ref):
    c = cond_ref[...]
    s = c * jax.nn.sigmoid(c)
    o_ref[...] = jnp.dot(s.astype(BF16), w_ref[...].astype(BF16), preferred_element_type=F32) + b_ref[...]


def _ada_call(cond, w_ada, b_ada):
    nj = (N_MOD * D_MODEL) // ADA_TN
    return pl.pallas_call(
        _ada_kernel,
        out_shape=jax.ShapeDtypeStruct((DEPTH, COND_ROWS, N_MOD * D_MODEL), F32),
        grid=(DEPTH, nj),
        in_specs=[
            pl.BlockSpec((COND_ROWS, D_MODEL), lambda l, j: (0, 0)),
            pl.BlockSpec((None, D_MODEL, ADA_TN), lambda l, j: (l, 0, j)),
            pl.BlockSpec((None, 1, ADA_TN), lambda l, j: (l, 0, j)),
        ],
        out_specs=pl.BlockSpec((None, COND_ROWS, ADA_TN), lambda l, j: (l, 0, j)),
        compiler_params=_params(40, 2),
        name="ada_mod",
    )(cond, w_ada, b_ada.reshape(DEPTH, 1, N_MOD * D_MODEL))


def _pool_band():
    col = np.arange(POOL_K)
    rel = np.where(col < TM, col, np.where(col < TM + POOL_HALO, col - TM - POOL_HALO, col - POOL_HALO))
    ok = col < TM + 2 * POOL_HALO
    row = np.arange(TM)[:, None]
    mats = []
    for w in POOL_WINDOWS:
        mats.append((rel[None, :] >= row - w // 2) & (rel[None, :] < row + (w - w // 2)) & ok[None, :])
    return np.stack(mats).astype(np.float32)


def _pool_kernel(*refs, split_streams):
    if split_streams:
        xc_ref, xl_ref, *refs = refs
    else:
        x_ref, *refs = refs
    (xp_ref, xn_ref, mod_ref, gmix_ref, gffn_ref, band_ref, wpool_ref, pscale_ref, wr_ref, br_ref,
     xnew_ref, h2_ref, route_ref, cnt_ref, seen_ref) = refs
    i = pl.program_id(0)
    lat = i >= CTX_TILES
    tile_pos = jnp.where(lat, jnp.maximum(i - CTX_TILES, 0) % LAT_TILES_PER_SEQ, 0)
    seq_len = jnp.where(lat, DEC_SEQ, SEQ)
    has_prev = tile_pos > 0
    has_next = (tile_pos + 1) * TM < seq_len

    g = gmix_ref[...]
    shift, scale = mod_ref[0:1, :], mod_ref[1:2, :]
    x = jnp.where(lat, xl_ref[...], xc_ref[...]) if split_streams else x_ref[...]
    h = _norm_mod(x, g, shift, scale)
    hp = jnp.where(has_prev, _norm_mod(xp_ref[...], g, shift, scale), 0.0)
    hn = jnp.where(has_next, _norm_mod(xn_ref[...], g, shift, scale), 0.0)
    hall = jnp.concatenate(
        [h, hp, hn, jnp.zeros((POOL_K - TM - 2 * POOL_HALO, D_MODEL), F32)], axis=0)
    hi = hall.astype(BF16)
    lo = (hall - hi.astype(F32)).astype(BF16)

    pos = tile_pos * TM + lax.broadcasted_iota(jnp.int32, (TM, 1), 0)
    outs = []
    for gi, w in enumerate(POOL_WINDOWS):
        sl = slice(gi * POOL_CH, (gi + 1) * POOL_CH)
        band = band_ref[gi]
        wsum = (jnp.dot(band, hi[:, sl], preferred_element_type=F32)
                + jnp.dot(band, lo[:, sl], preferred_element_type=F32))
        cnt = jnp.minimum(pos + (w - w // 2), seq_len) - jnp.maximum(pos - w // 2, 0)
        diff = wsum / cnt.astype(F32) - h[:, sl]
        outs.append(jnp.dot(diff.astype(BF16), wpool_ref[gi], preferred_element_type=F32))
    mix = jnp.concatenate(outs, axis=1) * pscale_ref[...]
    _mixer_tail(x, mix, mod_ref, gffn_ref, wr_ref, br_ref, xnew_ref, h2_ref, route_ref, cnt_ref, seen_ref)


def _tail_out_shapes():
    return (jax.ShapeDtypeStruct((N_TOK, D_MODEL), F32),
            jax.ShapeDtypeStruct((N_TOK, D_MODEL), F32),
            jax.ShapeDtypeStruct((N_TOK, ROUTE_LANES), F32),
            jax.ShapeDtypeStruct((1, ROUTE_LANES), jnp.int32))


def _tail_out_specs():
    return (pl.BlockSpec((TM, D_MODEL), lambda i: (i, 0)),
            pl.BlockSpec((TM, D_MODEL), lambda i: (i, 0)),
            pl.BlockSpec((TM, ROUTE_LANES), lambda i: (i, 0)),
            pl.BlockSpec((1, ROUTE_LANES), lambda i: (0, 0)))


def _tail_scratch():
    return [pltpu.VMEM((1, ROUTE_LANES), F32)]


def _row_spec(width):
    return pl.BlockSpec((1, width), lambda i: (0, 0))


def _mod_spec(layer):
    return pl.BlockSpec((None, None, N_MOD, D_MODEL), lambda i: (layer, _mod_row(i), 0, 0))


def _pool_call(layer, xs, mod, g_mix, g_ffn, band, wpool, pscale, wr, br):
    halo_blocks = TM // POOL_HALO
    split_streams = len(xs) == 2
    tile0 = CTX_TILES if split_streams else 0
    last_halo = xs[-1].shape[0] // POOL_HALO - 1

    def prev_map(i):
        return (jnp.clip((i - tile0) * halo_blocks - 1, 0, last_halo), 0)

    def next_map(i):
        return (jnp.clip((i - tile0 + 1) * halo_blocks, 0, last_halo), 0)

    if split_streams:
        main_specs = [pl.BlockSpec((TM, D_MODEL), lambda i: (jnp.minimum(i, CTX_TILES - 1), 0)),
                      pl.BlockSpec((TM, D_MODEL), lambda i: (jnp.maximum(i - CTX_TILES, 0), 0))]
    else:
        main_specs = [pl.BlockSpec((TM, D_MODEL), lambda i: (i, 0))]
    return pl.pallas_call(
        functools.partial(_pool_kernel, split_streams=split_streams),
        out_shape=_tail_out_shapes(),
        grid=(N_TILES,),
        in_specs=main_specs + [
            pl.BlockSpec((POOL_HALO, D_MODEL), prev_map),
            pl.BlockSpec((POOL_HALO, D_MODEL), next_map),
            _mod_spec(layer),
            _row_spec(D_MODEL),
            _row_spec(D_MODEL),
            pl.BlockSpec((N_POOL_GROUPS, TM, POOL_K), lambda i: (0, 0, 0)),
            pl.BlockSpec((N_POOL_GROUPS, POOL_CH, POOL_CH), lambda i: (0, 0, 0)),
            _row_spec(D_MODEL),
            pl.BlockSpec((D_MODEL, ROUTE_LANES), lambda i: (0, 0)),
            _row_spec(ROUTE_LANES),
        ],
        out_specs=_tail_out_specs(),
        scratch_shapes=_tail_scratch(),
        compiler_params=_params(48),
        name=f"pool_layer{layer}",
    )(*xs, xs[-1], xs[-1], mod, g_mix, g_ffn, band, wpool, pscale, wr, br)


def _qkv_kernel(x_ref, mod_ref, gmix_ref, wdq_ref, gq_ref, wdkv_ref, gkv_ref, wkr_ref, wuqn_ref, wuqr_ref,
                tabq_ref, tabk_ref, _ckv_leaf_in, _kr_leaf_in, qn_ref, qr_ref, ckv_ref, kr_ref, ckv_leaf_ref,
                kr_leaf_ref):
    i = pl.program_id(0)
    h = _norm_mod(x_ref[...], gmix_ref[...], mod_ref[0:1, :], mod_ref[1:2, :]).astype(BF16)

    def rms(v, g):
        return v * lax.rsqrt(jnp.mean(v * v, axis=-1, keepdims=True) + EPS) * g

    cq = rms(jnp.dot(h, wdq_ref[...], preferred_element_type=F32), gq_ref[...]).astype(BF16)
    ckv = rms(jnp.dot(h, wdkv_ref[...], preferred_element_type=F32), gkv_ref[...])
    ckv_ref[...] = ckv
    qn_ref[...] = jnp.dot(cq, wuqn_ref[...], preferred_element_type=F32).astype(BF16)
    qr = jnp.dot(cq, wuqr_ref[...], preferred_element_type=F32)
    kr = jnp.dot(h, wkr_ref[...], preferred_element_type=F32)
    lane = lax.broadcasted_iota(jnp.int32, kr.shape, 1)

    @pl.when(i < CTX_TILES)
    def _():
        qr_ref[...] = qr.astype(BF16)
        kr_ref[...] = jnp.where(lane < ROPE_DIM, kr, 0.0)
        ckv_leaf_ref[...] = ckv
        kr_leaf_ref[...] = kr[:, :ROPE_DIM]

    @pl.when(i >= CTX_TILES)
    def _():
        tq = qr * tabq_ref[...]
        qr_ref[...] = (tq + pltpu.roll(tq, N_HEADS * LANES - ROPE_DIM, 1)).astype(BF16)
        tk = kr * tabk_ref[...]
        kr_ref[...] = jnp.where(lane < ROPE_DIM, tk + pltpu.roll(tk, ROPE_DIM, 1), 0.0)


def _qkv_call(layer, x, mod, g_mix, wdq, gq, wdkv, gkv, wkr, wuqn, wuqr, tabq, tabk, ckv_leaf, kr_leaf):
    m = layer // N_MIXERS

    def leaf_spec(width):
        return pl.BlockSpec((None, None, SEQ, width), lambda i: (jnp.minimum(i, CTX_TILES - 1), m, 0, 0))

    hd = N_HEADS * LANES

    def lat_tile(i):
        return jnp.maximum(i - CTX_TILES, 0) % LAT_TILES_PER_SEQ

    def full(a):
        return pl.BlockSpec(a.shape, lambda i: (0,) * a.ndim)

    return pl.pallas_call(
        _qkv_kernel,
        out_shape=(jax.ShapeDtypeStruct((N_TOK, hd), BF16),
                   jax.ShapeDtypeStruct((N_TOK, hd), BF16),
                   jax.ShapeDtypeStruct((N_TOK, KV_RANK), F32),
                   jax.ShapeDtypeStruct((N_TOK, LANES), F32),
                   jax.ShapeDtypeStruct(ckv_leaf.shape, F32),
                   jax.ShapeDtypeStruct(kr_leaf.shape, F32)),
        grid=(N_TILES,),
        in_specs=[
            pl.BlockSpec((TM, D_MODEL), lambda i: (i, 0)),
            _mod_spec(layer),
            _row_spec(D_MODEL),
            full(wdq), _row_spec(Q_RANK), full(wdkv), _row_spec(KV_RANK), full(wkr), full(wuqn), full(wuqr),
            pl.BlockSpec((TM, hd), lambda i: (lat_tile(i), 0)),
            pl.BlockSpec((TM, LANES), lambda i: (lat_tile(i), 0)),
            pl.BlockSpec(memory_space=pl.ANY),
            pl.BlockSpec(memory_space=pl.ANY),
        ],
        out_specs=(pl.BlockSpec((TM, hd), lambda i: (i, 0)),
                   pl.BlockSpec((TM, hd), lambda i: (i, 0)),
                   pl.BlockSpec((TM, KV_RANK), lambda i: (i, 0)),
                   pl.BlockSpec((TM, LANES), lambda i: (i, 0)),
                   leaf_spec(KV_RANK),
                   leaf_spec(ROPE_DIM)),
        input_output_aliases={12: 4, 13: 5},
        compiler_params=_params(48),
        name=f"mla_qkv{layer}",
    )(x, mod, g_mix, wdq, gq, wdkv, gkv, wkr, wuqn, wuqr, tabq, tabk, ckv_leaf, kr_leaf)


HEAD_K = 2 * LANES


def _attn_kernel(*refs, n_new, n_cache):
    if n_cache:
        (qn_ref, qr_ref, ckv_ref, kr_ref, cckv_ref, ckr_ref, wuk_ref, wuv_ref, o_ref, kcat, vs) = refs
    else:
        (qn_ref, qr_ref, ckv_ref, kr_ref, wuk_ref, wuv_ref, o_ref, kcat, vs) = refs
        cckv_ref = ckr_ref = None

    @pl.when(pl.program_id(1) == 0)
    def _():
        def expand(src_ref, kr_src_ref, n_rows, base):
            for c in range(n_rows // TM):
                rows = slice(c * TM, (c + 1) * TM)
                dst = slice(base + c * TM, base + (c + 1) * TM)
                lat = src_ref[rows, :].astype(BF16)
                kn = jnp.dot(lat, wuk_ref[...], preferred_element_type=F32).astype(BF16)
                vs[dst, :] = jnp.dot(lat, wuv_ref[...], preferred_element_type=F32).astype(BF16)
                krb = kr_src_ref[rows, :].astype(BF16)
                for hh in range(N_HEADS):
                    kcat[dst, hh * HEAD_K:hh * HEAD_K + LANES] = kn[:, hh * LANES:(hh + 1) * LANES]
                    kcat[dst, hh * HEAD_K + LANES:(hh + 1) * HEAD_K] = krb

        expand(ckv_ref, kr_ref, n_new, 0)
        if n_cache:
            expand(cckv_ref, ckr_ref, n_cache, n_new)

    for hh in range(N_HEADS):
        hs = slice(hh * LANES, (hh + 1) * LANES)
        q = jnp.concatenate([qn_ref[:, hs], qr_ref[:, hs]], axis=1)
        s = lax.dot_general(q, kcat[:, hh * HEAD_K:(hh + 1) * HEAD_K], (((1,), (1,)), ((), ())),
                            preferred_element_type=F32) * ATTN_SCALE
        p = jnp.exp(s - jnp.max(s, axis=1, keepdims=True))
        den = jnp.sum(p, axis=1, keepdims=True)
        o = jnp.dot(p.astype(BF16), vs[:, hs], preferred_element_type=F32) / den
        o_ref[:, hs] = o.astype(BF16)


def _attn_call(layer, qn, qr, ckv, kr, wuk, wuv, cache=None):
    hd = N_HEADS * LANES
    if cache is None:
        n_b, n_q, n_new, n_cache, tile0 = BATCH, SEQ // TM, SEQ, 0, 0
    else:
        n_b, n_q, n_new, n_cache, tile0 = DEC_BATCH, DEC_SEQ // TM, DEC_SEQ, PAST_LEN, CTX_TILES
    seq_blk0 = tile0 * TM // n_new
    m = layer // N_MIXERS

    def q_map(b, j):
        return (tile0 + b * n_q + j, 0)

    in_specs = [
        pl.BlockSpec((TM, hd), q_map),
        pl.BlockSpec((TM, hd), q_map),
        pl.BlockSpec((n_new, KV_RANK), lambda b, j: (seq_blk0 + b, 0)),
        pl.BlockSpec((n_new, LANES), lambda b, j: (seq_blk0 + b, 0)),
    ]
    args = [qn, qr, ckv, kr]
    if cache is not None:
        in_specs += [
            pl.BlockSpec((None, None, n_cache, KV_RANK), lambda b, j: (b, m, 0, 0)),
            pl.BlockSpec((None, None, n_cache, LANES), lambda b, j: (b, m, 0, 0)),
        ]
        args += list(cache)
    in_specs += [
        pl.BlockSpec((KV_RANK, hd), lambda b, j: (0, 0)),
        pl.BlockSpec((KV_RANK, hd), lambda b, j: (0, 0)),
    ]
    args += [wuk, wuv]
    n_keys = n_new + n_cache
    return pl.pallas_call(
        functools.partial(_attn_kernel, n_new=n_new, n_cache=n_cache),
        out_shape=jax.ShapeDtypeStruct((n_b * n_q * TM, hd), BF16),
        grid=(n_b, n_q),
        in_specs=in_specs,
        out_specs=pl.BlockSpec((TM, hd), lambda b, j: (b * n_q + j, 0)),
        scratch_shapes=[pltpu.VMEM((n_keys, N_HEADS * HEAD_K), BF16), pltpu.VMEM((n_keys, hd), BF16)],
        compiler_params=_params(56, 2),
        name=f"mla_attn{layer}_{'lat' if cache is not None else 'ctx'}",
    )(*args)


def _attn_out_kernel(octx_ref, olat_ref, wo_ref, x_ref, mod_ref, gffn_ref, wr_ref, br_ref, xnew_ref, h2_ref,
                     route_ref, cnt_ref, seen_ref):
    o = jnp.where(pl.program_id(0) >= CTX_TILES, olat_ref[...], octx_ref[...])
    mix = jnp.dot(o, wo_ref[...], preferred_element_type=F32)
    _mixer_tail(x_ref[...], mix, mod_ref, gffn_ref, wr_ref, br_ref, xnew_ref, h2_ref, route_ref, cnt_ref,
                seen_ref)


def _attn_out_call(layer, o_ctx, o_lat, wo, x, mod, g_ffn, wr, br):
    hd = N_HEADS * V_DIM
    return pl.pallas_call(
        _attn_out_kernel,
        out_shape=_tail_out_shapes(),
        grid=(N_TILES,),
        in_specs=[
            pl.BlockSpec((TM, hd), lambda i: (jnp.minimum(i, CTX_TILES - 1), 0)),
            pl.BlockSpec((TM, hd), lambda i: (jnp.maximum(i - CTX_TILES, 0), 0)),
            pl.BlockSpec((hd, D_MODEL), lambda i: (0, 0)),
            pl.BlockSpec((TM, D_MODEL), lambda i: (i, 0)),
            _mod_spec(layer),
            _row_spec(D_MODEL),
            pl.BlockSpec((D_MODEL, ROUTE_LANES), lambda i: (0, 0)),
            _row_spec(ROUTE_LANES),
        ],
        out_specs=_tail_out_specs(),
        scratch_shapes=_tail_scratch(),
        compiler_params=_params(48),
        name=f"mla_out{layer}",
    )(o_ctx, o_lat, wo, x, mod, g_ffn, wr, br)


def _plan_kernel(cnt_ref, route_ref, pos_ref, wt_ref, we_ref, wlo_ref, whi_ref, nw_ref, start_ref):
    @pl.when(pl.program_id(0) == 0)
    def _():
        lane = lax.broadcasted_iota(jnp.int32, (1, ROUTE_LANES), 1)
        shift = TM.bit_length() - 1

        def per_expert(e, carry):
            n_work, lo, last_e, starts = carry
            cnt = cnt_ref[0, e]
            hi = lo + cnt
            first_tile = lax.shift_right_logical(lo, shift)
            end_tile = jnp.where(cnt > 0, lax.shift_right_logical(hi + (TM - 1), shift), first_tile)

            def per_tile(t, w):
                wt_ref[w] = t
                we_ref[w] = e
                wlo_ref[w] = lo
                whi_ref[w] = hi
                return w + 1

            n_work = lax.fori_loop(first_tile, end_tile, per_tile, n_work)
            starts = jnp.where(lane == e, lo.astype(F32), starts)
            return n_work, hi, jnp.where(cnt > 0, e, last_e), starts

        n_work, _, last_e, starts = lax.fori_loop(
            0, N_EXPERTS, per_expert,
            (jnp.int32(0), jnp.int32(0), jnp.int32(0), jnp.zeros((1, ROUTE_LANES), F32)))
        nw_ref[0] = n_work
        start_ref[...] = starts

        def fill(w, carry):
            wt_ref[w] = SORT_TILES - 1
            we_ref[w] = last_e
            wlo_ref[w] = 0
            whi_ref[w] = 0
            return carry

        lax.fori_loop(n_work, MAX_WORK, fill, 0)

    r = route_ref[...]
    lane = lax.broadcasted_iota(jnp.int32, r.shape, 1).astype(F32)
    starts = start_ref[...]
    pos1 = jnp.sum(jnp.where(lane == r[:, 0:1], starts, 0.0), axis=1, keepdims=True) + r[:, 4:5]
    pos2 = jnp.sum(jnp.where(lane == r[:, 1:2], starts, 0.0), axis=1, keepdims=True) + r[:, 5:6]
    pos = jnp.where(lane == 0.0, pos1, jnp.where(lane == 1.0, pos2, 0.0))
    pos_ref[...] = pos.T[0:SUBLANES, :].astype(jnp.int32)


PLAN_ROWS = 4 * TM


def _plan_call(layer, counts, route):
    smem = pl.BlockSpec(memory_space=pltpu.SMEM)
    work = jax.ShapeDtypeStruct((MAX_WORK,), jnp.int32)
    return pl.pallas_call(
        _plan_kernel,
        out_shape=(jax.ShapeDtypeStruct((SUBLANES, N_TOK), jnp.int32), work, work, work, work,
                   jax.ShapeDtypeStruct((1,), jnp.int32)),
        grid=(N_TOK // PLAN_ROWS,),
        in_specs=[smem, pl.BlockSpec((PLAN_ROWS, ROUTE_LANES), lambda i: (i, 0))],
        out_specs=(pl.BlockSpec((SUBLANES, PLAN_ROWS), lambda i: (0, i)), smem, smem, smem, smem, smem),
        scratch_shapes=[pltpu.VMEM((1, ROUTE_LANES), F32)],
        compiler_params=_params(16),
        name=f"moe_plan{layer}",
    )(counts, route)


DISPATCH_BUFS = 3


def _dispatch_kernel(pos_ref, h2_hbm, xs_hbm, buf, sem_in, sem_out):
    i = pl.program_id(0)

    def load(tile):
        slot = tile % DISPATCH_BUFS
        return pltpu.make_async_copy(h2_hbm.at[pl.ds(tile * TM, TM)], buf.at[slot], sem_in.at[slot])

    def drain_scatter(tile):
        slot = tile % DISPATCH_BUFS
        for _ in range(2):
            pltpu.make_async_copy(buf.at[slot], xs_hbm.at[pl.ds(0, TM)], sem_out.at[slot]).wait()

    @pl.when(i == 0)
    def _():
        load(i).start()

    @pl.when(i >= DISPATCH_BUFS - 1)
    def _():
        drain_scatter(i - (DISPATCH_BUFS - 1))

    @pl.when(i + 1 < N_TILES)
    def _():
        load(i + 1).start()

    load(i).wait()
    slot = i % DISPATCH_BUFS

    def issue(r, carry):
        t = i * TM + r
        src = buf.at[slot, pl.ds(r, 1)]
        for k in range(2):
            pltpu.make_async_copy(src, xs_hbm.at[pl.ds(pos_ref[k * N_TOK + t], 1)],
                                  sem_out.at[slot]).start(priority=k)
        return carry

    for r in range(TM):
        issue(r, 0)

    @pl.when(i == N_TILES - 1)
    def _():
        for back in range(DISPATCH_BUFS - 2, -1, -1):
            drain_scatter(i - back)


def _dispatch_call(layer, pos, h2):
    return pl.pallas_call(
        _dispatch_kernel,
        out_shape=jax.ShapeDtypeStruct((N_SLOTS, D_MODEL), F32),
        grid_spec=pltpu.PrefetchScalarGridSpec(
            num_scalar_prefetch=1,
            grid=(N_TILES,),
            in_specs=[pl.BlockSpec(memory_space=pl.ANY)],
            out_specs=pl.BlockSpec(memory_space=pl.ANY),
            scratch_shapes=[
                pltpu.VMEM((DISPATCH_BUFS, TM, D_MODEL), F32),
                pltpu.SemaphoreType.DMA((DISPATCH_BUFS,)),
                pltpu.SemaphoreType.DMA((DISPATCH_BUFS,)),
            ],
        ),
        compiler_params=_params(16),
        name=f"moe_dispatch{layer}",
    )(pos, h2)


def _gmm_kernel(wt_ref, we_ref, wlo_ref, whi_ref, nw_ref, xs_ref, wg_hbm, wu_hbm, wd_hbm, ys_ref,
                wg_f, wu_f, wd_f, wg_s, wu_s, wd_s, sem, nexp_ref, *, layer):
    w = pl.program_id(0)
    n_work = nw_ref[0]
    prev = jnp.maximum(w - 1, 0)
    new_expert = (w == 0) | (we_ref[w] != we_ref[prev])
    new_tile = (w == 0) | (wt_ref[w] != wt_ref[prev])

    def weight_copies(expert, slot):
        return [pltpu.make_async_copy(hbm.at[layer, expert], buf.at[slot], sem.at[slot, k])
                for k, (hbm, buf) in enumerate(((wg_hbm, wg_f), (wu_hbm, wu_f), (wd_hbm, wd_f)))]

    @pl.when(w == 0)
    def _():
        nexp_ref[0] = 0
        for cp in weight_copies(we_ref[0], 0):
            cp.start()

    @pl.when((w < n_work) & new_expert)
    def _():
        slot = nexp_ref[0] % 2
        nexp_ref[0] += 1
        nxt = lax.while_loop(
            lambda k: (k < n_work) & (we_ref[jnp.minimum(k, MAX_WORK - 1)] == we_ref[w]),
            lambda k: k + 1, w + 1)

        @pl.when(nxt < n_work)
        def _():
            for cp in weight_copies(we_ref[jnp.minimum(nxt, MAX_WORK - 1)], 1 - slot):
                cp.start()

        for cp in weight_copies(we_ref[w], slot):
            cp.wait()
        wg_s[...] = wg_f[slot].astype(BF16)
        wu_s[...] = wu_f[slot].astype(BF16)
        wd_s[...] = wd_f[slot].astype(BF16)

    @pl.when(w < n_work)
    def _():
        x = xs_ref[...].astype(BF16)
        gate = jnp.dot(x, wg_s[...], preferred_element_type=F32)
        up = jnp.dot(x, wu_s[...], preferred_element_type=F32)
        row = wt_ref[w] * TM + lax.broadcasted_iota(jnp.int32, (TM, 1), 0)
        mine = (row >= wlo_ref[w]) & (row < whi_ref[w])
        hid = jnp.where(mine, (gate * jax.nn.sigmoid(gate)) * up, 0.0)
        y = jnp.dot(hid.astype(BF16), wd_s[...], preferred_element_type=F32)

        @pl.when(new_tile)
        def _():
            ys_ref[...] = y

        @pl.when(jnp.logical_not(new_tile))
        def _():
            ys_ref[...] += y


def _gmm_call(layer, xs, work, w_gate, w_up, w_down):
    def tile_spec():
        return pl.BlockSpec((TM, D_MODEL), lambda w, wt, we, wlo, whi, nw: (wt[w], 0))

    up_shape, down_shape = (D_MODEL, EXPERT_HIDDEN), (EXPERT_HIDDEN, D_MODEL)
    hbm = pl.BlockSpec(memory_space=pl.ANY)
    grid_spec = pltpu.PrefetchScalarGridSpec(
        num_scalar_prefetch=5,
        grid=(MAX_WORK,),
        in_specs=[tile_spec(), hbm, hbm, hbm],
        out_specs=tile_spec(),
        scratch_shapes=[
            pltpu.VMEM((2,) + up_shape, F32),
            pltpu.VMEM((2,) + up_shape, F32),
            pltpu.VMEM((2,) + down_shape, F32),
            pltpu.VMEM(up_shape, BF16),
            pltpu.VMEM(up_shape, BF16),
            pltpu.VMEM(down_shape, BF16),
            pltpu.SemaphoreType.DMA((2, 3)),
            pltpu.SMEM((1,), jnp.int32),
        ],
    )
    return pl.pallas_call(
        functools.partial(_gmm_kernel, layer=layer),
        out_shape=jax.ShapeDtypeStruct((N_SLOTS, D_MODEL), F32),
        grid_spec=grid_spec,
        compiler_params=_params(48),
        name=f"moe_gmm{layer}",
    )(*work, xs, w_gate, w_up, w_down)


def _combine_kernel(pos_ref, x_ref, route_ref, mod_ref, gfin_ref, ys_hbm, *rest, final):
    if final:
        octx_ref, olat_ref, buf1, buf2, sem1, sem2 = rest
    else:
        o_ref, buf1, buf2, sem1, sem2 = rest
    i = pl.program_id(0)

    def fetch(tile, slot):
        def issue(r, carry):
            t = tile * TM + r
            pltpu.make_async_copy(ys_hbm.at[pl.ds(pos_ref[t], 1)], buf1.at[slot, pl.ds(r, 1)],
                                  sem1.at[slot]).start(priority=0)
            pltpu.make_async_copy(ys_hbm.at[pl.ds(pos_ref[N_TOK + t], 1)], buf2.at[slot, pl.ds(r, 1)],
                                  sem2.at[slot]).start(priority=1)
            return carry

        for r in range(TM):
            issue(r, 0)

    def wait_rows(slot):
        pltpu.make_async_copy(ys_hbm.at[pl.ds(0, TM)], buf1.at[slot], sem1.at[slot]).wait()
        pltpu.make_async_copy(ys_hbm.at[pl.ds(0, TM)], buf2.at[slot], sem2.at[slot]).wait()

    @pl.when(i == 0)
    def _():
        fetch(i, 0)

    slot = i % 2

    @pl.when(i + 1 < N_TILES)
    def _():
        fetch(i + 1, 1 - slot)

    wait_rows(slot)
    r = route_ref[...]
    moe = r[:, 2:3] * buf1[slot] + r[:, 3:4] * buf2[slot]
    x = x_ref[...] + mod_ref[5:6, :] * moe
    if final:
        y = x * lax.rsqrt(jnp.mean(x * x, axis=-1, keepdims=True) + EPS) * gfin_ref[...]

        @pl.when(i < CTX_TILES)
        def _():
            octx_ref[...] = y

        @pl.when(i >= CTX_TILES)
        def _():
            olat_ref[...] = y
    else:
        o_ref[...] = x


def _combine_call(layer, pos, x_new, route, ys, mod, g_final, final):
    def tok_spec(width):
        return pl.BlockSpec((TM, width), lambda i, pos: (i, 0))

    if final:
        out_shape = (jax.ShapeDtypeStruct((N_CTX, D_MODEL), F32), jax.ShapeDtypeStruct((N_LAT, D_MODEL), F32))
        out_specs = (pl.BlockSpec((TM, D_MODEL), lambda i, pos: (jnp.minimum(i, CTX_TILES - 1), 0)),
                     pl.BlockSpec((TM, D_MODEL), lambda i, pos: (jnp.maximum(i - CTX_TILES, 0), 0)))
    else:
        out_shape = jax.ShapeDtypeStruct((N_TOK, D_MODEL), F32)
        out_specs = tok_spec(D_MODEL)

    grid_spec = pltpu.PrefetchScalarGridSpec(
        num_scalar_prefetch=1,
        grid=(N_TILES,),
        in_specs=[
            tok_spec(D_MODEL),
            tok_spec(ROUTE_LANES),
            pl.BlockSpec((None, None, N_MOD, D_MODEL), lambda i, pos: (layer, _mod_row(i), 0, 0)),
            pl.BlockSpec((1, D_MODEL), lambda i, pos: (0, 0)),
            pl.BlockSpec(memory_space=pl.ANY),
        ],
        out_specs=out_specs,
        scratch_shapes=[
            pltpu.VMEM((2, TM, D_MODEL), F32),
            pltpu.VMEM((2, TM, D_MODEL), F32),
            pltpu.SemaphoreType.DMA((2,)),
            pltpu.SemaphoreType.DMA((2,)),
        ],
    )
    return pl.pallas_call(
        functools.partial(_combine_kernel, final=final),
        out_shape=out_shape,
        grid_spec=grid_spec,
        compiler_params=_params(40),
        name=f"moe_combine{layer}",
    )(pos, x_new, route, mod, g_final, ys)


def _rope_tables():
    rows = DEC_SEQ // GRID_W
    row_ids = jnp.repeat(jnp.arange(rows), GRID_W).astype(F32)
    col_ids = jnp.tile(jnp.arange(GRID_W), rows).astype(F32)
    freqs = ROPE_THETA ** (-jnp.arange(ROPE_QUARTER, dtype=F32) / ROPE_QUARTER)
    ar, ac = row_ids[:, None] * freqs[None, :], col_ids[:, None] * freqs[None, :]
    cos = jnp.concatenate([jnp.cos(ar), jnp.cos(ar), jnp.cos(ac), jnp.cos(ac)], axis=1)
    sin = jnp.concatenate([-jnp.sin(ar), jnp.sin(ar), -jnp.sin(ac), jnp.sin(ac)], axis=1)
    return jnp.concatenate([cos, sin], axis=1)


def _swap_partners(w):
    q = ROPE_QUARTER
    return jnp.concatenate([w[..., q:2 * q], w[..., 0:q], w[..., 3 * q:4 * q], w[..., 2 * q:3 * q]], axis=-1)


def kernel(x_prompt, x_sample, cache_ckv, cache_krope, c, c_ctx, g_mix, g_ffn, w_ada, b_ada, w_pool, pool_scale,
           w_dq, g_q, w_uq, w_dkv, g_kv, w_uk, w_uv, w_o, w_router_grp, b_router_grp, w_router_exp,
           b_router_exp, w_gate, w_up, w_down, g_final):
    n_mla = DEPTH // N_MIXERS
    x = (x_prompt.reshape(N_CTX, D_MODEL), x_sample.reshape(N_LAT, D_MODEL))

    cond = jnp.concatenate([c_ctx[None, :], c, jnp.zeros((COND_ROWS - 1 - DEC_BATCH, D_MODEL), F32)], axis=0)
    mod = _ada_call(cond, w_ada, b_ada).reshape(DEPTH, COND_ROWS, N_MOD, D_MODEL)

    wr = jnp.concatenate([w_router_exp, w_router_grp,
                          jnp.zeros((DEPTH, D_MODEL, ROUTE_LANES - N_EXPERTS - N_EXPERT_GROUPS), F32)],
                         axis=2).astype(BF16)
    br = jnp.concatenate([b_router_exp, b_router_grp,
                          jnp.zeros((DEPTH, ROUTE_LANES - N_EXPERTS - N_EXPERT_GROUPS), F32)], axis=1)
    band = jnp.asarray(_pool_band(), dtype=BF16)
    wpool = w_pool.astype(BF16)
    wdq = w_dq.astype(BF16)
    wdkv = w_dkv[:, :, :KV_RANK].astype(BF16)
    w_kr = w_dkv[:, :, KV_RANK:]
    wkr = jnp.concatenate([w_kr, _swap_partners(w_kr)], axis=-1).astype(BF16)
    wuqn = w_uq[..., :NOPE_DIM].reshape(n_mla, Q_RANK, N_HEADS * NOPE_DIM).astype(BF16)
    w_qr = w_uq[..., NOPE_DIM:]
    wuqr = jnp.concatenate([w_qr, _swap_partners(w_qr)], axis=-1).reshape(n_mla, Q_RANK, N_HEADS * LANES)
    wuqr = wuqr.astype(BF16)
    wuk = w_uk.reshape(n_mla, KV_RANK, N_HEADS * NOPE_DIM).astype(BF16)
    wuv = w_uv.reshape(n_mla, KV_RANK, N_HEADS * V_DIM).astype(BF16)
    wo = w_o.astype(BF16)
    tabk = _rope_tables()
    tabq = jnp.tile(tabk, (1, N_HEADS))
    cache_kr = jnp.pad(cache_krope, ((0, 0), (0, 0), (0, 0), (0, LANES - ROPE_DIM)))

    def row(a):
        return a.reshape(1, -1)

    ctx_ckv = jnp.zeros((BATCH, n_mla, SEQ, KV_RANK), F32)
    ctx_krope = jnp.zeros((BATCH, n_mla, SEQ, ROPE_DIM), F32)
    for layer in range(DEPTH):
        if layer % N_MIXERS == 0:
            p = layer // N_MIXERS
            streams = x if isinstance(x, tuple) else (x,)
            x_new, h2, route, counts = _pool_call(layer, streams, mod, row(g_mix[layer]), row(g_ffn[layer]),
                                                  band, wpool[p], row(pool_scale[p]), wr[layer], row(br[layer]))
        else:
            m = layer // N_MIXERS
            qn, qr, ckv, kr, ctx_ckv, ctx_krope = _qkv_call(
                layer, x, mod, row(g_mix[layer]), wdq[m], row(g_q[m]), wdkv[m], row(g_kv[m]), wkr[m], wuqn[m],
                wuqr[m], tabq, tabk, ctx_ckv, ctx_krope)
            o_ctx = _attn_call(layer, qn, qr, ckv, kr, wuk[m], wuv[m])
            o_lat = _attn_call(layer, qn, qr, ckv, kr, wuk[m], wuv[m], cache=(cache_ckv, cache_kr))
            x_new, h2, route, counts = _attn_out_call(layer, o_ctx, o_lat, wo[m], x, mod, row(g_ffn[layer]),
                                                      wr[layer], row(br[layer]))
        pos8, *work = _plan_call(layer, counts, route)
        pos = pos8[:2].reshape(N_SLOTS)
        xs = _dispatch_call(layer, pos, h2)
        ys = _gmm_call(layer, xs, work, w_gate, w_up, w_down)
        x = _combine_call(layer, pos, x_new, route, ys, mod, row(g_final), final=(layer == DEPTH - 1))

    y_prompt = x[0].reshape(BATCH, SEQ, D_MODEL)
    y_sample = x[1].reshape(DEC_BATCH, DEC_SEQ, D_MODEL)
    return y_prompt, y_sample, ctx_ckv, ctx_krope
```

```python
import functools

import numpy as np
import jax
import jax.numpy as jnp
from jax import lax
from jax.experimental import pallas as pl
from jax.experimental.pallas import tpu as pltpu

D_MODEL = 2048
BATCH = 32
SEQ = 256
DEPTH = 4
DEC_BATCH = 2
DEC_SEQ = 1024
PAST_LEN = 512
GRID_W = 64
N_MIXERS = 2
POOL_WINDOWS = (2, 4, 8, 16)
N_POOL_GROUPS = 4
POOL_CH = D_MODEL // N_POOL_GROUPS
N_HEADS = 16
Q_RANK = 512
KV_RANK = 512
NOPE_DIM = 128
ROPE_DIM = 64
V_DIM = 128
ROPE_QUARTER = ROPE_DIM // 4
ROPE_THETA = 10000.0
ATTN_SCALE = (NOPE_DIM + ROPE_DIM) ** -0.5
N_EXPERT_GROUPS = 4
EXPERTS_PER_GROUP = 8
N_EXPERTS = N_EXPERT_GROUPS * EXPERTS_PER_GROUP
EXPERT_HIDDEN = 512
N_MOD = 6
EPS = 1e-6

LANES = 128
SUBLANES = 8
TM = 256
N_CTX = BATCH * SEQ
N_LAT = DEC_BATCH * DEC_SEQ
N_TOK = N_CTX + N_LAT
CTX_TILES = N_CTX // TM
LAT_TILES = N_LAT // TM
N_TILES = CTX_TILES + LAT_TILES
LAT_TILES_PER_SEQ = DEC_SEQ // TM
COND_ROWS = SUBLANES
POOL_HALO = max(POOL_WINDOWS) // 2
POOL_K = TM + LANES
ROUTE_LANES = LANES
N_SLOTS = 2 * N_TOK
SORT_TILES = N_SLOTS // TM
MAX_WORK = SORT_TILES + N_EXPERTS - 1
MIB = 1024 * 1024

F32 = jnp.float32
BF16 = jnp.bfloat16


def _mod_row(tile):
    return jnp.where(tile < CTX_TILES, 0, 1 + jnp.maximum(tile - CTX_TILES, 0) // LAT_TILES_PER_SEQ)


def _params(vmem_mib, n_axes=1):
    return pltpu.CompilerParams(
        dimension_semantics=("arbitrary",) * n_axes,
        vmem_limit_bytes=vmem_mib * MIB,
    )


def _norm_mod(x, g, shift, scale):
    y = x * lax.rsqrt(jnp.mean(x * x, axis=-1, keepdims=True) + EPS)
    return (y * g) * (1.0 + scale) + shift


def _route(h_bf, wr_ref, br_ref, seen_ref):
    logits = jnp.dot(h_bf, wr_ref[...], preferred_element_type=F32) + br_ref[...]
    lane_i = lax.broadcasted_iota(jnp.int32, logits.shape, 1)
    lane = lane_i.astype(F32)
    neg = -jnp.inf
    far = float(ROUTE_LANES)
    gmask = (lane_i >= N_EXPERTS) & (lane_i < N_EXPERTS + N_EXPERT_GROUPS)
    gl = jnp.where(gmask, logits, neg)
    gmax = jnp.max(gl, axis=1, keepdims=True)
    grp = jnp.min(jnp.where(gl == gmax, lane, far), axis=1, keepdims=True) - N_EXPERTS
    p_grp = 1.0 / jnp.sum(jnp.exp(gl - gmax), axis=1, keepdims=True)
    lane_grp = lax.shift_right_logical(lane_i, EXPERTS_PER_GROUP.bit_length() - 1).astype(F32)
    emask = lane_grp == grp
    el = jnp.where(emask, logits, neg)
    m1 = jnp.max(el, axis=1, keepdims=True)
    i1 = jnp.min(jnp.where(el == m1, lane, far), axis=1, keepdims=True)
    el2 = jnp.where(lane == i1, neg, el)
    m2 = jnp.max(el2, axis=1, keepdims=True)
    i2 = jnp.min(jnp.where((el2 == m2) & emask & (lane != i1), lane, far), axis=1, keepdims=True)
    e2 = jnp.exp(m2 - m1)
    p1 = p_grp / (1.0 + e2)
    p2 = p1 * e2
    n_rows = logits.shape[0]
    earlier = (lax.broadcasted_iota(jnp.int32, (n_rows, n_rows), 0)
               > lax.broadcasted_iota(jnp.int32, (n_rows, n_rows), 1)).astype(BF16)
    hot1 = (lane == i1).astype(F32)
    hot2 = (lane == i2).astype(F32)
    before1 = jnp.dot(earlier, hot1.astype(BF16), preferred_element_type=F32)
    before2 = jnp.dot(earlier, hot2.astype(BF16), preferred_element_type=F32)
    cnt1 = jnp.sum(hot1, axis=0, keepdims=True)
    cnt2 = jnp.sum(hot2, axis=0, keepdims=True)
    seen = seen_ref[...]
    rank1 = jnp.sum(hot1 * (seen + before1), axis=1, keepdims=True)
    rank2 = jnp.sum(hot2 * (seen + cnt1 + before2), axis=1, keepdims=True)
    seen_ref[...] = seen + cnt1 + cnt2

    out = jnp.where(lane == 0, i1, 0.0)
    for k, col in enumerate((i2, p1, p2, rank1, rank2), start=1):
        out = jnp.where(lane == k, col, out)
    return out


def _mixer_tail(x, mix_out, mod_ref, gffn_ref, wr_ref, br_ref, xnew_ref, h2_ref, route_ref, cnt_ref, seen_ref):
    @pl.when(pl.program_id(0) == 0)
    def _():
        seen_ref[...] = jnp.zeros_like(seen_ref)

    x_new = x + mod_ref[2:3, :] * mix_out
    xnew_ref[...] = x_new
    h2 = _norm_mod(x_new, gffn_ref[...], mod_ref[3:4, :], mod_ref[4:5, :])
    h2_ref[...] = h2
    route_ref[...] = _route(h2.astype(BF16), wr_ref, br_ref, seen_ref)
    cnt_ref[...] = seen_ref[...].astype(jnp.int32)


ADA_TN = 1024


def _ada_kernel(cond_ref, w_ref, b_ref, o_ref):
    c = cond_ref[...]
    s = c * jax.nn.sigmoid(c)
    o_ref[...] = jnp.dot(s.astype(BF16), w_ref[...].astype(BF16), preferred_element_type=F32) + b_ref[...]


def _ada_call(cond, w_ada, b_ada):
    nj = (N_MOD * D_MODEL) // ADA_TN
    return pl.pallas_call(
        _ada_kernel,
        out_shape=jax.ShapeDtypeStruct((DEPTH, COND_ROWS, N_MOD * D_MODEL), F32),
        grid=(DEPTH, nj),
        in_specs=[
            pl.BlockSpec((COND_ROWS, D_MODEL), lambda l, j: (0, 0)),
            pl.BlockSpec((None, D_MODEL, ADA_TN), lambda l, j: (l, 0, j)),
            pl.BlockSpec((None, 1, ADA_TN), lambda l, j: (l, 0, j)),
        ],
        out_specs=pl.BlockSpec((None, COND_ROWS, ADA_TN), lambda l, j: (l, 0, j)),
        compiler_params=_params(40, 2),
        name="ada_mod",
    )(cond, w_ada, b_ada.reshape(DEPTH, 1, N_MOD * D_MODEL))


def _pool_band():
    col = np.arange(POOL_K)
    rel = np.where(col < TM, col, np.where(col < TM + POOL_HALO, col - TM - POOL_HALO, col - POOL_HALO))
    ok = col < TM + 2 * POOL_HALO
    row = np.arange(TM)[:, None]
    mats = []
    for w in POOL_WINDOWS:
        mats.append((rel[None, :] >= row - w // 2) & (rel[None, :] < row + (w - w // 2)) & ok[None, :])
    return np.stack(mats).astype(np.float32)


def _pool_kernel(*refs, split_streams):
    if split_streams:
        xc_ref, xl_ref, *refs = refs
    else:
        x_ref, *refs = refs
    (xp_ref, xn_ref, mod_ref, gmix_ref, gffn_ref, band_ref, wpool_ref, pscale_ref, wr_ref, br_ref,
     xnew_ref, h2_ref, route_ref, cnt_ref, seen_ref) = refs
    i = pl.program_id(0)
    lat = i >= CTX_TILES
    tile_pos = jnp.where(lat, jnp.maximum(i - CTX_TILES, 0) % LAT_TILES_PER_SEQ, 0)
    seq_len = jnp.where(lat, DEC_SEQ, SEQ)
    has_prev = tile_pos > 0
    has_next = (tile_pos + 1) * TM < seq_len

    g = gmix_ref[...]
    shift, scale = mod_ref[0:1, :], mod_ref[1:2, :]
    x = jnp.where(lat, xl_ref[...], xc_ref[...]) if split_streams else x_ref[...]
    h = _norm_mod(x, g, shift, scale)
    hp = jnp.where(has_prev, _norm_mod(xp_ref[...], g, shift, scale), 0.0)
    hn = jnp.where(has_next, _norm_mod(xn_ref[...], g, shift, scale), 0.0)
    hall = jnp.concatenate(
        [h, hp, hn, jnp.zeros((POOL_K - TM - 2 * POOL_HALO, D_MODEL), F32)], axis=0)
    hi = hall.astype(BF16)
    lo = (hall - hi.astype(F32)).astype(BF16)

    pos = tile_pos * TM + lax.broadcasted_iota(jnp.int32, (TM, 1), 0)
    outs = []
    for gi, w in enumerate(POOL_WINDOWS):
        sl = slice(gi * POOL_CH, (gi + 1) * POOL_CH)
        band = band_ref[gi]
        wsum = (jnp.dot(band, hi[:, sl], preferred_element_type=F32)
                + jnp.dot(band, lo[:, sl], preferred_element_type=F32))
        cnt = jnp.minimum(pos + (w - w // 2), seq_len) - jnp.maximum(pos - w // 2, 0)
        diff = wsum / cnt.astype(F32) - h[:, sl]
        outs.append(jnp.dot(diff.astype(BF16), wpool_ref[gi], preferred_element_type=F32))
    mix = jnp.concatenate(outs, axis=1) * pscale_ref[...]
    _mixer_tail(x, mix, mod_ref, gffn_ref, wr_ref, br_ref, xnew_ref, h2_ref, route_ref, cnt_ref, seen_ref)


def _tail_out_shapes():
    return (jax.ShapeDtypeStruct((N_TOK, D_MODEL), F32),
            jax.ShapeDtypeStruct((N_TOK, D_MODEL), F32),
            jax.ShapeDtypeStruct((N_TOK, ROUTE_LANES), F32),
            jax.ShapeDtypeStruct((1, ROUTE_LANES), jnp.int32))


def _tail_out_specs():
    return (pl.BlockSpec((TM, D_MODEL), lambda i: (i, 0)),
            pl.BlockSpec((TM, D_MODEL), lambda i: (i, 0)),
            pl.BlockSpec((TM, ROUTE_LANES), lambda i: (i, 0)),
            pl.BlockSpec((1, ROUTE_LANES), lambda i: (0, 0)))


def _tail_scratch():
    return [pltpu.VMEM((1, ROUTE_LANES), F32)]


def _row_spec(width):
    return pl.BlockSpec((1, width), lambda i: (0, 0))


def _mod_spec(layer):
    return pl.BlockSpec((None, None, N_MOD, D_MODEL), lambda i: (layer, _mod_row(i), 0, 0))


def _pool_call(layer, xs, mod, g_mix, g_ffn, band, wpool, pscale, wr, br):
    halo_blocks = TM // POOL_HALO
    split_streams = len(xs) == 2
    tile0 = CTX_TILES if split_streams else 0
    last_halo = xs[-1].shape[0] // POOL_HALO - 1

    def prev_map(i):
        return (jnp.clip((i - tile0) * halo_blocks - 1, 0, last_halo), 0)

    def next_map(i):
        return (jnp.clip((i - tile0 + 1) * halo_blocks, 0, last_halo), 0)

    if split_streams:
        main_specs = [pl.BlockSpec((TM, D_MODEL), lambda i: (jnp.minimum(i, CTX_TILES - 1), 0)),
                      pl.BlockSpec((TM, D_MODEL), lambda i: (jnp.maximum(i - CTX_TILES, 0), 0))]
    else:
        main_specs = [pl.BlockSpec((TM, D_MODEL), lambda i: (i, 0))]
    return pl.pallas_call(
        functools.partial(_pool_kernel, split_streams=split_streams),
        out_shape=_tail_out_shapes(),
        grid=(N_TILES,),
        in_specs=main_specs + [
            pl.BlockSpec((POOL_HALO, D_MODEL), prev_map),
            pl.BlockSpec((POOL_HALO, D_MODEL), next_map),
            _mod_spec(layer),
            _row_spec(D_MODEL),
            _row_spec(D_MODEL),
            pl.BlockSpec((N_POOL_GROUPS, TM, POOL_K), lambda i: (0, 0, 0)),
            pl.BlockSpec((N_POOL_GROUPS, POOL_CH, POOL_CH), lambda i: (0, 0, 0)),
            _row_spec(D_MODEL),
            pl.BlockSpec((D_MODEL, ROUTE_LANES), lambda i: (0, 0)),
            _row_spec(ROUTE_LANES),
        ],
        out_specs=_tail_out_specs(),
        scratch_shapes=_tail_scratch(),
        compiler_params=_params(48),
        name=f"pool_layer{layer}",
    )(*xs, xs[-1], xs[-1], mod, g_mix, g_ffn, band, wpool, pscale, wr, br)


def _qkv_kernel(x_ref, mod_ref, gmix_ref, wdq_ref, gq_ref, wdkv_ref, gkv_ref, wkr_ref, wuqn_ref, wuqr_ref,
                tabq_ref, tabk_ref, _ckv_leaf_in, _kr_leaf_in, qn_ref, qr_ref, ckv_ref, kr_ref, ckv_leaf_ref,
                kr_leaf_ref):
    i = pl.program_id(0)
    h = _norm_mod(x_ref[...], gmix_ref[...], mod_ref[0:1, :], mod_ref[1:2, :]).astype(BF16)

    def rms(v, g):
        return v * lax.rsqrt(jnp.mean(v * v, axis=-1, keepdims=True) + EPS) * g

    cq = rms(jnp.dot(h, wdq_ref[...], preferred_element_type=F32), gq_ref[...]).astype(BF16)
    ckv = rms(jnp.dot(h, wdkv_ref[...], preferred_element_type=F32), gkv_ref[...])
    ckv_ref[...] = ckv
    qn_ref[...] = jnp.dot(cq, wuqn_ref[...], preferred_element_type=F32).astype(BF16)
    qr = jnp.dot(cq, wuqr_ref[...], preferred_element_type=F32)
    kr = jnp.dot(h, wkr_ref[...], preferred_element_type=F32)
    lane = lax.broadcasted_iota(jnp.int32, kr.shape, 1)

    @pl.when(i < CTX_TILES)
    def _():
        qr_ref[...] = qr.astype(BF16)
        kr_ref[...] = jnp.where(lane < ROPE_DIM, kr, 0.0)
        ckv_leaf_ref[...] = ckv
        kr_leaf_ref[...] = kr[:, :ROPE_DIM]

    @pl.when(i >= CTX_TILES)
    def _():
        tq = qr * tabq_ref[...]
        qr_ref[...] = (tq + pltpu.roll(tq, N_HEADS * LANES - ROPE_DIM, 1)).astype(BF16)
        tk = kr * tabk_ref[...]
        kr_ref[...] = jnp.where(lane < ROPE_DIM, tk + pltpu.roll(tk, ROPE_DIM, 1), 0.0)


def _qkv_call(layer, x, mod, g_mix, wdq, gq, wdkv, gkv, wkr, wuqn, wuqr, tabq, tabk, ckv_leaf, kr_leaf):
    m = layer // N_MIXERS

    def leaf_spec(width):
        return pl.BlockSpec((None, None, SEQ, width), lambda i: (jnp.minimum(i, CTX_TILES - 1), m, 0, 0))

    hd = N_HEADS * LANES

    def lat_tile(i):
        return jnp.maximum(i - CTX_TILES, 0) % LAT_TILES_PER_SEQ

    def full(a):
        return pl.BlockSpec(a.shape, lambda i: (0,) * a.ndim)

    return pl.pallas_call(
        _qkv_kernel,
        out_shape=(jax.ShapeDtypeStruct((N_TOK, hd), BF16),
                   jax.ShapeDtypeStruct((N_TOK, hd), BF16),
                   jax.ShapeDtypeStruct((N_TOK, KV_RANK), F32),
                   jax.ShapeDtypeStruct((N_TOK, LANES), F32),
                   jax.ShapeDtypeStruct(ckv_leaf.shape, F32),
                   jax.ShapeDtypeStruct(kr_leaf.shape, F32)),
        grid=(N_TILES,),
        in_specs=[
            pl.BlockSpec((TM, D_MODEL), lambda i: (i, 0)),
            _mod_spec(layer),
            _row_spec(D_MODEL),
            full(wdq), _row_spec(Q_RANK), full(wdkv), _row_spec(KV_RANK), full(wkr), full(wuqn), full(wuqr),
            pl.BlockSpec((TM, hd), lambda i: (lat_tile(i), 0)),
            pl.BlockSpec((TM, LANES), lambda i: (lat_tile(i), 0)),
            pl.BlockSpec(memory_space=pl.ANY),
            pl.BlockSpec(memory_space=pl.ANY),
        ],
        out_specs=(pl.BlockSpec((TM, hd), lambda i: (i, 0)),
                   pl.BlockSpec((TM, hd), lambda i: (i, 0)),
                   pl.BlockSpec((TM, KV_RANK), lambda i: (i, 0)),
                   pl.BlockSpec((TM, LANES), lambda i: (i, 0)),
                   leaf_spec(KV_RANK),
                   leaf_spec(ROPE_DIM)),
        input_output_aliases={12: 4, 13: 5},
        compiler_params=_params(48),
        name=f"mla_qkv{layer}",
    )(x, mod, g_mix, wdq, gq, wdkv, gkv, wkr, wuqn, wuqr, tabq, tabk, ckv_leaf, kr_leaf)


HEAD_K = 2 * LANES


def _attn_kernel(*refs, n_new, n_cache):
    if n_cache:
        (qn_ref, qr_ref, ckv_ref, kr_ref, cckv_ref, ckr_ref, wuk_ref, wuv_ref, o_ref, kcat, vs) = refs
    else:
        (qn_ref, qr_ref, ckv_ref, kr_ref, wuk_ref, wuv_ref, o_ref, kcat, vs) = refs
        cckv_ref = ckr_ref = None

    @pl.when(pl.program_id(1) == 0)
    def _():
        def expand(src_ref, kr_src_ref, n_rows, base):
            for c in range(n_rows // TM):
                rows = slice(c * TM, (c + 1) * TM)
                dst = slice(base + c * TM, base + (c + 1) * TM)
                lat = src_ref[rows, :].astype(BF16)
                kn = jnp.dot(lat, wuk_ref[...], preferred_element_type=F32).astype(BF16)
                vs[dst, :] = jnp.dot(lat, wuv_ref[...], preferred_element_type=F32).astype(BF16)
                krb = kr_src_ref[rows, :].astype(BF16)
                for hh in range(N_HEADS):
                    kcat[dst, hh * HEAD_K:hh * HEAD_K + LANES] = kn[:, hh * LANES:(hh + 1) * LANES]
                    kcat[dst, hh * HEAD_K + LANES:(hh + 1) * HEAD_K] = krb

        expand(ckv_ref, kr_ref, n_new, 0)
        if n_cache:
            expand(cckv_ref, ckr_ref, n_cache, n_new)

    for hh in range(N_HEADS):
        hs = slice(hh * LANES, (hh + 1) * LANES)
        q = jnp.concatenate([qn_ref[:, hs], qr_ref[:, hs]], axis=1)
        s = lax.dot_general(q, kcat[:, hh * HEAD_K:(hh + 1) * HEAD_K], (((1,), (1,)), ((), ())),
                            preferred_element_type=F32) * ATTN_SCALE
        p = jnp.exp(s - jnp.max(s, axis=1, keepdims=True))
        den = jnp.sum(p, axis=1, keepdims=True)
        o = jnp.dot(p.astype(BF16), vs[:, hs], preferred_element_type=F32) / den
        o_ref[:, hs] = o.astype(BF16)


def _attn_call(layer, qn, qr, ckv, kr, wuk, wuv, cache=None):
    hd = N_HEADS * LANES
    if cache is None:
        n_b, n_q, n_new, n_cache, tile0 = BATCH, SEQ // TM, SEQ, 0, 0
    else:
        n_b, n_q, n_new, n_cache, tile0 = DEC_BATCH, DEC_SEQ // TM, DEC_SEQ, PAST_LEN, CTX_TILES
    seq_blk0 = tile0 * TM // n_new
    m = layer // N_MIXERS

    def q_map(b, j):
        return (tile0 + b * n_q + j, 0)

    in_specs = [
        pl.BlockSpec((TM, hd), q_map),
        pl.BlockSpec((TM, hd), q_map),
        pl.BlockSpec((n_new, KV_RANK), lambda b, j: (seq_blk0 + b, 0)),
        pl.BlockSpec((n_new, LANES), lambda b, j: (seq_blk0 + b, 0)),
    ]
    args = [qn, qr, ckv, kr]
    if cache is not None:
        in_specs += [
            pl.BlockSpec((None, None, n_cache, KV_RANK), lambda b, j: (b, m, 0, 0)),
            pl.BlockSpec((None, None, n_cache, LANES), lambda b, j: (b, m, 0, 0)),
        ]
        args += list(cache)
    in_specs += [
        pl.BlockSpec((KV_RANK, hd), lambda b, j: (0, 0)),
        pl.BlockSpec((KV_RANK, hd), lambda b, j: (0, 0)),
    ]
    args += [wuk, wuv]
    n_keys = n_new + n_cache
    return pl.pallas_call(
        functools.partial(_attn_kernel, n_new=n_new, n_cache=n_cache),
        out_shape=jax.ShapeDtypeStruct((n_b * n_q * TM, hd), BF16),
        grid=(n_b, n_q),
        in_specs=in_specs,
        out_specs=pl.BlockSpec((TM, hd), lambda b, j: (b * n_q + j, 0)),
        scratch_shapes=[pltpu.VMEM((n_keys, N_HEADS * HEAD_K), BF16), pltpu.VMEM((n_keys, hd), BF16)],
        compiler_params=_params(56, 2),
        name=f"mla_attn{layer}_{'lat' if cache is not None else 'ctx'}",
    )(*args)


def _attn_out_kernel(octx_ref, olat_ref, wo_ref, x_ref, mod_ref, gffn_ref, wr_ref, br_ref, xnew_ref, h2_ref,
                     route_ref, cnt_ref, seen_ref):
    o = jnp.where(pl.program_id(0) >= CTX_TILES, olat_ref[...], octx_ref[...])
    mix = jnp.dot(o, wo_ref[...], preferred_element_type=F32)
    _mixer_tail(x_ref[...], mix, mod_ref, gffn_ref, wr_ref, br_ref, xnew_ref, h2_ref, route_ref, cnt_ref,
                seen_ref)


def _attn_out_call(layer, o_ctx, o_lat, wo, x, mod, g_ffn, wr, br):
    hd = N_HEADS * V_DIM
    return pl.pallas_call(
        _attn_out_kernel,
        out_shape=_tail_out_shapes(),
        grid=(N_TILES,),
        in_specs=[
            pl.BlockSpec((TM, hd), lambda i: (jnp.minimum(i, CTX_TILES - 1), 0)),
            pl.BlockSpec((TM, hd), lambda i: (jnp.maximum(i - CTX_TILES, 0), 0)),
            pl.BlockSpec((hd, D_MODEL), lambda i: (0, 0)),
            pl.BlockSpec((TM, D_MODEL), lambda i: (i, 0)),
            _mod_spec(layer),
            _row_spec(D_MODEL),
            pl.BlockSpec((D_MODEL, ROUTE_LANES), lambda i: (0, 0)),
            _row_spec(ROUTE_LANES),
        ],
        out_specs=_tail_out_specs(),
        scratch_shapes=_tail_scratch(),
        compiler_params=_params(48),
        name=f"mla_out{layer}",
    )(o_ctx, o_lat, wo, x, mod, g_ffn, wr, br)


def _plan_kernel(cnt_ref, route_ref, pos_ref, wt_ref, we_ref, wlo_ref, whi_ref, nw_ref, start_ref):
    @pl.when(pl.program_id(0) == 0)
    def _():
        lane = lax.broadcasted_iota(jnp.int32, (1, ROUTE_LANES), 1)
        shift = TM.bit_length() - 1

        def per_expert(e, carry):
            n_work, lo, last_e, starts = carry
            cnt = cnt_ref[0, e]
            hi = lo + cnt
            first_tile = lax.shift_right_logical(lo, shift)
            end_tile = jnp.where(cnt > 0, lax.shift_right_logical(hi + (TM - 1), shift), first_tile)

            def per_tile(t, w):
                wt_ref[w] = t
                we_ref[w] = e
                wlo_ref[w] = lo
                whi_ref[w] = hi
                return w + 1

            n_work = lax.fori_loop(first_tile, end_tile, per_tile, n_work)
            starts = jnp.where(lane == e, lo.astype(F32), starts)
            return n_work, hi, jnp.where(cnt > 0, e, last_e), starts

        n_work, _, last_e, starts = lax.fori_loop(
            0, N_EXPERTS, per_expert,
            (jnp.int32(0), jnp.int32(0), jnp.int32(0), jnp.zeros((1, ROUTE_LANES), F32)))
        nw_ref[0] = n_work
        start_ref[...] = starts

        def fill(w, carry):
            wt_ref[w] = SORT_TILES - 1
            we_ref[w] = last_e
            wlo_ref[w] = 0
            whi_ref[w] = 0
            return carry

        lax.fori_loop(n_work, MAX_WORK, fill, 0)

    r = route_ref[...]
    lane = lax.broadcasted_iota(jnp.int32, r.shape, 1).astype(F32)
    starts = start_ref[...]
    pos1 = jnp.sum(jnp.where(lane == r[:, 0:1], starts, 0.0), axis=1, keepdims=True) + r[:, 4:5]
    pos2 = jnp.sum(jnp.where(lane == r[:, 1:2], starts, 0.0), axis=1, keepdims=True) + r[:, 5:6]
    pos = jnp.where(lane == 0.0, pos1, jnp.where(lane == 1.0, pos2, 0.0))
    pos_ref[...] = pos.T[0:SUBLANES, :].astype(jnp.int32)


PLAN_ROWS = 4 * TM


def _plan_call(layer, counts, route):
    smem = pl.BlockSpec(memory_space=pltpu.SMEM)
    work = jax.ShapeDtypeStruct((MAX_WORK,), jnp.int32)
    return pl.pallas_call(
        _plan_kernel,
        out_shape=(jax.ShapeDtypeStruct((SUBLANES, N_TOK), jnp.int32), work, work, work, work,
                   jax.ShapeDtypeStruct((1,), jnp.int32)),
        grid=(N_TOK // PLAN_ROWS,),
        in_specs=[smem, pl.BlockSpec((PLAN_ROWS, ROUTE_LANES), lambda i: (i, 0))],
        out_specs=(pl.BlockSpec((SUBLANES, PLAN_ROWS), lambda i: (0, i)), smem, smem, smem, smem, smem),
        scratch_shapes=[pltpu.VMEM((1, ROUTE_LANES), F32)],
        compiler_params=_params(16),
        name=f"moe_plan{layer}",
    )(counts, route)


INVERT_UNROLL = 8


def _invert_kernel(pos_ref, tok_ref):
    for k in range(2):
        def body(j, carry, k=k):
            for u in range(INVERT_UNROLL):
                t = j * INVERT_UNROLL + u
                tok_ref[pos_ref[k * N_TOK + t]] = t
            return carry

        lax.fori_loop(0, N_TOK // INVERT_UNROLL, body, 0)


def _invert_call(layer, pos):
    smem = pl.BlockSpec(memory_space=pltpu.SMEM)
    return pl.pallas_call(
        _invert_kernel,
        out_shape=jax.ShapeDtypeStruct((N_SLOTS,), jnp.int32),
        in_specs=[smem],
        out_specs=smem,
        name=f"moe_invert{layer}",
    )(pos)


def _gmm_kernel(wt_ref, we_ref, wlo_ref, whi_ref, nw_ref, tok_ref, h2_hbm, wg_hbm, wu_hbm, wd_hbm, ys_ref,
                xbuf, wg_f, wu_f, wd_f, wg_s, wu_s, wd_s, sem_x, sem, nexp_ref, *, layer):
    w = pl.program_id(0)
    n_work = nw_ref[0]
    prev = jnp.maximum(w - 1, 0)
    tile = wt_ref[w]
    new_expert = (w == 0) | (we_ref[w] != we_ref[prev])
    new_tile = (w == 0) | (tile != wt_ref[prev])

    def weight_copies(expert, slot):
        return [pltpu.make_async_copy(hbm.at[layer, expert], buf.at[slot], sem.at[slot, k])
                for k, (hbm, buf) in enumerate(((wg_hbm, wg_f), (wu_hbm, wu_f), (wd_hbm, wd_f)))]

    def gather(t):
        slot = t % 2
        for r in range(TM):
            pltpu.make_async_copy(h2_hbm.at[pl.ds(tok_ref[t * TM + r], 1)], xbuf.at[slot, pl.ds(r, 1)],
                                  sem_x.at[slot]).start()

    @pl.when(w == 0)
    def _():
        nexp_ref[0] = 0
        for cp in weight_copies(we_ref[0], 0):
            cp.start()
        gather(tile)

    @pl.when((w < n_work) & new_tile & (tile + 1 < SORT_TILES))
    def _():
        gather(tile + 1)

    @pl.when((w < n_work) & new_expert)
    def _():
        slot = nexp_ref[0] % 2
        nexp_ref[0] += 1
        nxt = lax.while_loop(
            lambda k: (k < n_work) & (we_ref[jnp.minimum(k, MAX_WORK - 1)] == we_ref[w]),
            lambda k: k + 1, w + 1)

        @pl.when(nxt < n_work)
        def _():
            for cp in weight_copies(we_ref[jnp.minimum(nxt, MAX_WORK - 1)], 1 - slot):
                cp.start()

        for cp in weight_copies(we_ref[w], slot):
            cp.wait()
        wg_s[...] = wg_f[slot].astype(BF16)
        wu_s[...] = wu_f[slot].astype(BF16)
        wd_s[...] = wd_f[slot].astype(BF16)

    @pl.when((w < n_work) & new_tile)
    def _():
        slot = tile % 2
        pltpu.make_async_copy(h2_hbm.at[pl.ds(0, TM)], xbuf.at[slot], sem_x.at[slot]).wait()

    @pl.when(w < n_work)
    def _():
        x = xbuf[tile % 2].astype(BF16)
        gate = jnp.dot(x, wg_s[...], preferred_element_type=F32)
        up = jnp.dot(x, wu_s[...], preferred_element_type=F32)
        row = tile * TM + lax.broadcasted_iota(jnp.int32, (TM, 1), 0)
        mine = (row >= wlo_ref[w]) & (row < whi_ref[w])
        hid = jnp.where(mine, (gate * jax.nn.sigmoid(gate)) * up, 0.0)
        y = jnp.dot(hid.astype(BF16), wd_s[...], preferred_element_type=F32)

        @pl.when(new_tile)
        def _():
            ys_ref[...] = y

        @pl.when(jnp.logical_not(new_tile))
        def _():
            ys_ref[...] += y


def _gmm_call(layer, h2, work, slot_tok, w_gate, w_up, w_down):
    up_shape, down_shape = (D_MODEL, EXPERT_HIDDEN), (EXPERT_HIDDEN, D_MODEL)
    hbm = pl.BlockSpec(memory_space=pl.ANY)
    grid_spec = pltpu.PrefetchScalarGridSpec(
        num_scalar_prefetch=6,
        grid=(MAX_WORK,),
        in_specs=[hbm, hbm, hbm, hbm],
        out_specs=pl.BlockSpec((TM, D_MODEL), lambda w, wt, we, wlo, whi, nw, tok: (wt[w], 0)),
        scratch_shapes=[
            pltpu.VMEM((2, TM, D_MODEL), F32),
            pltpu.VMEM((2,) + up_shape, F32),
            pltpu.VMEM((2,) + up_shape, F32),
            pltpu.VMEM((2,) + down_shape, F32),
            pltpu.VMEM(up_shape, BF16),
            pltpu.VMEM(up_shape, BF16),
            pltpu.VMEM(down_shape, BF16),
            pltpu.SemaphoreType.DMA((2,)),
            pltpu.SemaphoreType.DMA((2, 3)),
            pltpu.SMEM((1,), jnp.int32),
        ],
    )
    return pl.pallas_call(
        functools.partial(_gmm_kernel, layer=layer),
        out_shape=jax.ShapeDtypeStruct((N_SLOTS, D_MODEL), F32),
        grid_spec=grid_spec,
        compiler_params=_params(48),
        name=f"moe_gmm{layer}",
    )(*work, slot_tok, h2, w_gate, w_up, w_down)


def _combine_kernel(pos_ref, x_ref, route_ref, mod_ref, gfin_ref, ys_hbm, *rest, final):
    if final:
        octx_ref, olat_ref, buf1, buf2, sem1, sem2 = rest
    else:
        o_ref, buf1, buf2, sem1, sem2 = rest
    i = pl.program_id(0)

    def fetch(tile, slot):
        def issue(r, carry):
            t = tile * TM + r
            pltpu.make_async_copy(ys_hbm.at[pl.ds(pos_ref[t], 1)], buf1.at[slot, pl.ds(r, 1)],
                                  sem1.at[slot]).start(priority=0)
            pltpu.make_async_copy(ys_hbm.at[pl.ds(pos_ref[N_TOK + t], 1)], buf2.at[slot, pl.ds(r, 1)],
                                  sem2.at[slot]).start(priority=1)
            return carry

        for r in range(TM):
            issue(r, 0)

    def wait_rows(slot):
        pltpu.make_async_copy(ys_hbm.at[pl.ds(0, TM)], buf1.at[slot], sem1.at[slot]).wait()
        pltpu.make_async_copy(ys_hbm.at[pl.ds(0, TM)], buf2.at[slot], sem2.at[slot]).wait()

    @pl.when(i == 0)
    def _():
        fetch(i, 0)

    slot = i % 2

    @pl.when(i + 1 < N_TILES)
    def _():
        fetch(i + 1, 1 - slot)

    wait_rows(slot)
    r = route_ref[...]
    moe = r[:, 2:3] * buf1[slot] + r[:, 3:4] * buf2[slot]
    x = x_ref[...] + mod_ref[5:6, :] * moe
    if final:
        y = x * lax.rsqrt(jnp.mean(x * x, axis=-1, keepdims=True) + EPS) * gfin_ref[...]

        @pl.when(i < CTX_TILES)
        def _():
            octx_ref[...] = y

        @pl.when(i >= CTX_TILES)
        def _():
            olat_ref[...] = y
    else:
        o_ref[...] = x


def _combine_call(layer, pos, x_new, route, ys, mod, g_final, final):
    def tok_spec(width):
        return pl.BlockSpec((TM, width), lambda i, pos: (i, 0))

    if final:
        out_shape = (jax.ShapeDtypeStruct((N_CTX, D_MODEL), F32), jax.ShapeDtypeStruct((N_LAT, D_MODEL), F32))
        out_specs = (pl.BlockSpec((TM, D_MODEL), lambda i, pos: (jnp.minimum(i, CTX_TILES - 1), 0)),
                     pl.BlockSpec((TM, D_MODEL), lambda i, pos: (jnp.maximum(i - CTX_TILES, 0), 0)))
    else:
        out_shape = jax.ShapeDtypeStruct((N_TOK, D_MODEL), F32)
        out_specs = tok_spec(D_MODEL)

    grid_spec = pltpu.PrefetchScalarGridSpec(
        num_scalar_prefetch=1,
        grid=(N_TILES,),
        in_specs=[
            tok_spec(D_MODEL),
            tok_spec(ROUTE_LANES),
            pl.BlockSpec((None, None, N_MOD, D_MODEL), lambda i, pos: (layer, _mod_row(i), 0, 0)),
            pl.BlockSpec((1, D_MODEL), lambda i, pos: (0, 0)),
            pl.BlockSpec(memory_space=pl.ANY),
        ],
        out_specs=out_specs,
        scratch_shapes=[
            pltpu.VMEM((2, TM, D_MODEL), F32),
            pltpu.VMEM((2, TM, D_MODEL), F32),
            pltpu.SemaphoreType.DMA((2,)),
            pltpu.SemaphoreType.DMA((2,)),
        ],
    )
    return pl.pallas_call(
        functools.partial(_combine_kernel, final=final),
        out_shape=out_shape,
        grid_spec=grid_spec,
        compiler_params=_params(40),
        name=f"moe_combine{layer}",
    )(pos, x_new, route, mod, g_final, ys)


def _rope_tables():
    rows = DEC_SEQ // GRID_W
    row_ids = jnp.repeat(jnp.arange(rows), GRID_W).astype(F32)
    col_ids = jnp.tile(jnp.arange(GRID_W), rows).astype(F32)
    freqs = ROPE_THETA ** (-jnp.arange(ROPE_QUARTER, dtype=F32) / ROPE_QUARTER)
    ar, ac = row_ids[:, None] * freqs[None, :], col_ids[:, None] * freqs[None, :]
    cos = jnp.concatenate([jnp.cos(ar), jnp.cos(ar), jnp.cos(ac), jnp.cos(ac)], axis=1)
    sin = jnp.concatenate([-jnp.sin(ar), jnp.sin(ar), -jnp.sin(ac), jnp.sin(ac)], axis=1)
    return jnp.concatenate([cos, sin], axis=1)


def _swap_partners(w):
    q = ROPE_QUARTER
    return jnp.concatenate([w[..., q:2 * q], w[..., 0:q], w[..., 3 * q:4 * q], w[..., 2 * q:3 * q]], axis=-1)


def kernel(x_prompt, x_sample, cache_ckv, cache_krope, c, c_ctx, g_mix, g_ffn, w_ada, b_ada, w_pool, pool_scale,
           w_dq, g_q, w_uq, w_dkv, g_kv, w_uk, w_uv, w_o, w_router_grp, b_router_grp, w_router_exp,
           b_router_exp, w_gate, w_up, w_down, g_final):
    n_mla = DEPTH // N_MIXERS
    x = (x_prompt.reshape(N_CTX, D_MODEL), x_sample.reshape(N_LAT, D_MODEL))

    cond = jnp.concatenate([c_ctx[None, :], c, jnp.zeros((COND_ROWS - 1 - DEC_BATCH, D_MODEL), F32)], axis=0)
    mod = _ada_call(cond, w_ada, b_ada).reshape(DEPTH, COND_ROWS, N_MOD, D_MODEL)

    wr = jnp.concatenate([w_router_exp, w_router_grp,
                          jnp.zeros((DEPTH, D_MODEL, ROUTE_LANES - N_EXPERTS - N_EXPERT_GROUPS), F32)],
                         axis=2).astype(BF16)
    br = jnp.concatenate([b_router_exp, b_router_grp,
                          jnp.zeros((DEPTH, ROUTE_LANES - N_EXPERTS - N_EXPERT_GROUPS), F32)], axis=1)
    band = jnp.asarray(_pool_band(), dtype=BF16)
    wpool = w_pool.astype(BF16)
    wdq = w_dq.astype(BF16)
    wdkv = w_dkv[:, :, :KV_RANK].astype(BF16)
    w_kr = w_dkv[:, :, KV_RANK:]
    wkr = jnp.concatenate([w_kr, _swap_partners(w_kr)], axis=-1).astype(BF16)
    wuqn = w_uq[..., :NOPE_DIM].reshape(n_mla, Q_RANK, N_HEADS * NOPE_DIM).astype(BF16)
    w_qr = w_uq[..., NOPE_DIM:]
    wuqr = jnp.concatenate([w_qr, _swap_partners(w_qr)], axis=-1).reshape(n_mla, Q_RANK, N_HEADS * LANES)
    wuqr = wuqr.astype(BF16)
    wuk = w_uk.reshape(n_mla, KV_RANK, N_HEADS * NOPE_DIM).astype(BF16)
    wuv = w_uv.reshape(n_mla, KV_RANK, N_HEADS * V_DIM).astype(BF16)
    wo = w_o.astype(BF16)
    tabk = _rope_tables()
    tabq = jnp.tile(tabk, (1, N_HEADS))
    cache_kr = jnp.pad(cache_krope, ((0, 0), (0, 0), (0, 0), (0, LANES - ROPE_DIM)))

    def row(a):
        return a.reshape(1, -1)

    ctx_ckv = jnp.zeros((BATCH, n_mla, SEQ, KV_RANK), F32)
    ctx_krope = jnp.zeros((BATCH, n_mla, SEQ, ROPE_DIM), F32)
    for layer in range(DEPTH):
        if layer % N_MIXERS == 0:
            p = layer // N_MIXERS
            streams = x if isinstance(x, tuple) else (x,)
            x_new, h2, route, counts = _pool_call(layer, streams, mod, row(g_mix[layer]), row(g_ffn[layer]),
                                                  band, wpool[p], row(pool_scale[p]), wr[layer], row(br[layer]))
        else:
            m = layer // N_MIXERS
            qn, qr, ckv, kr, ctx_ckv, ctx_krope = _qkv_call(
                layer, x, mod, row(g_mix[layer]), wdq[m], row(g_q[m]), wdkv[m], row(g_kv[m]), wkr[m], wuqn[m],
                wuqr[m], tabq, tabk, ctx_ckv, ctx_krope)
            o_ctx = _attn_call(layer, qn, qr, ckv, kr, wuk[m], wuv[m])
            o_lat = _attn_call(layer, qn, qr, ckv, kr, wuk[m], wuv[m], cache=(cache_ckv, cache_kr))
            x_new, h2, route, counts = _attn_out_call(layer, o_ctx, o_lat, wo[m], x, mod, row(g_ffn[layer]),
                                                      wr[layer], row(br[layer]))
        pos8, *work = _plan_call(layer, counts, route)
        pos = pos8[:2].reshape(N_SLOTS)
        ys = _gmm_call(layer, h2, work, _invert_call(layer, pos), w_gate, w_up, w_down)
        x = _combine_call(layer, pos, x_new, route, ys, mod, row(g_final), final=(layer == DEPTH - 1))

    y_prompt = x[0].reshape(BATCH, SEQ, D_MODEL)
    y_sample = x[1].reshape(DEC_BATCH, DEC_SEQ, D_MODEL)
    return y_prompt, y_sample, ctx_ckv, ctx_krope
```

```python
import functools

import numpy as np
import jax
import jax.numpy as jnp
from jax import lax
from jax.experimental import pallas as pl
from jax.experimental.pallas import tpu as pltpu

D_MODEL = 2048
BATCH = 32
SEQ = 256
DEPTH = 4
DEC_BATCH = 2
DEC_SEQ = 1024
PAST_LEN = 512
GRID_W = 64
N_MIXERS = 2
POOL_WINDOWS = (2, 4, 8, 16)
N_POOL_GROUPS = 4
POOL_CH = D_MODEL // N_POOL_GROUPS
N_HEADS = 16
Q_RANK = 512
KV_RANK = 512
NOPE_DIM = 128
ROPE_DIM = 64
V_DIM = 128
ROPE_QUARTER = ROPE_DIM // 4
ROPE_THETA = 10000.0
ATTN_SCALE = (NOPE_DIM + ROPE_DIM) ** -0.5
N_EXPERT_GROUPS = 4
EXPERTS_PER_GROUP = 8
N_EXPERTS = N_EXPERT_GROUPS * EXPERTS_PER_GROUP
EXPERT_HIDDEN = 512
N_MOD = 6
EPS = 1e-6

LANES = 128
SUBLANES = 8
TM = 256
N_CTX = BATCH * SEQ
N_LAT = DEC_BATCH * DEC_SEQ
N_TOK = N_CTX + N_LAT
CTX_TILES = N_CTX // TM
LAT_TILES = N_LAT // TM
N_TILES = CTX_TILES + LAT_TILES
LAT_TILES_PER_SEQ = DEC_SEQ // TM
COND_ROWS = SUBLANES
POOL_HALO = max(POOL_WINDOWS) // 2
POOL_K = TM + LANES
ROUTE_LANES = LANES
N_SLOTS = 2 * N_TOK
SORT_TILES = N_SLOTS // TM
MAX_WORK = SORT_TILES + N_EXPERTS - 1
MIB = 1024 * 1024

F32 = jnp.float32
BF16 = jnp.bfloat16


def _mod_row(tile):
    return jnp.where(tile < CTX_TILES, 0, 1 + jnp.maximum(tile - CTX_TILES, 0) // LAT_TILES_PER_SEQ)


def _params(vmem_mib, n_axes=1):
    return pltpu.CompilerParams(
        dimension_semantics=("arbitrary",) * n_axes,
        vmem_limit_bytes=vmem_mib * MIB,
    )


def _norm_mod(x, g, shift, scale):
    y = x * lax.rsqrt(jnp.mean(x * x, axis=-1, keepdims=True) + EPS)
    return (y * g) * (1.0 + scale) + shift


def _route(h_bf, wr_ref, br_ref, seen_ref):
    logits = jnp.dot(h_bf, wr_ref[...], preferred_element_type=F32) + br_ref[...]
    lane_i = lax.broadcasted_iota(jnp.int32, logits.shape, 1)
    lane = lane_i.astype(F32)
    neg = -jnp.inf
    far = float(ROUTE_LANES)
    gmask = (lane_i >= N_EXPERTS) & (lane_i < N_EXPERTS + N_EXPERT_GROUPS)
    gl = jnp.where(gmask, logits, neg)
    gmax = jnp.max(gl, axis=1, keepdims=True)
    grp = jnp.min(jnp.where(gl == gmax, lane, far), axis=1, keepdims=True) - N_EXPERTS
    p_grp = 1.0 / jnp.sum(jnp.exp(gl - gmax), axis=1, keepdims=True)
    lane_grp = lax.shift_right_logical(lane_i, EXPERTS_PER_GROUP.bit_length() - 1).astype(F32)
    emask = lane_grp == grp
    el = jnp.where(emask, logits, neg)
    m1 = jnp.max(el, axis=1, keepdims=True)
    i1 = jnp.min(jnp.where(el == m1, lane, far), axis=1, keepdims=True)
    el2 = jnp.where(lane == i1, neg, el)
    m2 = jnp.max(el2, axis=1, keepdims=True)
    i2 = jnp.min(jnp.where((el2 == m2) & emask & (lane != i1), lane, far), axis=1, keepdims=True)
    e2 = jnp.exp(m2 - m1)
    p1 = p_grp / (1.0 + e2)
    p2 = p1 * e2
    n_rows = logits.shape[0]
    earlier = (lax.broadcasted_iota(jnp.int32, (n_rows, n_rows), 0)
               > lax.broadcasted_iota(jnp.int32, (n_rows, n_rows), 1)).astype(BF16)
    hot1 = (lane == i1).astype(F32)
    hot2 = (lane == i2).astype(F32)
    before1 = jnp.dot(earlier, hot1.astype(BF16), preferred_element_type=F32)
    before2 = jnp.dot(earlier, hot2.astype(BF16), preferred_element_type=F32)
    cnt1 = jnp.sum(hot1, axis=0, keepdims=True)
    cnt2 = jnp.sum(hot2, axis=0, keepdims=True)
    seen = seen_ref[...]
    rank1 = jnp.sum(hot1 * (seen + before1), axis=1, keepdims=True)
    rank2 = jnp.sum(hot2 * (seen + cnt1 + before2), axis=1, keepdims=True)
    seen_ref[...] = seen + cnt1 + cnt2

    out = jnp.where(lane == 0, i1, 0.0)
    for k, col in enumerate((i2, p1, p2, rank1, rank2), start=1):
        out = jnp.where(lane == k, col, out)
    return out


def _mixer_tail(x, mix_out, mod_ref, gffn_ref, wr_ref, br_ref, xnew_ref, h2_ref, route_ref, cnt_ref, seen_ref):
    @pl.when(pl.program_id(0) == 0)
    def _():
        seen_ref[...] = jnp.zeros_like(seen_ref)

    x_new = x + mod_ref[2:3, :] * mix_out
    xnew_ref[...] = x_new
    h2 = _norm_mod(x_new, gffn_ref[...], mod_ref[3:4, :], mod_ref[4:5, :])
    h2_ref[...] = h2
    route_ref[...] = _route(h2.astype(BF16), wr_ref, br_ref, seen_ref)
    cnt_ref[...] = seen_ref[...].astype(jnp.int32)


ADA_TN = 1024


def _ada_kernel(cond_ref, w_ref, b_ref, o_ref):
    c = cond_ref[...]
    s = c * jax.nn.sigmoid(c)
    o_ref[...] = jnp.dot(s.astype(BF16), w_ref[...].astype(BF16), preferred_element_type=F32) + b_ref[...]


def _ada_call(cond, w_ada, b_ada):
    nj = (N_MOD * D_MODEL) // ADA_TN
    return pl.pallas_call(
        _ada_kernel,
        out_shape=jax.ShapeDtypeStruct((DEPTH, COND_ROWS, N_MOD * D_MODEL), F32),
        grid=(DEPTH, nj),
        in_specs=[
            pl.BlockSpec((COND_ROWS, D_MODEL), lambda l, j: (0, 0)),
            pl.BlockSpec((None, D_MODEL, ADA_TN), lambda l, j: (l, 0, j)),
            pl.BlockSpec((None, 1, ADA_TN), lambda l, j: (l, 0, j)),
        ],
        out_specs=pl.BlockSpec((None, COND_ROWS, ADA_TN), lambda l, j: (l, 0, j)),
        compiler_params=_params(40, 2),
        name="ada_mod",
    )(cond, w_ada, b_ada.reshape(DEPTH, 1, N_MOD * D_MODEL))


def _pool_band():
    col = np.arange(POOL_K)
    rel = np.where(col < TM, col, np.where(col < TM + POOL_HALO, col - TM - POOL_HALO, col - POOL_HALO))
    ok = col < TM + 2 * POOL_HALO
    row = np.arange(TM)[:, None]
    mats = []
    for w in POOL_WINDOWS:
        mats.append((rel[None, :] >= row - w // 2) & (rel[None, :] < row + (w - w // 2)) & ok[None, :])
    return np.stack(mats).astype(np.float32)


def _pool_kernel(*refs, split_streams):
    if split_streams:
        xc_ref, xl_ref, *refs = refs
    else:
        x_ref, *refs = refs
    (xp_ref, xn_ref, mod_ref, gmix_ref, gffn_ref, band_ref, wpool_ref, pscale_ref, wr_ref, br_ref,
     xnew_ref, h2_ref, route_ref, cnt_ref, seen_ref) = refs
    i = pl.program_id(0)
    lat = i >= CTX_TILES
    tile_pos = jnp.where(lat, jnp.maximum(i - CTX_TILES, 0) % LAT_TILES_PER_SEQ, 0)
    seq_len = jnp.where(lat, DEC_SEQ, SEQ)
    has_prev = tile_pos > 0
    has_next = (tile_pos + 1) * TM < seq_len

    g = gmix_ref[...]
    shift, scale = mod_ref[0:1, :], mod_ref[1:2, :]
    x = jnp.where(lat, xl_ref[...], xc_ref[...]) if split_streams else x_ref[...]
    h = _norm_mod(x, g, shift, scale)
    hp = jnp.where(has_prev, _norm_mod(xp_ref[...], g, shift, scale), 0.0)
    hn = jnp.where(has_next, _norm_mod(xn_ref[...], g, shift, scale), 0.0)
    hall = jnp.concatenate(
        [h, hp, hn, jnp.zeros((POOL_K - TM - 2 * POOL_HALO, D_MODEL), F32)], axis=0)
    hi = hall.astype(BF16)
    lo = (hall - hi.astype(F32)).astype(BF16)

    pos = tile_pos * TM + lax.broadcasted_iota(jnp.int32, (TM, 1), 0)
    outs = []
    for gi, w in enumerate(POOL_WINDOWS):
        sl = slice(gi * POOL_CH, (gi + 1) * POOL_CH)
        band = band_ref[gi]
        wsum = (jnp.dot(band, hi[:, sl], preferred_element_type=F32)
                + jnp.dot(band, lo[:, sl], preferred_element_type=F32))
        cnt = jnp.minimum(pos + (w - w // 2), seq_len) - jnp.maximum(pos - w // 2, 0)
        diff = wsum / cnt.astype(F32) - h[:, sl]
        outs.append(jnp.dot(diff.astype(BF16), wpool_ref[gi], preferred_element_type=F32))
    mix = jnp.concatenate(outs, axis=1) * pscale_ref[...]
    _mixer_tail(x, mix, mod_ref, gffn_ref, wr_ref, br_ref, xnew_ref, h2_ref, route_ref, cnt_ref, seen_ref)


def _tail_out_shapes():
    return (jax.ShapeDtypeStruct((N_TOK, D_MODEL), F32),
            jax.ShapeDtypeStruct((N_TOK, D_MODEL), F32),
            jax.ShapeDtypeStruct((N_TOK, ROUTE_LANES), F32),
            jax.ShapeDtypeStruct((1, ROUTE_LANES), jnp.int32))


def _tail_out_specs():
    return (pl.BlockSpec((TM, D_MODEL), lambda i: (i, 0)),
            pl.BlockSpec((TM, D_MODEL), lambda i: (i, 0)),
            pl.BlockSpec((TM, ROUTE_LANES), lambda i: (i, 0)),
            pl.BlockSpec((1, ROUTE_LANES), lambda i: (0, 0)))


def _tail_scratch():
    return [pltpu.VMEM((1, ROUTE_LANES), F32)]


def _row_spec(width):
    return pl.BlockSpec((1, width), lambda i: (0, 0))


def _mod_spec(layer):
    return pl.BlockSpec((None, None, N_MOD, D_MODEL), lambda i: (layer, _mod_row(i), 0, 0))


def _pool_call(layer, xs, mod, g_mix, g_ffn, band, wpool, pscale, wr, br):
    halo_blocks = TM // POOL_HALO
    split_streams = len(xs) == 2
    tile0 = CTX_TILES if split_streams else 0
    last_halo = xs[-1].shape[0] // POOL_HALO - 1

    def prev_map(i):
        return (jnp.clip((i - tile0) * halo_blocks - 1, 0, last_halo), 0)

    def next_map(i):
        return (jnp.clip((i - tile0 + 1) * halo_blocks, 0, last_halo), 0)

    if split_streams:
        main_specs = [pl.BlockSpec((TM, D_MODEL), lambda i: (jnp.minimum(i, CTX_TILES - 1), 0)),
                      pl.BlockSpec((TM, D_MODEL), lambda i: (jnp.maximum(i - CTX_TILES, 0), 0))]
    else:
        main_specs = [pl.BlockSpec((TM, D_MODEL), lambda i: (i, 0))]
    return pl.pallas_call(
        functools.partial(_pool_kernel, split_streams=split_streams),
        out_shape=_tail_out_shapes(),
        grid=(N_TILES,),
        in_specs=main_specs + [
            pl.BlockSpec((POOL_HALO, D_MODEL), prev_map),
            pl.BlockSpec((POOL_HALO, D_MODEL), next_map),
            _mod_spec(layer),
            _row_spec(D_MODEL),
            _row_spec(D_MODEL),
            pl.BlockSpec((N_POOL_GROUPS, TM, POOL_K), lambda i: (0, 0, 0)),
            pl.BlockSpec((N_POOL_GROUPS, POOL_CH, POOL_CH), lambda i: (0, 0, 0)),
            _row_spec(D_MODEL),
            pl.BlockSpec((D_MODEL, ROUTE_LANES), lambda i: (0, 0)),
            _row_spec(ROUTE_LANES),
        ],
        out_specs=_tail_out_specs(),
        scratch_shapes=_tail_scratch(),
        compiler_params=_params(48),
        name=f"pool_layer{layer}",
    )(*xs, xs[-1], xs[-1], mod, g_mix, g_ffn, band, wpool, pscale, wr, br)


def _qkv_kernel(x_ref, mod_ref, gmix_ref, wdq_ref, gq_ref, wdkv_ref, gkv_ref, wkr_ref, wuqn_ref, wuqr_ref,
                tabq_ref, tabk_ref, _ckv_leaf_in, _kr_leaf_in, qn_ref, qr_ref, ckv_ref, kr_ref, ckv_leaf_ref,
                kr_leaf_ref):
    i = pl.program_id(0)
    h = _norm_mod(x_ref[...], gmix_ref[...], mod_ref[0:1, :], mod_ref[1:2, :]).astype(BF16)

    def rms(v, g):
        return v * lax.rsqrt(jnp.mean(v * v, axis=-1, keepdims=True) + EPS) * g

    cq = rms(jnp.dot(h, wdq_ref[...], preferred_element_type=F32), gq_ref[...]).astype(BF16)
    ckv = rms(jnp.dot(h, wdkv_ref[...], preferred_element_type=F32), gkv_ref[...])
    ckv_ref[...] = ckv
    qn_ref[...] = (jnp.dot(cq, wuqn_ref[...], preferred_element_type=F32) * ATTN_SCALE).astype(BF16)
    qr = jnp.dot(cq, wuqr_ref[...], preferred_element_type=F32) * ATTN_SCALE
    kr = jnp.dot(h, wkr_ref[...], preferred_element_type=F32)
    lane = lax.broadcasted_iota(jnp.int32, kr.shape, 1)

    @pl.when(i < CTX_TILES)
    def _():
        qr_ref[...] = qr.astype(BF16)
        kr_ref[...] = jnp.where(lane < ROPE_DIM, kr, 0.0)
        ckv_leaf_ref[...] = ckv
        kr_leaf_ref[...] = kr[:, :ROPE_DIM]

    @pl.when(i >= CTX_TILES)
    def _():
        tq = qr * tabq_ref[...]
        qr_ref[...] = (tq + pltpu.roll(tq, N_HEADS * LANES - ROPE_DIM, 1)).astype(BF16)
        tk = kr * tabk_ref[...]
        kr_ref[...] = jnp.where(lane < ROPE_DIM, tk + pltpu.roll(tk, ROPE_DIM, 1), 0.0)


def _qkv_call(layer, x, mod, g_mix, wdq, gq, wdkv, gkv, wkr, wuqn, wuqr, tabq, tabk, ckv_leaf, kr_leaf):
    m = layer // N_MIXERS

    def leaf_spec(width):
        return pl.BlockSpec((None, None, SEQ, width), lambda i: (jnp.minimum(i, CTX_TILES - 1), m, 0, 0))

    hd = N_HEADS * LANES

    def lat_tile(i):
        return jnp.maximum(i - CTX_TILES, 0) % LAT_TILES_PER_SEQ

    def full(a):
        return pl.BlockSpec(a.shape, lambda i: (0,) * a.ndim)

    return pl.pallas_call(
        _qkv_kernel,
        out_shape=(jax.ShapeDtypeStruct((N_TOK, hd), BF16),
                   jax.ShapeDtypeStruct((N_TOK, hd), BF16),
                   jax.ShapeDtypeStruct((N_TOK, KV_RANK), F32),
                   jax.ShapeDtypeStruct((N_TOK, LANES), F32),
                   jax.ShapeDtypeStruct(ckv_leaf.shape, F32),
                   jax.ShapeDtypeStruct(kr_leaf.shape, F32)),
        grid=(N_TILES,),
        in_specs=[
            pl.BlockSpec((TM, D_MODEL), lambda i: (i, 0)),
            _mod_spec(layer),
            _row_spec(D_MODEL),
            full(wdq), _row_spec(Q_RANK), full(wdkv), _row_spec(KV_RANK), full(wkr), full(wuqn), full(wuqr),
            pl.BlockSpec((TM, hd), lambda i: (lat_tile(i), 0)),
            pl.BlockSpec((TM, LANES), lambda i: (lat_tile(i), 0)),
            pl.BlockSpec(memory_space=pl.ANY),
            pl.BlockSpec(memory_space=pl.ANY),
        ],
        out_specs=(pl.BlockSpec((TM, hd), lambda i: (i, 0)),
                   pl.BlockSpec((TM, hd), lambda i: (i, 0)),
                   pl.BlockSpec((TM, KV_RANK), lambda i: (i, 0)),
                   pl.BlockSpec((TM, LANES), lambda i: (i, 0)),
                   leaf_spec(KV_RANK),
                   leaf_spec(ROPE_DIM)),
        input_output_aliases={12: 4, 13: 5},
        compiler_params=_params(48),
        name=f"mla_qkv{layer}",
    )(x, mod, g_mix, wdq, gq, wdkv, gkv, wkr, wuqn, wuqr, tabq, tabk, ckv_leaf, kr_leaf)


HEAD_K = 2 * LANES


def _attn_kernel(*refs, n_new, n_cache):
    if n_cache:
        (qn_ref, qr_ref, ckv_ref, kr_ref, cckv_ref, ckr_ref, wuk_ref, wuv_ref, o_ref, kcat, vs) = refs
    else:
        (qn_ref, qr_ref, ckv_ref, kr_ref, wuk_ref, wuv_ref, o_ref, kcat, vs) = refs
        cckv_ref = ckr_ref = None

    @pl.when(pl.program_id(1) == 0)
    def _():
        def expand(src_ref, kr_src_ref, n_rows, base):
            for c in range(n_rows // TM):
                rows = slice(c * TM, (c + 1) * TM)
                dst = slice(base + c * TM, base + (c + 1) * TM)
                lat = src_ref[rows, :].astype(BF16)
                kn = jnp.dot(lat, wuk_ref[...], preferred_element_type=F32).astype(BF16)
                vs[dst, :] = jnp.dot(lat, wuv_ref[...], preferred_element_type=F32).astype(BF16)
                krb = kr_src_ref[rows, :].astype(BF16)
                for hh in range(N_HEADS):
                    kcat[dst, hh * HEAD_K:hh * HEAD_K + LANES] = kn[:, hh * LANES:(hh + 1) * LANES]
                    kcat[dst, hh * HEAD_K + LANES:(hh + 1) * HEAD_K] = krb

        expand(ckv_ref, kr_ref, n_new, 0)
        if n_cache:
            expand(cckv_ref, ckr_ref, n_cache, n_new)

    for hh in range(N_HEADS):
        hs = slice(hh * LANES, (hh + 1) * LANES)
        q = jnp.concatenate([qn_ref[:, hs], qr_ref[:, hs]], axis=1)
        s = lax.dot_general(q, kcat[:, hh * HEAD_K:(hh + 1) * HEAD_K], (((1,), (1,)), ((), ())),
                            preferred_element_type=F32)
        p = jnp.exp(s - jnp.max(s, axis=1, keepdims=True))
        den = jnp.sum(p, axis=1, keepdims=True)
        o = jnp.dot(p.astype(BF16), vs[:, hs], preferred_element_type=F32) / den
        o_ref[:, hs] = o.astype(BF16)


def _attn_call(layer, qn, qr, ckv, kr, wuk, wuv, cache=None):
    hd = N_HEADS * LANES
    if cache is None:
        n_b, n_q, n_new, n_cache, tile0 = BATCH, SEQ // TM, SEQ, 0, 0
    else:
        n_b, n_q, n_new, n_cache, tile0 = DEC_BATCH, DEC_SEQ // TM, DEC_SEQ, PAST_LEN, CTX_TILES
    seq_blk0 = tile0 * TM // n_new
    m = layer // N_MIXERS

    def q_map(b, j):
        return (tile0 + b * n_q + j, 0)

    in_specs = [
        pl.BlockSpec((TM, hd), q_map),
        pl.BlockSpec((TM, hd), q_map),
        pl.BlockSpec((n_new, KV_RANK), lambda b, j: (seq_blk0 + b, 0)),
        pl.BlockSpec((n_new, LANES), lambda b, j: (seq_blk0 + b, 0)),
    ]
    args = [qn, qr, ckv, kr]
    if cache is not None:
        in_specs += [
            pl.BlockSpec((None, None, n_cache, KV_RANK), lambda b, j: (b, m, 0, 0)),
            pl.BlockSpec((None, None, n_cache, LANES), lambda b, j: (b, m, 0, 0)),
        ]
        args += list(cache)
    in_specs += [
        pl.BlockSpec((KV_RANK, hd), lambda b, j: (0, 0)),
        pl.BlockSpec((KV_RANK, hd), lambda b, j: (0, 0)),
    ]
    args += [wuk, wuv]
    n_keys = n_new + n_cache
    return pl.pallas_call(
        functools.partial(_attn_kernel, n_new=n_new, n_cache=n_cache),
        out_shape=jax.ShapeDtypeStruct((n_b * n_q * TM, hd), BF16),
        grid=(n_b, n_q),
        in_specs=in_specs,
        out_specs=pl.BlockSpec((TM, hd), lambda b, j: (b * n_q + j, 0)),
        scratch_shapes=[pltpu.VMEM((n_keys, N_HEADS * HEAD_K), BF16), pltpu.VMEM((n_keys, hd), BF16)],
        compiler_params=_params(60, 2),
        name=f"mla_attn{layer}_{'lat' if cache is not None else 'ctx'}",
    )(*args)


def _attn_out_kernel(octx_ref, olat_ref, wo_ref, x_ref, mod_ref, gffn_ref, wr_ref, br_ref, xnew_ref, h2_ref,
                     route_ref, cnt_ref, seen_ref):
    o = jnp.where(pl.program_id(0) >= CTX_TILES, olat_ref[...], octx_ref[...])
    mix = jnp.dot(o, wo_ref[...], preferred_element_type=F32)
    _mixer_tail(x_ref[...], mix, mod_ref, gffn_ref, wr_ref, br_ref, xnew_ref, h2_ref, route_ref, cnt_ref,
                seen_ref)


def _attn_out_call(layer, o_ctx, o_lat, wo, x, mod, g_ffn, wr, br):
    hd = N_HEADS * V_DIM
    return pl.pallas_call(
        _attn_out_kernel,
        out_shape=_tail_out_shapes(),
        grid=(N_TILES,),
        in_specs=[
            pl.BlockSpec((TM, hd), lambda i: (jnp.minimum(i, CTX_TILES - 1), 0)),
            pl.BlockSpec((TM, hd), lambda i: (jnp.maximum(i - CTX_TILES, 0), 0)),
            pl.BlockSpec((hd, D_MODEL), lambda i: (0, 0)),
            pl.BlockSpec((TM, D_MODEL), lambda i: (i, 0)),
            _mod_spec(layer),
            _row_spec(D_MODEL),
            pl.BlockSpec((D_MODEL, ROUTE_LANES), lambda i: (0, 0)),
            _row_spec(ROUTE_LANES),
        ],
        out_specs=_tail_out_specs(),
        scratch_shapes=_tail_scratch(),
        compiler_params=_params(48),
        name=f"mla_out{layer}",
    )(o_ctx, o_lat, wo, x, mod, g_ffn, wr, br)


def _plan_kernel(cnt_ref, route_ref, pos_ref, wt_ref, we_ref, wlo_ref, whi_ref, nw_ref, start_ref):
    @pl.when(pl.program_id(0) == 0)
    def _():
        lane = lax.broadcasted_iota(jnp.int32, (1, ROUTE_LANES), 1)
        shift = TM.bit_length() - 1

        def per_expert(e, carry):
            n_work, lo, last_e, starts = carry
            cnt = cnt_ref[0, e]
            hi = lo + cnt
            first_tile = lax.shift_right_logical(lo, shift)
            end_tile = jnp.where(cnt > 0, lax.shift_right_logical(hi + (TM - 1), shift), first_tile)

            def per_tile(t, w):
                wt_ref[w] = t
                we_ref[w] = e
                wlo_ref[w] = lo
                whi_ref[w] = hi
                return w + 1

            n_work = lax.fori_loop(first_tile, end_tile, per_tile, n_work)
            starts = jnp.where(lane == e, lo.astype(F32), starts)
            return n_work, hi, jnp.where(cnt > 0, e, last_e), starts

        n_work, _, last_e, starts = lax.fori_loop(
            0, N_EXPERTS, per_expert,
            (jnp.int32(0), jnp.int32(0), jnp.int32(0), jnp.zeros((1, ROUTE_LANES), F32)))
        nw_ref[0] = n_work
        start_ref[...] = starts

        def fill(w, carry):
            wt_ref[w] = SORT_TILES - 1
            we_ref[w] = last_e
            wlo_ref[w] = 0
            whi_ref[w] = 0
            return carry

        lax.fori_loop(n_work, MAX_WORK, fill, 0)

    r = route_ref[...]
    lane = lax.broadcasted_iota(jnp.int32, r.shape, 1).astype(F32)
    starts = start_ref[...]
    pos1 = jnp.sum(jnp.where(lane == r[:, 0:1], starts, 0.0), axis=1, keepdims=True) + r[:, 4:5]
    pos2 = jnp.sum(jnp.where(lane == r[:, 1:2], starts, 0.0), axis=1, keepdims=True) + r[:, 5:6]
    pos = jnp.where(lane == 0.0, pos1, jnp.where(lane == 1.0, pos2, 0.0))
    pos_ref[...] = pos.T[0:SUBLANES, :].astype(jnp.int32)


PLAN_ROWS = 4 * TM


def _plan_call(layer, counts, route):
    smem = pl.BlockSpec(memory_space=pltpu.SMEM)
    work = jax.ShapeDtypeStruct((MAX_WORK,), jnp.int32)
    return pl.pallas_call(
        _plan_kernel,
        out_shape=(jax.ShapeDtypeStruct((SUBLANES, N_TOK), jnp.int32), work, work, work, work,
                   jax.ShapeDtypeStruct((1,), jnp.int32)),
        grid=(N_TOK // PLAN_ROWS,),
        in_specs=[smem, pl.BlockSpec((PLAN_ROWS, ROUTE_LANES), lambda i: (i, 0))],
        out_specs=(pl.BlockSpec((SUBLANES, PLAN_ROWS), lambda i: (0, i)), smem, smem, smem, smem, smem),
        scratch_shapes=[pltpu.VMEM((1, ROUTE_LANES), F32)],
        compiler_params=_params(16),
        name=f"moe_plan{layer}",
    )(counts, route)


INVERT_UNROLL = 16


def _invert_kernel(pos_ref, tok_ref):
    for k in range(2):
        def body(j, carry, k=k):
            t0 = j * INVERT_UNROLL
            where = [pos_ref[k * N_TOK + t0 + u] for u in range(INVERT_UNROLL)]
            for u in range(INVERT_UNROLL):
                tok_ref[where[u]] = t0 + u
            return carry

        lax.fori_loop(0, N_TOK // INVERT_UNROLL, body, 0)


def _invert_call(layer, pos):
    smem = pl.BlockSpec(memory_space=pltpu.SMEM)
    return pl.pallas_call(
        _invert_kernel,
        out_shape=jax.ShapeDtypeStruct((N_SLOTS,), jnp.int32),
        in_specs=[smem],
        out_specs=smem,
        name=f"moe_invert{layer}",
    )(pos)


def _gmm_kernel(wt_ref, we_ref, wlo_ref, whi_ref, nw_ref, tok_ref, h2_hbm, wg_hbm, wu_hbm, wd_hbm, ys_ref,
                xbuf, wg_f, wu_f, wd_f, wg_s, wu_s, wd_s, sem_x, sem, nexp_ref, *, layer):
    w = pl.program_id(0)
    n_work = nw_ref[0]
    prev = jnp.maximum(w - 1, 0)
    tile = wt_ref[w]
    new_expert = (w == 0) | (we_ref[w] != we_ref[prev])
    new_tile = (w == 0) | (tile != wt_ref[prev])

    def weight_copies(expert, slot):
        return [pltpu.make_async_copy(hbm.at[layer, expert], buf.at[slot], sem.at[slot, k])
                for k, (hbm, buf) in enumerate(((wg_hbm, wg_f), (wu_hbm, wu_f), (wd_hbm, wd_f)))]

    def gather(t):
        slot = t % 2
        for r in range(TM):
            pltpu.make_async_copy(h2_hbm.at[pl.ds(tok_ref[t * TM + r], 1)], xbuf.at[slot, pl.ds(r, 1)],
                                  sem_x.at[slot]).start()

    @pl.when(w == 0)
    def _():
        nexp_ref[0] = 0
        for cp in weight_copies(we_ref[0], 0):
            cp.start()
        gather(tile)

    @pl.when((w < n_work) & new_tile & (tile + 1 < SORT_TILES))
    def _():
        gather(tile + 1)

    @pl.when((w < n_work) & new_expert)
    def _():
        slot = nexp_ref[0] % 2
        nexp_ref[0] += 1
        nxt = lax.while_loop(
            lambda k: (k < n_work) & (we_ref[jnp.minimum(k, MAX_WORK - 1)] == we_ref[w]),
            lambda k: k + 1, w + 1)

        @pl.when(nxt < n_work)
        def _():
            for cp in weight_copies(we_ref[jnp.minimum(nxt, MAX_WORK - 1)], 1 - slot):
                cp.start()

        for cp in weight_copies(we_ref[w], slot):
            cp.wait()
        wg_s[...] = wg_f[slot].astype(BF16)
        wu_s[...] = wu_f[slot].astype(BF16)
        wd_s[...] = wd_f[slot].astype(BF16)

    @pl.when((w < n_work) & new_tile)
    def _():
        slot = tile % 2
        pltpu.make_async_copy(h2_hbm.at[pl.ds(0, TM)], xbuf.at[slot], sem_x.at[slot]).wait()

    @pl.when(w < n_work)
    def _():
        x = xbuf[tile % 2].astype(BF16)
        gate = jnp.dot(x, wg_s[...], preferred_element_type=F32)
        up = jnp.dot(x, wu_s[...], preferred_element_type=F32)
        row = tile * TM + lax.broadcasted_iota(jnp.int32, (TM, 1), 0)
        mine = (row >= wlo_ref[w]) & (row < whi_ref[w])
        hid = jnp.where(mine, (gate * jax.nn.sigmoid(gate)) * up, 0.0)
        y = jnp.dot(hid.astype(BF16), wd_s[...], preferred_element_type=F32)

        @pl.when(new_tile)
        def _():
            ys_ref[...] = y

        @pl.when(jnp.logical_not(new_tile))
        def _():
            ys_ref[...] += y


def _gmm_call(layer, h2, work, slot_tok, w_gate, w_up, w_down):
    up_shape, down_shape = (D_MODEL, EXPERT_HIDDEN), (EXPERT_HIDDEN, D_MODEL)
    hbm = pl.BlockSpec(memory_space=pl.ANY)
    grid_spec = pltpu.PrefetchScalarGridSpec(
        num_scalar_prefetch=6,
        grid=(MAX_WORK,),
        in_specs=[hbm, hbm, hbm, hbm],
        out_specs=pl.BlockSpec((TM, D_MODEL), lambda w, wt, we, wlo, whi, nw, tok: (wt[w], 0)),
        scratch_shapes=[
            pltpu.VMEM((2, TM, D_MODEL), F32),
            pltpu.VMEM((2,) + up_shape, F32),
            pltpu.VMEM((2,) + up_shape, F32),
            pltpu.VMEM((2,) + down_shape, F32),
            pltpu.VMEM(up_shape, BF16),
            pltpu.VMEM(up_shape, BF16),
            pltpu.VMEM(down_shape, BF16),
            pltpu.SemaphoreType.DMA((2,)),
            pltpu.SemaphoreType.DMA((2, 3)),
            pltpu.SMEM((1,), jnp.int32),
        ],
    )
    return pl.pallas_call(
        functools.partial(_gmm_kernel, layer=layer),
        out_shape=jax.ShapeDtypeStruct((N_SLOTS, D_MODEL), F32),
        grid_spec=grid_spec,
        compiler_params=_params(48),
        name=f"moe_gmm{layer}",
    )(*work, slot_tok, h2, w_gate, w_up, w_down)


def _combine_kernel(pos_ref, x_ref, route_ref, mod_ref, gfin_ref, ys_hbm, *rest, final):
    if final:
        octx_ref, olat_ref, buf1, buf2, sem1, sem2 = rest
    else:
        o_ref, buf1, buf2, sem1, sem2 = rest
    i = pl.program_id(0)

    def fetch(tile, slot):
        def issue(r, carry):
            t = tile * TM + r
            pltpu.make_async_copy(ys_hbm.at[pl.ds(pos_ref[t], 1)], buf1.at[slot, pl.ds(r, 1)],
                                  sem1.at[slot]).start(priority=0)
            pltpu.make_async_copy(ys_hbm.at[pl.ds(pos_ref[N_TOK + t], 1)], buf2.at[slot, pl.ds(r, 1)],
                                  sem2.at[slot]).start(priority=1)
            return carry

        for r in range(TM):
            issue(r, 0)

    def wait_rows(slot):
        pltpu.make_async_copy(ys_hbm.at[pl.ds(0, TM)], buf1.at[slot], sem1.at[slot]).wait()
        pltpu.make_async_copy(ys_hbm.at[pl.ds(0, TM)], buf2.at[slot], sem2.at[slot]).wait()

    @pl.when(i == 0)
    def _():
        fetch(i, 0)

    slot = i % 2

    @pl.when(i + 1 < N_TILES)
    def _():
        fetch(i + 1, 1 - slot)

    wait_rows(slot)
    r = route_ref[...]
    moe = r[:, 2:3] * buf1[slot] + r[:, 3:4] * buf2[slot]
    x = x_ref[...] + mod_ref[5:6, :] * moe
    if final:
        y = x * lax.rsqrt(jnp.mean(x * x, axis=-1, keepdims=True) + EPS) * gfin_ref[...]

        @pl.when(i < CTX_TILES)
        def _():
            octx_ref[...] = y

        @pl.when(i >= CTX_TILES)
        def _():
            olat_ref[...] = y
    else:
        o_ref[...] = x


def _combine_call(layer, pos, x_new, route, ys, mod, g_final, final):
    def tok_spec(width):
        return pl.BlockSpec((TM, width), lambda i, pos: (i, 0))

    if final:
        out_shape = (jax.ShapeDtypeStruct((N_CTX, D_MODEL), F32), jax.ShapeDtypeStruct((N_LAT, D_MODEL), F32))
        out_specs = (pl.BlockSpec((TM, D_MODEL), lambda i, pos: (jnp.minimum(i, CTX_TILES - 1), 0)),
                     pl.BlockSpec((TM, D_MODEL), lambda i, pos: (jnp.maximum(i - CTX_TILES, 0), 0)))
    else:
        out_shape = jax.ShapeDtypeStruct((N_TOK, D_MODEL), F32)
        out_specs = tok_spec(D_MODEL)

    grid_spec = pltpu.PrefetchScalarGridSpec(
        num_scalar_prefetch=1,
        grid=(N_TILES,),
        in_specs=[
            tok_spec(D_MODEL),
            tok_spec(ROUTE_LANES),
            pl.BlockSpec((None, None, N_MOD, D_MODEL), lambda i, pos: (layer, _mod_row(i), 0, 0)),
            pl.BlockSpec((1, D_MODEL), lambda i, pos: (0, 0)),
            pl.BlockSpec(memory_space=pl.ANY),
        ],
        out_specs=out_specs,
        scratch_shapes=[
            pltpu.VMEM((2, TM, D_MODEL), F32),
            pltpu.VMEM((2, TM, D_MODEL), F32),
            pltpu.SemaphoreType.DMA((2,)),
            pltpu.SemaphoreType.DMA((2,)),
        ],
    )
    return pl.pallas_call(
        functools.partial(_combine_kernel, final=final),
        out_shape=out_shape,
        grid_spec=grid_spec,
        compiler_params=_params(40),
        name=f"moe_combine{layer}",
    )(pos, x_new, route, mod, g_final, ys)


def _rope_tables():
    rows = DEC_SEQ // GRID_W
    row_ids = np.repeat(np.arange(rows), GRID_W).astype(np.float32)
    col_ids = np.tile(np.arange(GRID_W), rows).astype(np.float32)
    freqs = (np.float32(ROPE_THETA) ** (-np.arange(ROPE_QUARTER, dtype=np.float32) / ROPE_QUARTER))
    ar = (row_ids[:, None] * freqs[None, :]).astype(np.float32).astype(np.float64)
    ac = (col_ids[:, None] * freqs[None, :]).astype(np.float32).astype(np.float64)
    cos = np.concatenate([np.cos(ar), np.cos(ar), np.cos(ac), np.cos(ac)], axis=1)
    sin = np.concatenate([-np.sin(ar), np.sin(ar), -np.sin(ac), np.sin(ac)], axis=1)
    return np.concatenate([cos, sin], axis=1).astype(np.float32)


def _swap_partners(w):
    q = ROPE_QUARTER
    return jnp.concatenate([w[..., q:2 * q], w[..., 0:q], w[..., 3 * q:4 * q], w[..., 2 * q:3 * q]], axis=-1)


def kernel(x_prompt, x_sample, cache_ckv, cache_krope, c, c_ctx, g_mix, g_ffn, w_ada, b_ada, w_pool, pool_scale,
           w_dq, g_q, w_uq, w_dkv, g_kv, w_uk, w_uv, w_o, w_router_grp, b_router_grp, w_router_exp,
           b_router_exp, w_gate, w_up, w_down, g_final):
    n_mla = DEPTH // N_MIXERS
    x = (x_prompt.reshape(N_CTX, D_MODEL), x_sample.reshape(N_LAT, D_MODEL))

    cond = jnp.concatenate([c_ctx[None, :], c, jnp.zeros((COND_ROWS - 1 - DEC_BATCH, D_MODEL), F32)], axis=0)
    mod = _ada_call(cond, w_ada, b_ada).reshape(DEPTH, COND_ROWS, N_MOD, D_MODEL)

    wr = jnp.concatenate([w_router_exp, w_router_grp,
                          jnp.zeros((DEPTH, D_MODEL, ROUTE_LANES - N_EXPERTS - N_EXPERT_GROUPS), F32)],
                         axis=2).astype(BF16)
    br = jnp.concatenate([b_router_exp, b_router_grp,
                          jnp.zeros((DEPTH, ROUTE_LANES - N_EXPERTS - N_EXPERT_GROUPS), F32)], axis=1)
    band = jnp.asarray(_pool_band(), dtype=BF16)
    wpool = w_pool.astype(BF16)
    wdq = w_dq.astype(BF16)
    wdkv = w_dkv[:, :, :KV_RANK].astype(BF16)
    w_kr = w_dkv[:, :, KV_RANK:]
    wkr = jnp.concatenate([w_kr, _swap_partners(w_kr)], axis=-1).astype(BF16)
    wuqn = w_uq[..., :NOPE_DIM].reshape(n_mla, Q_RANK, N_HEADS * NOPE_DIM).astype(BF16)
    w_qr = w_uq[..., NOPE_DIM:]
    wuqr = jnp.concatenate([w_qr, _swap_partners(w_qr)], axis=-1).reshape(n_mla, Q_RANK, N_HEADS * LANES)
    wuqr = wuqr.astype(BF16)
    wuk = w_uk.reshape(n_mla, KV_RANK, N_HEADS * NOPE_DIM).astype(BF16)
    wuv = w_uv.reshape(n_mla, KV_RANK, N_HEADS * V_DIM).astype(BF16)
    wo = w_o.astype(BF16)
    rope_tab = _rope_tables()
    tabk = jnp.asarray(rope_tab)
    tabq = jnp.asarray(np.tile(rope_tab, (1, N_HEADS)))
    cache_kr = jnp.pad(cache_krope, ((0, 0), (0, 0), (0, 0), (0, LANES - ROPE_DIM)))

    def row(a):
        return a.reshape(1, -1)

    ctx_ckv = jnp.zeros((BATCH, n_mla, SEQ, KV_RANK), F32)
    ctx_krope = jnp.zeros((BATCH, n_mla, SEQ, ROPE_DIM), F32)
    for layer in range(DEPTH):
        if layer % N_MIXERS == 0:
            p = layer // N_MIXERS
            streams = x if isinstance(x, tuple) else (x,)
            x_new, h2, route, counts = _pool_call(layer, streams, mod, row(g_mix[layer]), row(g_ffn[layer]),
                                                  band, wpool[p], row(pool_scale[p]), wr[layer], row(br[layer]))
        else:
            m = layer // N_MIXERS
            qn, qr, ckv, kr, ctx_ckv, ctx_krope = _qkv_call(
                layer, x, mod, row(g_mix[layer]), wdq[m], row(g_q[m]), wdkv[m], row(g_kv[m]), wkr[m], wuqn[m],
                wuqr[m], tabq, tabk, ctx_ckv, ctx_krope)
            o_ctx = _attn_call(layer, qn, qr, ckv, kr, wuk[m], wuv[m])
            o_lat = _attn_call(layer, qn, qr, ckv, kr, wuk[m], wuv[m], cache=(cache_ckv, cache_kr))
            x_new, h2, route, counts = _attn_out_call(layer, o_ctx, o_lat, wo[m], x, mod, row(g_ffn[layer]),
                                                      wr[layer], row(br[layer]))
        pos8, *work = _plan_call(layer, counts, route)
        pos = pos8[:2].reshape(N_SLOTS)
        ys = _gmm_call(layer, h2, work, _invert_call(layer, pos), w_gate, w_up, w_down)
        x = _combine_call(layer, pos, x_new, route, ys, mod, row(g_final), final=(layer == DEPTH - 1))

    y_prompt = x[0].reshape(BATCH, SEQ, D_MODEL)
    y_sample = x[1].reshape(DEC_BATCH, DEC_SEQ, D_MODEL)
    return y_prompt, y_sample, ctx_ckv, ctx_krope
```

```python
import functools

import numpy as np
import jax
import jax.numpy as jnp
from jax import lax
from jax.experimental import pallas as pl
from jax.experimental.pallas import tpu as pltpu

D_MODEL = 2048
BATCH = 32
SEQ = 256
DEPTH = 4
DEC_BATCH = 2
DEC_SEQ = 1024
PAST_LEN = 512
GRID_W = 64
N_MIXERS = 2
POOL_WINDOWS = (2, 4, 8, 16)
N_POOL_GROUPS = 4
POOL_CH = D_MODEL // N_POOL_GROUPS
N_HEADS = 16
Q_RANK = 512
KV_RANK = 512
NOPE_DIM = 128
ROPE_DIM = 64
V_DIM = 128
ROPE_QUARTER = ROPE_DIM // 4
ROPE_THETA = 10000.0
ATTN_SCALE = (NOPE_DIM + ROPE_DIM) ** -0.5
N_EXPERT_GROUPS = 4
EXPERTS_PER_GROUP = 8
N_EXPERTS = N_EXPERT_GROUPS * EXPERTS_PER_GROUP
EXPERT_HIDDEN = 512
N_MOD = 6
EPS = 1e-6

LANES = 128
SUBLANES = 8
TM = 256
N_CTX = BATCH * SEQ
N_LAT = DEC_BATCH * DEC_SEQ
N_TOK = N_CTX + N_LAT
CTX_TILES = N_CTX // TM
LAT_TILES = N_LAT // TM
N_TILES = CTX_TILES + LAT_TILES
LAT_TILES_PER_SEQ = DEC_SEQ // TM
COND_ROWS = SUBLANES
POOL_HALO = max(POOL_WINDOWS) // 2
POOL_K = TM + LANES
ROUTE_LANES = LANES
N_SLOTS = 2 * N_TOK
SORT_TILES = N_SLOTS // TM
MAX_WORK = SORT_TILES + N_EXPERTS - 1
MIB = 1024 * 1024

F32 = jnp.float32
BF16 = jnp.bfloat16


def _mod_row(tile):
    return jnp.where(tile < CTX_TILES, 0, 1 + jnp.maximum(tile - CTX_TILES, 0) // LAT_TILES_PER_SEQ)


def _params(vmem_mib, n_axes=1):
    return pltpu.CompilerParams(
        dimension_semantics=("arbitrary",) * n_axes,
        vmem_limit_bytes=vmem_mib * MIB,
    )


def _norm_mod(x, g, shift, scale):
    y = x * lax.rsqrt(jnp.mean(x * x, axis=-1, keepdims=True) + EPS)
    return (y * g) * (1.0 + scale) + shift


def _route(h_bf, wr_ref, br_ref, seen_ref):
    logits = jnp.dot(h_bf, wr_ref[...], preferred_element_type=F32) + br_ref[...]
    n_rows = logits.shape[0]
    lt = logits.T
    sub = lax.broadcasted_iota(jnp.int32, (EXPERTS_PER_GROUP, n_rows), 0).astype(F32)
    neg = -jnp.inf
    far = float(EXPERTS_PER_GROUP)
    gl = jnp.where(sub < N_EXPERT_GROUPS, lt[N_EXPERTS:N_EXPERTS + EXPERTS_PER_GROUP, :], neg)
    gmax = jnp.max(gl, axis=0, keepdims=True)
    grp = jnp.min(jnp.where(gl == gmax, sub, far), axis=0, keepdims=True)
    p_grp = 1.0 / jnp.sum(jnp.exp(gl - gmax), axis=0, keepdims=True)
    el = lt[0:EXPERTS_PER_GROUP, :]
    for g in range(1, N_EXPERT_GROUPS):
        el = jnp.where(grp == float(g), lt[g * EXPERTS_PER_GROUP:(g + 1) * EXPERTS_PER_GROUP, :], el)
    m1 = jnp.max(el, axis=0, keepdims=True)
    j1 = jnp.min(jnp.where(el == m1, sub, far), axis=0, keepdims=True)
    el2 = jnp.where(sub == j1, neg, el)
    m2 = jnp.max(el2, axis=0, keepdims=True)
    j2 = jnp.min(jnp.where((el2 == m2) & (sub != j1), sub, far), axis=0, keepdims=True)
    e2 = jnp.exp(m2 - m1)
    p1 = p_grp / (1.0 + e2)
    p2 = p1 * e2
    base = grp * float(EXPERTS_PER_GROUP)
    picked = jnp.concatenate(
        [base + j1, base + j2, p1, p2, jnp.zeros((ROUTE_LANES - 4, n_rows), F32)], axis=0).T
    i1, i2, p1, p2 = (picked[:, k:k + 1] for k in range(4))
    lane = lax.broadcasted_iota(jnp.int32, logits.shape, 1).astype(F32)
    earlier = (lax.broadcasted_iota(jnp.int32, (n_rows, n_rows), 0)
               > lax.broadcasted_iota(jnp.int32, (n_rows, n_rows), 1)).astype(BF16)
    hot1 = (lane == i1).astype(F32)
    hot2 = (lane == i2).astype(F32)
    before1 = jnp.dot(earlier, hot1.astype(BF16), preferred_element_type=F32)
    before2 = jnp.dot(earlier, hot2.astype(BF16), preferred_element_type=F32)
    cnt1 = jnp.sum(hot1, axis=0, keepdims=True)
    cnt2 = jnp.sum(hot2, axis=0, keepdims=True)
    seen = seen_ref[...]
    rank1 = jnp.sum(hot1 * (seen + before1), axis=1, keepdims=True)
    rank2 = jnp.sum(hot2 * (seen + cnt1 + before2), axis=1, keepdims=True)
    seen_ref[...] = seen + cnt1 + cnt2

    out = jnp.where(lane == 0, i1, 0.0)
    for k, col in enumerate((i2, p1, p2, rank1, rank2), start=1):
        out = jnp.where(lane == k, col, out)
    return out


def _mixer_tail(x, mix_out, mod_ref, gffn_ref, wr_ref, br_ref, xnew_ref, h2_ref, route_ref, cnt_ref, seen_ref):
    @pl.when(pl.program_id(0) == 0)
    def _():
        seen_ref[...] = jnp.zeros_like(seen_ref)

    x_new = x + mod_ref[2:3, :] * mix_out
    xnew_ref[...] = x_new
    h2 = _norm_mod(x_new, gffn_ref[...], mod_ref[3:4, :], mod_ref[4:5, :])
    h2_ref[...] = h2
    route_ref[...] = _route(h2.astype(BF16), wr_ref, br_ref, seen_ref)
    cnt_ref[...] = seen_ref[...].astype(jnp.int32)


ADA_TN = 1024


def _ada_kernel(cond_ref, w_ref, b_ref, o_ref):
    c = cond_ref[...]
    s = c * jax.nn.sigmoid(c)
    o_ref[...] = jnp.dot(s.astype(BF16), w_ref[...].astype(BF16), preferred_element_type=F32) + b_ref[...]


def _ada_call(cond, w_ada, b_ada):
    nj = (N_MOD * D_MODEL) // ADA_TN
    return pl.pallas_call(
        _ada_kernel,
        out_shape=jax.ShapeDtypeStruct((DEPTH, COND_ROWS, N_MOD * D_MODEL), F32),
        grid=(DEPTH, nj),
        in_specs=[
            pl.BlockSpec((COND_ROWS, D_MODEL), lambda l, j: (0, 0)),
            pl.BlockSpec((None, D_MODEL, ADA_TN), lambda l, j: (l, 0, j)),
            pl.BlockSpec((None, 1, ADA_TN), lambda l, j: (l, 0, j)),
        ],
        out_specs=pl.BlockSpec((None, COND_ROWS, ADA_TN), lambda l, j: (l, 0, j)),
        compiler_params=_params(40, 2),
        name="ada_mod",
    )(cond, w_ada, b_ada.reshape(DEPTH, 1, N_MOD * D_MODEL))


def _pool_band():
    col = np.arange(POOL_K)
    rel = np.where(col < TM, col, np.where(col < TM + POOL_HALO, col - TM - POOL_HALO, col - POOL_HALO))
    ok = col < TM + 2 * POOL_HALO
    row = np.arange(TM)[:, None]
    mats = []
    for w in POOL_WINDOWS:
        mats.append((rel[None, :] >= row - w // 2) & (rel[None, :] < row + (w - w // 2)) & ok[None, :])
    return np.stack(mats).astype(np.float32)


def _pool_kernel(*refs, split_streams):
    if split_streams:
        xc_ref, xl_ref, *refs = refs
    else:
        x_ref, *refs = refs
    (xp_ref, xn_ref, mod_ref, gmix_ref, gffn_ref, band_ref, wpool_ref, pscale_ref, wr_ref, br_ref,
     xnew_ref, h2_ref, route_ref, cnt_ref, seen_ref) = refs
    i = pl.program_id(0)
    lat = i >= CTX_TILES
    tile_pos = jnp.where(lat, jnp.maximum(i - CTX_TILES, 0) % LAT_TILES_PER_SEQ, 0)
    seq_len = jnp.where(lat, DEC_SEQ, SEQ)
    has_prev = tile_pos > 0
    has_next = (tile_pos + 1) * TM < seq_len

    g = gmix_ref[...]
    shift, scale = mod_ref[0:1, :], mod_ref[1:2, :]
    x = jnp.where(lat, xl_ref[...], xc_ref[...]) if split_streams else x_ref[...]
    h = _norm_mod(x, g, shift, scale)
    hp = jnp.where(has_prev, _norm_mod(xp_ref[...], g, shift, scale), 0.0)
    hn = jnp.where(has_next, _norm_mod(xn_ref[...], g, shift, scale), 0.0)
    hall = jnp.concatenate(
        [h, hp, hn, jnp.zeros((POOL_K - TM - 2 * POOL_HALO, D_MODEL), F32)], axis=0)
    hi = hall.astype(BF16)
    lo = (hall - hi.astype(F32)).astype(BF16)

    pos = tile_pos * TM + lax.broadcasted_iota(jnp.int32, (TM, 1), 0)
    outs = []
    for gi, w in enumerate(POOL_WINDOWS):
        sl = slice(gi * POOL_CH, (gi + 1) * POOL_CH)
        band = band_ref[gi]
        wsum = (jnp.dot(band, hi[:, sl], preferred_element_type=F32)
                + jnp.dot(band, lo[:, sl], preferred_element_type=F32))
        cnt = jnp.minimum(pos + (w - w // 2), seq_len) - jnp.maximum(pos - w // 2, 0)
        diff = wsum / cnt.astype(F32) - h[:, sl]
        outs.append(jnp.dot(diff.astype(BF16), wpool_ref[gi], preferred_element_type=F32))
    mix = jnp.concatenate(outs, axis=1) * pscale_ref[...]
    _mixer_tail(x, mix, mod_ref, gffn_ref, wr_ref, br_ref, xnew_ref, h2_ref, route_ref, cnt_ref, seen_ref)


def _tail_out_shapes():
    return (jax.ShapeDtypeStruct((N_TOK, D_MODEL), F32),
            jax.ShapeDtypeStruct((N_TOK, D_MODEL), F32),
            jax.ShapeDtypeStruct((N_TOK, ROUTE_LANES), F32),
            jax.ShapeDtypeStruct((1, ROUTE_LANES), jnp.int32))


def _tail_out_specs():
    return (pl.BlockSpec((TM, D_MODEL), lambda i: (i, 0)),
            pl.BlockSpec((TM, D_MODEL), lambda i: (i, 0)),
            pl.BlockSpec((TM, ROUTE_LANES), lambda i: (i, 0)),
            pl.BlockSpec((1, ROUTE_LANES), lambda i: (0, 0)))


def _tail_scratch():
    return [pltpu.VMEM((1, ROUTE_LANES), F32)]


def _row_spec(width):
    return pl.BlockSpec((1, width), lambda i: (0, 0))


def _mod_spec(layer):
    return pl.BlockSpec((None, None, N_MOD, D_MODEL), lambda i: (layer, _mod_row(i), 0, 0))


def _pool_call(layer, xs, mod, g_mix, g_ffn, band, wpool, pscale, wr, br):
    halo_blocks = TM // POOL_HALO
    split_streams = len(xs) == 2
    tile0 = CTX_TILES if split_streams else 0
    last_halo = xs[-1].shape[0] // POOL_HALO - 1

    def prev_map(i):
        return (jnp.clip((i - tile0) * halo_blocks - 1, 0, last_halo), 0)

    def next_map(i):
        return (jnp.clip((i - tile0 + 1) * halo_blocks, 0, last_halo), 0)

    if split_streams:
        main_specs = [pl.BlockSpec((TM, D_MODEL), lambda i: (jnp.minimum(i, CTX_TILES - 1), 0)),
                      pl.BlockSpec((TM, D_MODEL), lambda i: (jnp.maximum(i - CTX_TILES, 0), 0))]
    else:
        main_specs = [pl.BlockSpec((TM, D_MODEL), lambda i: (i, 0))]
    return pl.pallas_call(
        functools.partial(_pool_kernel, split_streams=split_streams),
        out_shape=_tail_out_shapes(),
        grid=(N_TILES,),
        in_specs=main_specs + [
            pl.BlockSpec((POOL_HALO, D_MODEL), prev_map),
            pl.BlockSpec((POOL_HALO, D_MODEL), next_map),
            _mod_spec(layer),
            _row_spec(D_MODEL),
            _row_spec(D_MODEL),
            pl.BlockSpec((N_POOL_GROUPS, TM, POOL_K), lambda i: (0, 0, 0)),
            pl.BlockSpec((N_POOL_GROUPS, POOL_CH, POOL_CH), lambda i: (0, 0, 0)),
            _row_spec(D_MODEL),
            pl.BlockSpec((D_MODEL, ROUTE_LANES), lambda i: (0, 0)),
            _row_spec(ROUTE_LANES),
        ],
        out_specs=_tail_out_specs(),
        scratch_shapes=_tail_scratch(),
        compiler_params=_params(48),
        name=f"pool_layer{layer}",
    )(*xs, xs[-1], xs[-1], mod, g_mix, g_ffn, band, wpool, pscale, wr, br)


def _qkv_kernel(x_ref, mod_ref, gmix_ref, wdq_ref, gq_ref, wdkv_ref, gkv_ref, wkr_ref, wuqn_ref, wuqr_ref,
                tabq_ref, tabk_ref, _ckv_leaf_in, _kr_leaf_in, qn_ref, qr_ref, ckv_ref, kr_ref, ckv_leaf_ref,
                kr_leaf_ref):
    i = pl.program_id(0)
    h = _norm_mod(x_ref[...], gmix_ref[...], mod_ref[0:1, :], mod_ref[1:2, :]).astype(BF16)

    def rms(v, g):
        return v * lax.rsqrt(jnp.mean(v * v, axis=-1, keepdims=True) + EPS) * g

    cq = rms(jnp.dot(h, wdq_ref[...], preferred_element_type=F32), gq_ref[...]).astype(BF16)
    ckv = rms(jnp.dot(h, wdkv_ref[...], preferred_element_type=F32), gkv_ref[...])
    ckv_ref[...] = ckv
    qn_ref[...] = (jnp.dot(cq, wuqn_ref[...], preferred_element_type=F32) * ATTN_SCALE).astype(BF16)
    qr = jnp.dot(cq, wuqr_ref[...], preferred_element_type=F32) * ATTN_SCALE
    kr = jnp.dot(h, wkr_ref[...], preferred_element_type=F32)
    lane = lax.broadcasted_iota(jnp.int32, kr.shape, 1)

    @pl.when(i < CTX_TILES)
    def _():
        qr_ref[...] = qr.astype(BF16)
        kr_ref[...] = jnp.where(lane < ROPE_DIM, kr, 0.0)
        ckv_leaf_ref[...] = ckv
        kr_leaf_ref[...] = kr[:, :ROPE_DIM]

    @pl.when(i >= CTX_TILES)
    def _():
        tq = qr * tabq_ref[...]
        qr_ref[...] = (tq + pltpu.roll(tq, N_HEADS * LANES - ROPE_DIM, 1)).astype(BF16)
        tk = kr * tabk_ref[...]
        kr_ref[...] = jnp.where(lane < ROPE_DIM, tk + pltpu.roll(tk, ROPE_DIM, 1), 0.0)


def _qkv_call(layer, x, mod, g_mix, wdq, gq, wdkv, gkv, wkr, wuqn, wuqr, tabq, tabk, ckv_leaf, kr_leaf):
    m = layer // N_MIXERS

    def leaf_spec(width):
        return pl.BlockSpec((None, None, SEQ, width), lambda i: (jnp.minimum(i, CTX_TILES - 1), m, 0, 0))

    hd = N_HEADS * LANES

    def lat_tile(i):
        return jnp.maximum(i - CTX_TILES, 0) % LAT_TILES_PER_SEQ

    def full(a):
        return pl.BlockSpec(a.shape, lambda i: (0,) * a.ndim)

    return pl.pallas_call(
        _qkv_kernel,
        out_shape=(jax.ShapeDtypeStruct((N_TOK, hd), BF16),
                   jax.ShapeDtypeStruct((N_TOK, hd), BF16),
                   jax.ShapeDtypeStruct((N_TOK, KV_RANK), F32),
                   jax.ShapeDtypeStruct((N_TOK, LANES), F32),
                   jax.ShapeDtypeStruct(ckv_leaf.shape, F32),
                   jax.ShapeDtypeStruct(kr_leaf.shape, F32)),
        grid=(N_TILES,),
        in_specs=[
            pl.BlockSpec((TM, D_MODEL), lambda i: (i, 0)),
            _mod_spec(layer),
            _row_spec(D_MODEL),
            full(wdq), _row_spec(Q_RANK), full(wdkv), _row_spec(KV_RANK), full(wkr), full(wuqn), full(wuqr),
            pl.BlockSpec((TM, hd), lambda i: (lat_tile(i), 0)),
            pl.BlockSpec((TM, LANES), lambda i: (lat_tile(i), 0)),
            pl.BlockSpec(memory_space=pl.ANY),
            pl.BlockSpec(memory_space=pl.ANY),
        ],
        out_specs=(pl.BlockSpec((TM, hd), lambda i: (i, 0)),
                   pl.BlockSpec((TM, hd), lambda i: (i, 0)),
                   pl.BlockSpec((TM, KV_RANK), lambda i: (i, 0)),
                   pl.BlockSpec((TM, LANES), lambda i: (i, 0)),
                   leaf_spec(KV_RANK),
                   leaf_spec(ROPE_DIM)),
        input_output_aliases={12: 4, 13: 5},
        compiler_params=_params(48),
        name=f"mla_qkv{layer}",
    )(x, mod, g_mix, wdq, gq, wdkv, gkv, wkr, wuqn, wuqr, tabq, tabk, ckv_leaf, kr_leaf)


HEAD_K = 2 * LANES


def _attn_kernel(*refs, n_new, n_cache):
    if n_cache:
        (qn_ref, qr_ref, ckv_ref, kr_ref, cckv_ref, ckr_ref, wuk_ref, wuv_ref, o_ref, kcat, vs) = refs
    else:
        (qn_ref, qr_ref, ckv_ref, kr_ref, wuk_ref, wuv_ref, o_ref, kcat, vs) = refs
        cckv_ref = ckr_ref = None

    @pl.when(pl.program_id(1) == 0)
    def _():
        def expand(src_ref, kr_src_ref, n_rows, base):
            for c in range(n_rows // TM):
                rows = slice(c * TM, (c + 1) * TM)
                dst = slice(base + c * TM, base + (c + 1) * TM)
                lat = src_ref[rows, :].astype(BF16)
                kn = jnp.dot(lat, wuk_ref[...], preferred_element_type=F32).astype(BF16)
                vs[dst, :] = jnp.dot(lat, wuv_ref[...], preferred_element_type=F32).astype(BF16)
                krb = kr_src_ref[rows, :].astype(BF16)
                for hh in range(N_HEADS):
                    kcat[dst, hh * HEAD_K:hh * HEAD_K + LANES] = kn[:, hh * LANES:(hh + 1) * LANES]
                    kcat[dst, hh * HEAD_K + LANES:(hh + 1) * HEAD_K] = krb

        expand(ckv_ref, kr_ref, n_new, 0)
        if n_cache:
            expand(cckv_ref, ckr_ref, n_cache, n_new)

    for hh in range(N_HEADS):
        hs = slice(hh * LANES, (hh + 1) * LANES)
        q = jnp.concatenate([qn_ref[:, hs], qr_ref[:, hs]], axis=1)
        s = lax.dot_general(q, kcat[:, hh * HEAD_K:(hh + 1) * HEAD_K], (((1,), (1,)), ((), ())),
                            preferred_element_type=F32)
        p = jnp.exp(s - jnp.max(s, axis=1, keepdims=True))
        den = jnp.sum(p, axis=1, keepdims=True)
        o = jnp.dot(p.astype(BF16), vs[:, hs], preferred_element_type=F32) / den
        o_ref[:, hs] = o.astype(BF16)


def _attn_call(layer, qn, qr, ckv, kr, wuk, wuv, cache=None):
    hd = N_HEADS * LANES
    if cache is None:
        n_b, n_q, n_new, n_cache, tile0 = BATCH, SEQ // TM, SEQ, 0, 0
    else:
        n_b, n_q, n_new, n_cache, tile0 = DEC_BATCH, DEC_SEQ // TM, DEC_SEQ, PAST_LEN, CTX_TILES
    seq_blk0 = tile0 * TM // n_new
    m = layer // N_MIXERS

    def q_map(b, j):
        return (tile0 + b * n_q + j, 0)

    in_specs = [
        pl.BlockSpec((TM, hd), q_map),
        pl.BlockSpec((TM, hd), q_map),
        pl.BlockSpec((n_new, KV_RANK), lambda b, j: (seq_blk0 + b, 0)),
        pl.BlockSpec((n_new, LANES), lambda b, j: (seq_blk0 + b, 0)),
    ]
    args = [qn, qr, ckv, kr]
    if cache is not None:
        in_specs += [
            pl.BlockSpec((None, None, n_cache, KV_RANK), lambda b, j: (b, m, 0, 0)),
            pl.BlockSpec((None, None, n_cache, LANES), lambda b, j: (b, m, 0, 0)),
        ]
        args += list(cache)
    in_specs += [
        pl.BlockSpec((KV_RANK, hd), lambda b, j: (0, 0)),
        pl.BlockSpec((KV_RANK, hd), lambda b, j: (0, 0)),
    ]
    args += [wuk, wuv]
    n_keys = n_new + n_cache
    return pl.pallas_call(
        functools.partial(_attn_kernel, n_new=n_new, n_cache=n_cache),
        out_shape=jax.ShapeDtypeStruct((n_b * n_q * TM, hd), BF16),
        grid=(n_b, n_q),
        in_specs=in_specs,
        out_specs=pl.BlockSpec((TM, hd), lambda b, j: (b * n_q + j, 0)),
        scratch_shapes=[pltpu.VMEM((n_keys, N_HEADS * HEAD_K), BF16), pltpu.VMEM((n_keys, hd), BF16)],
        compiler_params=_params(60, 2),
        name=f"mla_attn{layer}_{'lat' if cache is not None else 'ctx'}",
    )(*args)


def _attn_out_kernel(octx_ref, olat_ref, wo_ref, x_ref, mod_ref, gffn_ref, wr_ref, br_ref, xnew_ref, h2_ref,
                     route_ref, cnt_ref, seen_ref):
    o = jnp.where(pl.program_id(0) >= CTX_TILES, olat_ref[...], octx_ref[...])
    mix = jnp.dot(o, wo_ref[...], preferred_element_type=F32)
    _mixer_tail(x_ref[...], mix, mod_ref, gffn_ref, wr_ref, br_ref, xnew_ref, h2_ref, route_ref, cnt_ref,
                seen_ref)


def _attn_out_call(layer, o_ctx, o_lat, wo, x, mod, g_ffn, wr, br):
    hd = N_HEADS * V_DIM
    return pl.pallas_call(
        _attn_out_kernel,
        out_shape=_tail_out_shapes(),
        grid=(N_TILES,),
        in_specs=[
            pl.BlockSpec((TM, hd), lambda i: (jnp.minimum(i, CTX_TILES - 1), 0)),
            pl.BlockSpec((TM, hd), lambda i: (jnp.maximum(i - CTX_TILES, 0), 0)),
            pl.BlockSpec((hd, D_MODEL), lambda i: (0, 0)),
            pl.BlockSpec((TM, D_MODEL), lambda i: (i, 0)),
            _mod_spec(layer),
            _row_spec(D_MODEL),
            pl.BlockSpec((D_MODEL, ROUTE_LANES), lambda i: (0, 0)),
            _row_spec(ROUTE_LANES),
        ],
        out_specs=_tail_out_specs(),
        scratch_shapes=_tail_scratch(),
        compiler_params=_params(48),
        name=f"mla_out{layer}",
    )(o_ctx, o_lat, wo, x, mod, g_ffn, wr, br)


def _plan_kernel(cnt_ref, route_ref, pos_ref, wt_ref, we_ref, wlo_ref, whi_ref, nw_ref, start_ref):
    @pl.when(pl.program_id(0) == 0)
    def _():
        lane = lax.broadcasted_iota(jnp.int32, (1, ROUTE_LANES), 1)
        shift = TM.bit_length() - 1

        def per_expert(e, carry):
            n_work, lo, last_e, starts = carry
            cnt = cnt_ref[0, e]
            hi = lo + cnt
            first_tile = lax.shift_right_logical(lo, shift)
            end_tile = jnp.where(cnt > 0, lax.shift_right_logical(hi + (TM - 1), shift), first_tile)

            def per_tile(t, w):
                wt_ref[w] = t
                we_ref[w] = e
                wlo_ref[w] = lo
                whi_ref[w] = hi
                return w + 1

            n_work = lax.fori_loop(first_tile, end_tile, per_tile, n_work)
            starts = jnp.where(lane == e, lo.astype(F32), starts)
            return n_work, hi, jnp.where(cnt > 0, e, last_e), starts

        n_work, _, last_e, starts = lax.fori_loop(
            0, N_EXPERTS, per_expert,
            (jnp.int32(0), jnp.int32(0), jnp.int32(0), jnp.zeros((1, ROUTE_LANES), F32)))
        nw_ref[0] = n_work
        start_ref[...] = starts

        def fill(w, carry):
            wt_ref[w] = SORT_TILES - 1
            we_ref[w] = last_e
            wlo_ref[w] = 0
            whi_ref[w] = 0
            return carry

        lax.fori_loop(n_work, MAX_WORK, fill, 0)

    r = route_ref[...]
    lane = lax.broadcasted_iota(jnp.int32, r.shape, 1).astype(F32)
    starts = start_ref[...]
    pos1 = jnp.sum(jnp.where(lane == r[:, 0:1], starts, 0.0), axis=1, keepdims=True) + r[:, 4:5]
    pos2 = jnp.sum(jnp.where(lane == r[:, 1:2], starts, 0.0), axis=1, keepdims=True) + r[:, 5:6]
    pos = jnp.where(lane == 0.0, pos1, jnp.where(lane == 1.0, pos2, 0.0))
    pos_ref[...] = pos.T[0:SUBLANES, :].astype(jnp.int32)


PLAN_ROWS = 4 * TM


def _plan_call(layer, counts, route):
    smem = pl.BlockSpec(memory_space=pltpu.SMEM)
    work = jax.ShapeDtypeStruct((MAX_WORK,), jnp.int32)
    return pl.pallas_call(
        _plan_kernel,
        out_shape=(jax.ShapeDtypeStruct((SUBLANES, N_TOK), jnp.int32), work, work, work, work,
                   jax.ShapeDtypeStruct((1,), jnp.int32)),
        grid=(N_TOK // PLAN_ROWS,),
        in_specs=[smem, pl.BlockSpec((PLAN_ROWS, ROUTE_LANES), lambda i: (i, 0))],
        out_specs=(pl.BlockSpec((SUBLANES, PLAN_ROWS), lambda i: (0, i)), smem, smem, smem, smem, smem),
        scratch_shapes=[pltpu.VMEM((1, ROUTE_LANES), F32)],
        compiler_params=_params(16),
        name=f"moe_plan{layer}",
    )(counts, route)


INVERT_UNROLL = 16


def _invert_kernel(pos_ref, tok_ref):
    for k in range(2):
        def body(j, carry, k=k):
            t0 = j * INVERT_UNROLL
            where = [pos_ref[k * N_TOK + t0 + u] for u in range(INVERT_UNROLL)]
            for u in range(INVERT_UNROLL):
                tok_ref[where[u]] = t0 + u
            return carry

        lax.fori_loop(0, N_TOK // INVERT_UNROLL, body, 0)


def _invert_call(layer, pos):
    smem = pl.BlockSpec(memory_space=pltpu.SMEM)
    return pl.pallas_call(
        _invert_kernel,
        out_shape=jax.ShapeDtypeStruct((N_SLOTS,), jnp.int32),
        in_specs=[smem],
        out_specs=smem,
        name=f"moe_invert{layer}",
    )(pos)


def _gmm_kernel(wt_ref, we_ref, wlo_ref, whi_ref, nw_ref, tok_ref, h2_hbm, wg_hbm, wu_hbm, wd_hbm, ys_ref,
                xbuf, wg_f, wu_f, wd_f, wg_s, wu_s, wd_s, sem_x, sem, nexp_ref, *, layer):
    w = pl.program_id(0)
    n_work = nw_ref[0]
    prev = jnp.maximum(w - 1, 0)
    tile = wt_ref[w]
    new_expert = (w == 0) | (we_ref[w] != we_ref[prev])
    new_tile = (w == 0) | (tile != wt_ref[prev])

    def weight_copies(expert, slot):
        return [pltpu.make_async_copy(hbm.at[layer, expert], buf.at[slot], sem.at[slot, k])
                for k, (hbm, buf) in enumerate(((wg_hbm, wg_f), (wu_hbm, wu_f), (wd_hbm, wd_f)))]

    def gather(t):
        slot = t % 2
        for r in range(TM):
            pltpu.make_async_copy(h2_hbm.at[pl.ds(tok_ref[t * TM + r], 1)], xbuf.at[slot, pl.ds(r, 1)],
                                  sem_x.at[slot]).start()

    @pl.when(w == 0)
    def _():
        nexp_ref[0] = 0
        for cp in weight_copies(we_ref[0], 0):
            cp.start()
        gather(tile)

    @pl.when((w < n_work) & new_tile & (tile + 1 < SORT_TILES))
    def _():
        gather(tile + 1)

    @pl.when((w < n_work) & new_expert)
    def _():
        slot = nexp_ref[0] % 2
        nexp_ref[0] += 1
        nxt = lax.while_loop(
            lambda k: (k < n_work) & (we_ref[jnp.minimum(k, MAX_WORK - 1)] == we_ref[w]),
            lambda k: k + 1, w + 1)

        @pl.when(nxt < n_work)
        def _():
            for cp in weight_copies(we_ref[jnp.minimum(nxt, MAX_WORK - 1)], 1 - slot):
                cp.start()

        for cp in weight_copies(we_ref[w], slot):
            cp.wait()
        wg_s[...] = wg_f[slot].astype(BF16)
        wu_s[...] = wu_f[slot].astype(BF16)
        wd_s[...] = wd_f[slot].astype(BF16)

    @pl.when((w < n_work) & new_tile)
    def _():
        slot = tile % 2
        pltpu.make_async_copy(h2_hbm.at[pl.ds(0, TM)], xbuf.at[slot], sem_x.at[slot]).wait()

    @pl.when(w < n_work)
    def _():
        x = xbuf[tile % 2].astype(BF16)
        gate = jnp.dot(x, wg_s[...], preferred_element_type=F32)
        up = jnp.dot(x, wu_s[...], preferred_element_type=F32)
        row = tile * TM + lax.broadcasted_iota(jnp.int32, (TM, 1), 0)
        mine = (row >= wlo_ref[w]) & (row < whi_ref[w])
        hid = jnp.where(mine, (gate * jax.nn.sigmoid(gate)) * up, 0.0)
        y = jnp.dot(hid.astype(BF16), wd_s[...], preferred_element_type=F32)

        @pl.when(new_tile)
        def _():
            ys_ref[...] = y

        @pl.when(jnp.logical_not(new_tile))
        def _():
            ys_ref[...] += y


def _gmm_call(layer, h2, work, slot_tok, w_gate, w_up, w_down):
    up_shape, down_shape = (D_MODEL, EXPERT_HIDDEN), (EXPERT_HIDDEN, D_MODEL)
    hbm = pl.BlockSpec(memory_space=pl.ANY)
    grid_spec = pltpu.PrefetchScalarGridSpec(
        num_scalar_prefetch=6,
        grid=(MAX_WORK,),
        in_specs=[hbm, hbm, hbm, hbm],
        out_specs=pl.BlockSpec((TM, D_MODEL), lambda w, wt, we, wlo, whi, nw, tok: (wt[w], 0)),
        scratch_shapes=[
            pltpu.VMEM((2, TM, D_MODEL), F32),
            pltpu.VMEM((2,) + up_shape, F32),
            pltpu.VMEM((2,) + up_shape, F32),
            pltpu.VMEM((2,) + down_shape, F32),
            pltpu.VMEM(up_shape, BF16),
            pltpu.VMEM(up_shape, BF16),
            pltpu.VMEM(down_shape, BF16),
            pltpu.SemaphoreType.DMA((2,)),
            pltpu.SemaphoreType.DMA((2, 3)),
            pltpu.SMEM((1,), jnp.int32),
        ],
    )
    return pl.pallas_call(
        functools.partial(_gmm_kernel, layer=layer),
        out_shape=jax.ShapeDtypeStruct((N_SLOTS, D_MODEL), F32),
        grid_spec=grid_spec,
        compiler_params=_params(48),
        name=f"moe_gmm{layer}",
    )(*work, slot_tok, h2, w_gate, w_up, w_down)


def _combine_kernel(pos_ref, x_ref, route_ref, mod_ref, gfin_ref, ys_hbm, *rest, final):
    if final:
        octx_ref, olat_ref, buf1, buf2, sem1, sem2 = rest
    else:
        o_ref, buf1, buf2, sem1, sem2 = rest
    i = pl.program_id(0)

    def fetch(tile, slot):
        def issue(r, carry):
            t = tile * TM + r
            pltpu.make_async_copy(ys_hbm.at[pl.ds(pos_ref[t], 1)], buf1.at[slot, pl.ds(r, 1)],
                                  sem1.at[slot]).start(priority=0)
            pltpu.make_async_copy(ys_hbm.at[pl.ds(pos_ref[N_TOK + t], 1)], buf2.at[slot, pl.ds(r, 1)],
                                  sem2.at[slot]).start(priority=1)
            return carry

        for r in range(TM):
            issue(r, 0)

    def wait_rows(slot):
        pltpu.make_async_copy(ys_hbm.at[pl.ds(0, TM)], buf1.at[slot], sem1.at[slot]).wait()
        pltpu.make_async_copy(ys_hbm.at[pl.ds(0, TM)], buf2.at[slot], sem2.at[slot]).wait()

    @pl.when(i == 0)
    def _():
        fetch(i, 0)

    slot = i % 2

    @pl.when(i + 1 < N_TILES)
    def _():
        fetch(i + 1, 1 - slot)

    wait_rows(slot)
    r = route_ref[...]
    moe = r[:, 2:3] * buf1[slot] + r[:, 3:4] * buf2[slot]
    x = x_ref[...] + mod_ref[5:6, :] * moe
    if final:
        y = x * lax.rsqrt(jnp.mean(x * x, axis=-1, keepdims=True) + EPS) * gfin_ref[...]

        @pl.when(i < CTX_TILES)
        def _():
            octx_ref[...] = y

        @pl.when(i >= CTX_TILES)
        def _():
            olat_ref[...] = y
    else:
        o_ref[...] = x


def _combine_call(layer, pos, x_new, route, ys, mod, g_final, final):
    def tok_spec(width):
        return pl.BlockSpec((TM, width), lambda i, pos: (i, 0))

    if final:
        out_shape = (jax.ShapeDtypeStruct((N_CTX, D_MODEL), F32), jax.ShapeDtypeStruct((N_LAT, D_MODEL), F32))
        out_specs = (pl.BlockSpec((TM, D_MODEL), lambda i, pos: (jnp.minimum(i, CTX_TILES - 1), 0)),
                     pl.BlockSpec((TM, D_MODEL), lambda i, pos: (jnp.maximum(i - CTX_TILES, 0), 0)))
    else:
        out_shape = jax.ShapeDtypeStruct((N_TOK, D_MODEL), F32)
        out_specs = tok_spec(D_MODEL)

    grid_spec = pltpu.PrefetchScalarGridSpec(
        num_scalar_prefetch=1,
        grid=(N_TILES,),
        in_specs=[
            tok_spec(D_MODEL),
            tok_spec(ROUTE_LANES),
            pl.BlockSpec((None, None, N_MOD, D_MODEL), lambda i, pos: (layer, _mod_row(i), 0, 0)),
            pl.BlockSpec((1, D_MODEL), lambda i, pos: (0, 0)),
            pl.BlockSpec(memory_space=pl.ANY),
        ],
        out_specs=out_specs,
        scratch_shapes=[
            pltpu.VMEM((2, TM, D_MODEL), F32),
            pltpu.VMEM((2, TM, D_MODEL), F32),
            pltpu.SemaphoreType.DMA((2,)),
            pltpu.SemaphoreType.DMA((2,)),
        ],
    )
    return pl.pallas_call(
        functools.partial(_combine_kernel, final=final),
        out_shape=out_shape,
        grid_spec=grid_spec,
        compiler_params=_params(40),
        name=f"moe_combine{layer}",
    )(pos, x_new, route, mod, g_final, ys)


def _rope_tables():
    rows = DEC_SEQ // GRID_W
    row_ids = np.repeat(np.arange(rows), GRID_W).astype(np.float32)
    col_ids = np.tile(np.arange(GRID_W), rows).astype(np.float32)
    freqs = (np.float32(ROPE_THETA) ** (-np.arange(ROPE_QUARTER, dtype=np.float32) / ROPE_QUARTER))
    ar = (row_ids[:, None] * freqs[None, :]).astype(np.float32).astype(np.float64)
    ac = (col_ids[:, None] * freqs[None, :]).astype(np.float32).astype(np.float64)
    cos = np.concatenate([np.cos(ar), np.cos(ar), np.cos(ac), np.cos(ac)], axis=1)
    sin = np.concatenate([-np.sin(ar), np.sin(ar), -np.sin(ac), np.sin(ac)], axis=1)
    return np.concatenate([cos, sin], axis=1).astype(np.float32)


def _swap_partners(w):
    q = ROPE_QUARTER
    return jnp.concatenate([w[..., q:2 * q], w[..., 0:q], w[..., 3 * q:4 * q], w[..., 2 * q:3 * q]], axis=-1)


def kernel(x_prompt, x_sample, cache_ckv, cache_krope, c, c_ctx, g_mix, g_ffn, w_ada, b_ada, w_pool, pool_scale,
           w_dq, g_q, w_uq, w_dkv, g_kv, w_uk, w_uv, w_o, w_router_grp, b_router_grp, w_router_exp,
           b_router_exp, w_gate, w_up, w_down, g_final):
    n_mla = DEPTH // N_MIXERS
    x = (x_prompt.reshape(N_CTX, D_MODEL), x_sample.reshape(N_LAT, D_MODEL))

    cond = jnp.concatenate([c_ctx[None, :], c, jnp.zeros((COND_ROWS - 1 - DEC_BATCH, D_MODEL), F32)], axis=0)
    mod = _ada_call(cond, w_ada, b_ada).reshape(DEPTH, COND_ROWS, N_MOD, D_MODEL)

    wr = jnp.concatenate([w_router_exp, w_router_grp,
                          jnp.zeros((DEPTH, D_MODEL, ROUTE_LANES - N_EXPERTS - N_EXPERT_GROUPS), F32)],
                         axis=2).astype(BF16)
    br = jnp.concatenate([b_router_exp, b_router_grp,
                          jnp.zeros((DEPTH, ROUTE_LANES - N_EXPERTS - N_EXPERT_GROUPS), F32)], axis=1)
    band = jnp.asarray(_pool_band(), dtype=BF16)
    wpool = w_pool.astype(BF16)
    wdq = w_dq.astype(BF16)
    wdkv = w_dkv[:, :, :KV_RANK].astype(BF16)
    w_kr = w_dkv[:, :, KV_RANK:]
    wkr = jnp.concatenate([w_kr, _swap_partners(w_kr)], axis=-1).astype(BF16)
    wuqn = w_uq[..., :NOPE_DIM].reshape(n_mla, Q_RANK, N_HEADS * NOPE_DIM).astype(BF16)
    w_qr = w_uq[..., NOPE_DIM:]
    wuqr = jnp.concatenate([w_qr, _swap_partners(w_qr)], axis=-1).reshape(n_mla, Q_RANK, N_HEADS * LANES)
    wuqr = wuqr.astype(BF16)
    wuk = w_uk.reshape(n_mla, KV_RANK, N_HEADS * NOPE_DIM).astype(BF16)
    wuv = w_uv.reshape(n_mla, KV_RANK, N_HEADS * V_DIM).astype(BF16)
    wo = w_o.astype(BF16)
    rope_tab = _rope_tables()
    tabk = jnp.asarray(rope_tab)
    tabq = jnp.asarray(np.tile(rope_tab, (1, N_HEADS)))
    cache_kr = jnp.pad(cache_krope, ((0, 0), (0, 0), (0, 0), (0, LANES - ROPE_DIM)))

    def row(a):
        return a.reshape(1, -1)

    ctx_ckv = jnp.zeros((BATCH, n_mla, SEQ, KV_RANK), F32)
    ctx_krope = jnp.zeros((BATCH, n_mla, SEQ, ROPE_DIM), F32)
    for layer in range(DEPTH):
        if layer % N_MIXERS == 0:
            p = layer // N_MIXERS
            streams = x if isinstance(x, tuple) else (x,)
            x_new, h2, route, counts = _pool_call(layer, streams, mod, row(g_mix[layer]), row(g_ffn[layer]),
                                                  band, wpool[p], row(pool_scale[p]), wr[layer], row(br[layer]))
        else:
            m = layer // N_MIXERS
            qn, qr, ckv, kr, ctx_ckv, ctx_krope = _qkv_call(
                layer, x, mod, row(g_mix[layer]), wdq[m], row(g_q[m]), wdkv[m], row(g_kv[m]), wkr[m], wuqn[m],
                wuqr[m], tabq, tabk, ctx_ckv, ctx_krope)
            o_ctx = _attn_call(layer, qn, qr, ckv, kr, wuk[m], wuv[m])
            o_lat = _attn_call(layer, qn, qr, ckv, kr, wuk[m], wuv[m], cache=(cache_ckv, cache_kr))
            x_new, h2, route, counts = _attn_out_call(layer, o_ctx, o_lat, wo[m], x, mod, row(g_ffn[layer]),
                                                      wr[layer], row(br[layer]))
        pos8, *work = _plan_call(layer, counts, route)
        pos = pos8[:2].reshape(N_SLOTS)
        ys = _gmm_call(layer, h2, work, _invert_call(layer, pos), w_gate, w_up, w_down)
        x = _combine_call(layer, pos, x_new, route, ys, mod, row(g_final), final=(layer == DEPTH - 1))

    y_prompt = x[0].reshape(BATCH, SEQ, D_MODEL)
    y_sample = x[1].reshape(DEC_BATCH, DEC_SEQ, D_MODEL)
    return y_prompt, y_sample, ctx_ckv, ctx_krope
```

```python
import functools

import numpy as np
import jax
import jax.numpy as jnp
from jax import lax
from jax.experimental import pallas as pl
from jax.experimental.pallas import tpu as pltpu

D_MODEL = 2048
BATCH = 32
SEQ = 256
DEPTH = 4
DEC_BATCH = 2
DEC_SEQ = 1024
PAST_LEN = 512
GRID_W = 64
N_MIXERS = 2
POOL_WINDOWS = (2, 4, 8, 16)
N_POOL_GROUPS = 4
POOL_CH = D_MODEL // N_POOL_GROUPS
N_HEADS = 16
Q_RANK = 512
KV_RANK = 512
NOPE_DIM = 128
ROPE_DIM = 64
V_DIM = 128
ROPE_QUARTER = ROPE_DIM // 4
ROPE_THETA = 10000.0
ATTN_SCALE = (NOPE_DIM + ROPE_DIM) ** -0.5
N_EXPERT_GROUPS = 4
EXPERTS_PER_GROUP = 8
N_EXPERTS = N_EXPERT_GROUPS * EXPERTS_PER_GROUP
EXPERT_HIDDEN = 512
N_MOD = 6
EPS = 1e-6

LANES = 128
SUBLANES = 8
TM = 256
N_CTX = BATCH * SEQ
N_LAT = DEC_BATCH * DEC_SEQ
N_TOK = N_CTX + N_LAT
CTX_TILES = N_CTX // TM
LAT_TILES = N_LAT // TM
N_TILES = CTX_TILES + LAT_TILES
LAT_TILES_PER_SEQ = DEC_SEQ // TM
COND_ROWS = SUBLANES
POOL_HALO = max(POOL_WINDOWS) // 2
POOL_K = TM + LANES
ROUTE_LANES = LANES
N_SLOTS = 2 * N_TOK
SORT_TILES = N_SLOTS // TM
MAX_WORK = SORT_TILES + N_EXPERTS - 1
MIB = 1024 * 1024

F32 = jnp.float32
BF16 = jnp.bfloat16


def _mod_row(tile):
    return jnp.where(tile < CTX_TILES, 0, 1 + jnp.maximum(tile - CTX_TILES, 0) // LAT_TILES_PER_SEQ)


def _params(vmem_mib, n_axes=1):
    return pltpu.CompilerParams(
        dimension_semantics=("arbitrary",) * n_axes,
        vmem_limit_bytes=vmem_mib * MIB,
    )


def _norm_mod(x, g, shift, scale):
    y = x * lax.rsqrt(jnp.mean(x * x, axis=-1, keepdims=True) + EPS)
    return (y * g) * (1.0 + scale) + shift


def _route(h_bf, wr_ref, br_ref, seen_ref):
    logits = jnp.dot(h_bf, wr_ref[...], preferred_element_type=F32) + br_ref[...]
    n_rows = logits.shape[0]
    lt = logits.T
    sub = lax.broadcasted_iota(jnp.int32, (EXPERTS_PER_GROUP, n_rows), 0).astype(F32)
    neg = -jnp.inf
    far = float(EXPERTS_PER_GROUP)
    gl = jnp.where(sub < N_EXPERT_GROUPS, lt[N_EXPERTS:N_EXPERTS + EXPERTS_PER_GROUP, :], neg)
    gmax = jnp.max(gl, axis=0, keepdims=True)
    grp = jnp.min(jnp.where(gl == gmax, sub, far), axis=0, keepdims=True)
    p_grp = 1.0 / jnp.sum(jnp.exp(gl - gmax), axis=0, keepdims=True)
    el = lt[0:EXPERTS_PER_GROUP, :]
    for g in range(1, N_EXPERT_GROUPS):
        el = jnp.where(grp == float(g), lt[g * EXPERTS_PER_GROUP:(g + 1) * EXPERTS_PER_GROUP, :], el)
    m1 = jnp.max(el, axis=0, keepdims=True)
    j1 = jnp.min(jnp.where(el == m1, sub, far), axis=0, keepdims=True)
    el2 = jnp.where(sub == j1, neg, el)
    m2 = jnp.max(el2, axis=0, keepdims=True)
    j2 = jnp.min(jnp.where((el2 == m2) & (sub != j1), sub, far), axis=0, keepdims=True)
    e2 = jnp.exp(m2 - m1)
    p1 = p_grp / (1.0 + e2)
    p2 = p1 * e2
    base = grp * float(EXPERTS_PER_GROUP)
    picked = jnp.concatenate(
        [base + j1, base + j2, p1, p2, jnp.zeros((ROUTE_LANES - 4, n_rows), F32)], axis=0).T
    i1, i2, p1, p2 = (picked[:, k:k + 1] for k in range(4))
    lane = lax.broadcasted_iota(jnp.int32, logits.shape, 1).astype(F32)
    earlier = (lax.broadcasted_iota(jnp.int32, (n_rows, n_rows), 0)
               > lax.broadcasted_iota(jnp.int32, (n_rows, n_rows), 1)).astype(BF16)
    hot1 = (lane == i1).astype(F32)
    hot2 = (lane == i2).astype(F32)
    before1 = jnp.dot(earlier, hot1.astype(BF16), preferred_element_type=F32)
    before2 = jnp.dot(earlier, hot2.astype(BF16), preferred_element_type=F32)
    cnt1 = jnp.sum(hot1, axis=0, keepdims=True)
    cnt2 = jnp.sum(hot2, axis=0, keepdims=True)
    seen = seen_ref[...]
    rank1 = jnp.sum(hot1 * (seen + before1), axis=1, keepdims=True)
    rank2 = jnp.sum(hot2 * (seen + cnt1 + before2), axis=1, keepdims=True)
    seen_ref[...] = seen + cnt1 + cnt2

    out = jnp.where(lane == 0, i1, 0.0)
    for k, col in enumerate((i2, p1, p2, rank1, rank2), start=1):
        out = jnp.where(lane == k, col, out)
    return out


def _mixer_tail(x, mix_out, mod_ref, gffn_ref, wr_ref, br_ref, xnew_ref, h2_ref, route_ref, cnt_ref, seen_ref):
    @pl.when(pl.program_id(0) == 0)
    def _():
        seen_ref[...] = jnp.zeros_like(seen_ref)

    x_new = x + mod_ref[2:3, :] * mix_out
    xnew_ref[...] = x_new
    h2 = _norm_mod(x_new, gffn_ref[...], mod_ref[3:4, :], mod_ref[4:5, :])
    h2_ref[...] = h2
    route_ref[...] = _route(h2.astype(BF16), wr_ref, br_ref, seen_ref)
    cnt_ref[...] = seen_ref[...].astype(jnp.int32)


ADA_TN = 2048


def _ada_kernel(cond_ref, w_ref, b_ref, o_ref):
    c = cond_ref[...]
    s = c * jax.nn.sigmoid(c)
    o_ref[...] = jnp.dot(s.astype(BF16), w_ref[...].astype(BF16), preferred_element_type=F32) + b_ref[...]


def _ada_call(cond, w_ada, b_ada):
    nj = (N_MOD * D_MODEL) // ADA_TN
    return pl.pallas_call(
        _ada_kernel,
        out_shape=jax.ShapeDtypeStruct((DEPTH, COND_ROWS, N_MOD * D_MODEL), F32),
        grid=(DEPTH, nj),
        in_specs=[
            pl.BlockSpec((COND_ROWS, D_MODEL), lambda l, j: (0, 0)),
            pl.BlockSpec((None, D_MODEL, ADA_TN), lambda l, j: (l, 0, j)),
            pl.BlockSpec((None, 1, ADA_TN), lambda l, j: (l, 0, j)),
        ],
        out_specs=pl.BlockSpec((None, COND_ROWS, ADA_TN), lambda l, j: (l, 0, j)),
        compiler_params=_params(56, 2),
        name="ada_mod",
    )(cond, w_ada, b_ada.reshape(DEPTH, 1, N_MOD * D_MODEL))


def _pool_band():
    col = np.arange(POOL_K)
    rel = np.where(col < TM, col, np.where(col < TM + POOL_HALO, col - TM - POOL_HALO, col - POOL_HALO))
    ok = col < TM + 2 * POOL_HALO
    row = np.arange(TM)[:, None]
    mats = []
    for w in POOL_WINDOWS:
        mats.append((rel[None, :] >= row - w // 2) & (rel[None, :] < row + (w - w // 2)) & ok[None, :])
    return np.stack(mats).astype(np.float32)


def _pool_kernel(*refs, split_streams):
    if split_streams:
        xc_ref, xl_ref, *refs = refs
    else:
        x_ref, *refs = refs
    (xp_ref, xn_ref, mod_ref, gmix_ref, gffn_ref, band_ref, wpool_ref, pscale_ref, wr_ref, br_ref,
     xnew_ref, h2_ref, route_ref, cnt_ref, seen_ref) = refs
    i = pl.program_id(0)
    lat = i >= CTX_TILES
    tile_pos = jnp.where(lat, jnp.maximum(i - CTX_TILES, 0) % LAT_TILES_PER_SEQ, 0)
    seq_len = jnp.where(lat, DEC_SEQ, SEQ)
    has_prev = tile_pos > 0
    has_next = (tile_pos + 1) * TM < seq_len

    g = gmix_ref[...]
    shift, scale = mod_ref[0:1, :], mod_ref[1:2, :]
    x = jnp.where(lat, xl_ref[...], xc_ref[...]) if split_streams else x_ref[...]
    h = _norm_mod(x, g, shift, scale)
    hp = jnp.where(has_prev, _norm_mod(xp_ref[...], g, shift, scale), 0.0)
    hn = jnp.where(has_next, _norm_mod(xn_ref[...], g, shift, scale), 0.0)
    hall = jnp.concatenate(
        [h, hp, hn, jnp.zeros((POOL_K - TM - 2 * POOL_HALO, D_MODEL), F32)], axis=0)
    hi = hall.astype(BF16)
    lo = (hall - hi.astype(F32)).astype(BF16)

    pos = tile_pos * TM + lax.broadcasted_iota(jnp.int32, (TM, 1), 0)
    outs = []
    for gi, w in enumerate(POOL_WINDOWS):
        sl = slice(gi * POOL_CH, (gi + 1) * POOL_CH)
        band = band_ref[gi]
        wsum = (jnp.dot(band, hi[:, sl], preferred_element_type=F32)
                + jnp.dot(band, lo[:, sl], preferred_element_type=F32))
        cnt = jnp.minimum(pos + (w - w // 2), seq_len) - jnp.maximum(pos - w // 2, 0)
        diff = wsum / cnt.astype(F32) - h[:, sl]
        outs.append(jnp.dot(diff.astype(BF16), wpool_ref[gi], preferred_element_type=F32))
    mix = jnp.concatenate(outs, axis=1) * pscale_ref[...]
    _mixer_tail(x, mix, mod_ref, gffn_ref, wr_ref, br_ref, xnew_ref, h2_ref, route_ref, cnt_ref, seen_ref)


def _tail_out_shapes():
    return (jax.ShapeDtypeStruct((N_TOK, D_MODEL), F32),
            jax.ShapeDtypeStruct((N_TOK, D_MODEL), F32),
            jax.ShapeDtypeStruct((N_TOK, ROUTE_LANES), F32),
            jax.ShapeDtypeStruct((1, ROUTE_LANES), jnp.int32))


def _tail_out_specs():
    return (pl.BlockSpec((TM, D_MODEL), lambda i: (i, 0)),
            pl.BlockSpec((TM, D_MODEL), lambda i: (i, 0)),
            pl.BlockSpec((TM, ROUTE_LANES), lambda i: (i, 0)),
            pl.BlockSpec((1, ROUTE_LANES), lambda i: (0, 0)))


def _tail_scratch():
    return [pltpu.VMEM((1, ROUTE_LANES), F32)]


def _row_spec(width):
    return pl.BlockSpec((1, width), lambda i: (0, 0))


def _mod_spec(layer):
    return pl.BlockSpec((None, None, N_MOD, D_MODEL), lambda i: (layer, _mod_row(i), 0, 0))


def _pool_call(layer, xs, mod, g_mix, g_ffn, band, wpool, pscale, wr, br):
    halo_blocks = TM // POOL_HALO
    split_streams = len(xs) == 2
    tile0 = CTX_TILES if split_streams else 0
    last_halo = xs[-1].shape[0] // POOL_HALO - 1

    def prev_map(i):
        return (jnp.clip((i - tile0) * halo_blocks - 1, 0, last_halo), 0)

    def next_map(i):
        return (jnp.clip((i - tile0 + 1) * halo_blocks, 0, last_halo), 0)

    if split_streams:
        main_specs = [pl.BlockSpec((TM, D_MODEL), lambda i: (jnp.minimum(i, CTX_TILES - 1), 0)),
                      pl.BlockSpec((TM, D_MODEL), lambda i: (jnp.maximum(i - CTX_TILES, 0), 0))]
    else:
        main_specs = [pl.BlockSpec((TM, D_MODEL), lambda i: (i, 0))]
    return pl.pallas_call(
        functools.partial(_pool_kernel, split_streams=split_streams),
        out_shape=_tail_out_shapes(),
        grid=(N_TILES,),
        in_specs=main_specs + [
            pl.BlockSpec((POOL_HALO, D_MODEL), prev_map),
            pl.BlockSpec((POOL_HALO, D_MODEL), next_map),
            _mod_spec(layer),
            _row_spec(D_MODEL),
            _row_spec(D_MODEL),
            pl.BlockSpec((N_POOL_GROUPS, TM, POOL_K), lambda i: (0, 0, 0)),
            pl.BlockSpec((N_POOL_GROUPS, POOL_CH, POOL_CH), lambda i: (0, 0, 0)),
            _row_spec(D_MODEL),
            pl.BlockSpec((D_MODEL, ROUTE_LANES), lambda i: (0, 0)),
            _row_spec(ROUTE_LANES),
        ],
        out_specs=_tail_out_specs(),
        scratch_shapes=_tail_scratch(),
        compiler_params=_params(48),
        name=f"pool_layer{layer}",
    )(*xs, xs[-1], xs[-1], mod, g_mix, g_ffn, band, wpool, pscale, wr, br)


def _qkv_kernel(x_ref, mod_ref, gmix_ref, wdq_ref, gq_ref, wdkv_ref, gkv_ref, wkr_ref, wuqn_ref, wuqr_ref,
                tabq_ref, tabk_ref, _ckv_leaf_in, _kr_leaf_in, qn_ref, qr_ref, ckv_ref, kr_ref, ckv_leaf_ref,
                kr_leaf_ref):
    i = pl.program_id(0)
    h = _norm_mod(x_ref[...], gmix_ref[...], mod_ref[0:1, :], mod_ref[1:2, :]).astype(BF16)

    def rms(v, g):
        return v * lax.rsqrt(jnp.mean(v * v, axis=-1, keepdims=True) + EPS) * g

    cq = rms(jnp.dot(h, wdq_ref[...], preferred_element_type=F32), gq_ref[...]).astype(BF16)
    ckv = rms(jnp.dot(h, wdkv_ref[...], preferred_element_type=F32), gkv_ref[...])
    ckv_ref[...] = ckv
    qn_ref[...] = (jnp.dot(cq, wuqn_ref[...], preferred_element_type=F32) * ATTN_SCALE).astype(BF16)
    qr = jnp.dot(cq, wuqr_ref[...], preferred_element_type=F32) * ATTN_SCALE
    kr = jnp.dot(h, wkr_ref[...], preferred_element_type=F32)
    lane = lax.broadcasted_iota(jnp.int32, kr.shape, 1)

    @pl.when(i < CTX_TILES)
    def _():
        qr_ref[...] = qr.astype(BF16)
        kr_ref[...] = jnp.where(lane < ROPE_DIM, kr, 0.0)
        ckv_leaf_ref[...] = ckv
        kr_leaf_ref[...] = kr[:, :ROPE_DIM]

    @pl.when(i >= CTX_TILES)
    def _():
        tq = qr * tabq_ref[...]
        qr_ref[...] = (tq + pltpu.roll(tq, N_HEADS * LANES - ROPE_DIM, 1)).astype(BF16)
        tk = kr * tabk_ref[...]
        kr_ref[...] = jnp.where(lane < ROPE_DIM, tk + pltpu.roll(tk, ROPE_DIM, 1), 0.0)


def _qkv_call(layer, x, mod, g_mix, wdq, gq, wdkv, gkv, wkr, wuqn, wuqr, tabq, tabk, ckv_leaf, kr_leaf):
    m = layer // N_MIXERS

    def leaf_spec(width):
        return pl.BlockSpec((None, None, SEQ, width), lambda i: (jnp.minimum(i, CTX_TILES - 1), m, 0, 0))

    hd = N_HEADS * LANES

    def lat_tile(i):
        return jnp.maximum(i - CTX_TILES, 0) % LAT_TILES_PER_SEQ

    def full(a):
        return pl.BlockSpec(a.shape, lambda i: (0,) * a.ndim)

    return pl.pallas_call(
        _qkv_kernel,
        out_shape=(jax.ShapeDtypeStruct((N_TOK, hd), BF16),
                   jax.ShapeDtypeStruct((N_TOK, hd), BF16),
                   jax.ShapeDtypeStruct((N_TOK, KV_RANK), F32),
                   jax.ShapeDtypeStruct((N_TOK, LANES), F32),
                   jax.ShapeDtypeStruct(ckv_leaf.shape, F32),
                   jax.ShapeDtypeStruct(kr_leaf.shape, F32)),
        grid=(N_TILES,),
        in_specs=[
            pl.BlockSpec((TM, D_MODEL), lambda i: (i, 0)),
            _mod_spec(layer),
            _row_spec(D_MODEL),
            full(wdq), _row_spec(Q_RANK), full(wdkv), _row_spec(KV_RANK), full(wkr), full(wuqn), full(wuqr),
            pl.BlockSpec((TM, hd), lambda i: (lat_tile(i), 0)),
            pl.BlockSpec((TM, LANES), lambda i: (lat_tile(i), 0)),
            pl.BlockSpec(memory_space=pl.ANY),
            pl.BlockSpec(memory_space=pl.ANY),
        ],
        out_specs=(pl.BlockSpec((TM, hd), lambda i: (i, 0)),
                   pl.BlockSpec((TM, hd), lambda i: (i, 0)),
                   pl.BlockSpec((TM, KV_RANK), lambda i: (i, 0)),
                   pl.BlockSpec((TM, LANES), lambda i: (i, 0)),
                   leaf_spec(KV_RANK),
                   leaf_spec(ROPE_DIM)),
        input_output_aliases={12: 4, 13: 5},
        compiler_params=_params(48),
        name=f"mla_qkv{layer}",
    )(x, mod, g_mix, wdq, gq, wdkv, gkv, wkr, wuqn, wuqr, tabq, tabk, ckv_leaf, kr_leaf)


HEAD_K = 2 * LANES
CTX_SEQS_PER_STEP = 2


def _attn_kernel(*refs, n_new, n_cache, seqs, chunk):
    n_keys = n_new + n_cache
    if n_cache:
        (qn_ref, qr_ref, ckv_ref, kr_ref, cckv_ref, ckr_ref, wuk_ref, wuv_ref, o_ref, kcat, vs) = refs
    else:
        (qn_ref, qr_ref, ckv_ref, kr_ref, wuk_ref, wuv_ref, o_ref, kcat, vs) = refs
        cckv_ref = ckr_ref = None

    @pl.when(pl.program_id(1) == 0)
    def _():
        def expand(src_ref, kr_src_ref, n_rows, base):
            for c in range(n_rows // chunk):
                rows = slice(c * chunk, (c + 1) * chunk)
                dst = slice(base + c * chunk, base + (c + 1) * chunk)
                lat = src_ref[rows, :].astype(BF16)
                kn = jnp.dot(lat, wuk_ref[...], preferred_element_type=F32).astype(BF16)
                vs[dst, :] = jnp.dot(lat, wuv_ref[...], preferred_element_type=F32).astype(BF16)
                krb = kr_src_ref[rows, :].astype(BF16)
                for hh in range(N_HEADS):
                    kcat[dst, hh * HEAD_K:hh * HEAD_K + LANES] = kn[:, hh * LANES:(hh + 1) * LANES]
                    kcat[dst, hh * HEAD_K + LANES:(hh + 1) * HEAD_K] = krb

        expand(ckv_ref, kr_ref, seqs * n_new, 0)
        if n_cache:
            expand(cckv_ref, ckr_ref, n_cache, n_new)

    for sq in range(seqs):
        qrows = slice(sq * TM, (sq + 1) * TM)
        krows = slice(sq * n_keys, (sq + 1) * n_keys)
        for hh in range(N_HEADS):
            hs = slice(hh * LANES, (hh + 1) * LANES)
            q = jnp.concatenate([qn_ref[qrows, hs], qr_ref[qrows, hs]], axis=1)
            s = lax.dot_general(q, kcat[krows, hh * HEAD_K:(hh + 1) * HEAD_K], (((1,), (1,)), ((), ())),
                                preferred_element_type=F32)
            p = jnp.exp(s - jnp.max(s, axis=1, keepdims=True))
            den = jnp.sum(p, axis=1, keepdims=True)
            o = jnp.dot(p.astype(BF16), vs[krows, hs], preferred_element_type=F32) / den
            o_ref[qrows, hs] = o.astype(BF16)


def _attn_call(layer, qn, qr, ckv, kr, wuk, wuv, cache=None):
    hd = N_HEADS * LANES
    if cache is None:
        seqs = CTX_SEQS_PER_STEP
        n_b, n_q, n_new, n_cache, tile0 = BATCH // seqs, SEQ // TM, SEQ, 0, 0
        chunk = seqs * SEQ
    else:
        seqs = 1
        n_b, n_q, n_new, n_cache, tile0 = DEC_BATCH, DEC_SEQ // TM, DEC_SEQ, PAST_LEN, CTX_TILES
        chunk = TM
    q_rows, kv_rows = seqs * TM, seqs * n_new
    q_blk0, seq_blk0 = tile0 * TM // q_rows, tile0 * TM // kv_rows
    m = layer // N_MIXERS

    def q_map(b, j):
        return (q_blk0 + b * n_q + j, 0)

    in_specs = [
        pl.BlockSpec((q_rows, hd), q_map),
        pl.BlockSpec((q_rows, hd), q_map),
        pl.BlockSpec((kv_rows, KV_RANK), lambda b, j: (seq_blk0 + b, 0)),
        pl.BlockSpec((kv_rows, LANES), lambda b, j: (seq_blk0 + b, 0)),
    ]
    args = [qn, qr, ckv, kr]
    if cache is not None:
        in_specs += [
            pl.BlockSpec((None, None, n_cache, KV_RANK), lambda b, j: (b, m, 0, 0)),
            pl.BlockSpec((None, None, n_cache, LANES), lambda b, j: (b, m, 0, 0)),
        ]
        args += list(cache)
    in_specs += [
        pl.BlockSpec((KV_RANK, hd), lambda b, j: (0, 0)),
        pl.BlockSpec((KV_RANK, hd), lambda b, j: (0, 0)),
    ]
    args += [wuk, wuv]
    key_rows = seqs * (n_new + n_cache)
    return pl.pallas_call(
        functools.partial(_attn_kernel, n_new=n_new, n_cache=n_cache, seqs=seqs, chunk=chunk),
        out_shape=jax.ShapeDtypeStruct((n_b * n_q * q_rows, hd), BF16),
        grid=(n_b, n_q),
        in_specs=in_specs,
        out_specs=pl.BlockSpec((q_rows, hd), lambda b, j: (b * n_q + j, 0)),
        scratch_shapes=[pltpu.VMEM((key_rows, N_HEADS * HEAD_K), BF16), pltpu.VMEM((key_rows, hd), BF16)],
        compiler_params=_params(60, 2),
        name=f"mla_attn{layer}_{'lat' if cache is not None else 'ctx'}",
    )(*args)


def _attn_out_kernel(octx_ref, olat_ref, wo_ref, x_ref, mod_ref, gffn_ref, wr_ref, br_ref, xnew_ref, h2_ref,
                     route_ref, cnt_ref, seen_ref):
    o = jnp.where(pl.program_id(0) >= CTX_TILES, olat_ref[...], octx_ref[...])
    mix = jnp.dot(o, wo_ref[...], preferred_element_type=F32)
    _mixer_tail(x_ref[...], mix, mod_ref, gffn_ref, wr_ref, br_ref, xnew_ref, h2_ref, route_ref, cnt_ref,
                seen_ref)


def _attn_out_call(layer, o_ctx, o_lat, wo, x, mod, g_ffn, wr, br):
    hd = N_HEADS * V_DIM
    return pl.pallas_call(
        _attn_out_kernel,
        out_shape=_tail_out_shapes(),
        grid=(N_TILES,),
        in_specs=[
            pl.BlockSpec((TM, hd), lambda i: (jnp.minimum(i, CTX_TILES - 1), 0)),
            pl.BlockSpec((TM, hd), lambda i: (jnp.maximum(i - CTX_TILES, 0), 0)),
            pl.BlockSpec((hd, D_MODEL), lambda i: (0, 0)),
            pl.BlockSpec((TM, D_MODEL), lambda i: (i, 0)),
            _mod_spec(layer),
            _row_spec(D_MODEL),
            pl.BlockSpec((D_MODEL, ROUTE_LANES), lambda i: (0, 0)),
            _row_spec(ROUTE_LANES),
        ],
        out_specs=_tail_out_specs(),
        scratch_shapes=_tail_scratch(),
        compiler_params=_params(48),
        name=f"mla_out{layer}",
    )(o_ctx, o_lat, wo, x, mod, g_ffn, wr, br)


def _plan_kernel(cnt_ref, route_ref, pos_ref, wt_ref, we_ref, wlo_ref, whi_ref, nw_ref, start_ref):
    @pl.when(pl.program_id(0) == 0)
    def _():
        lane = lax.broadcasted_iota(jnp.int32, (1, ROUTE_LANES), 1)
        shift = TM.bit_length() - 1

        def per_expert(e, carry):
            n_work, lo, last_e, starts = carry
            cnt = cnt_ref[0, e]
            hi = lo + cnt
            first_tile = lax.shift_right_logical(lo, shift)
            end_tile = jnp.where(cnt > 0, lax.shift_right_logical(hi + (TM - 1), shift), first_tile)

            def per_tile(t, w):
                wt_ref[w] = t
                we_ref[w] = e
                wlo_ref[w] = lo
                whi_ref[w] = hi
                return w + 1

            n_work = lax.fori_loop(first_tile, end_tile, per_tile, n_work)
            starts = jnp.where(lane == e, lo.astype(F32), starts)
            return n_work, hi, jnp.where(cnt > 0, e, last_e), starts

        n_work, _, last_e, starts = lax.fori_loop(
            0, N_EXPERTS, per_expert,
            (jnp.int32(0), jnp.int32(0), jnp.int32(0), jnp.zeros((1, ROUTE_LANES), F32)))
        nw_ref[0] = n_work
        start_ref[...] = starts

        def fill(w, carry):
            wt_ref[w] = SORT_TILES - 1
            we_ref[w] = last_e
            wlo_ref[w] = 0
            whi_ref[w] = 0
            return carry

        lax.fori_loop(n_work, MAX_WORK, fill, 0)

    r = route_ref[...]
    lane = lax.broadcasted_iota(jnp.int32, r.shape, 1).astype(F32)
    starts = start_ref[...]
    pos1 = jnp.sum(jnp.where(lane == r[:, 0:1], starts, 0.0), axis=1, keepdims=True) + r[:, 4:5]
    pos2 = jnp.sum(jnp.where(lane == r[:, 1:2], starts, 0.0), axis=1, keepdims=True) + r[:, 5:6]
    pos = jnp.where(lane == 0.0, pos1, jnp.where(lane == 1.0, pos2, 0.0))
    pos_ref[...] = pos.T[0:SUBLANES, :].astype(jnp.int32)


PLAN_ROWS = 4 * TM


def _plan_call(layer, counts, route):
    smem = pl.BlockSpec(memory_space=pltpu.SMEM)
    work = jax.ShapeDtypeStruct((MAX_WORK,), jnp.int32)
    return pl.pallas_call(
        _plan_kernel,
        out_shape=(jax.ShapeDtypeStruct((SUBLANES, N_TOK), jnp.int32), work, work, work, work,
                   jax.ShapeDtypeStruct((1,), jnp.int32)),
        grid=(N_TOK // PLAN_ROWS,),
        in_specs=[smem, pl.BlockSpec((PLAN_ROWS, ROUTE_LANES), lambda i: (i, 0))],
        out_specs=(pl.BlockSpec((SUBLANES, PLAN_ROWS), lambda i: (0, i)), smem, smem, smem, smem, smem),
        scratch_shapes=[pltpu.VMEM((1, ROUTE_LANES), F32)],
        compiler_params=_params(16),
        name=f"moe_plan{layer}",
    )(counts, route)


INVERT_UNROLL = 16


def _invert_kernel(pos_ref, tok_ref):
    for k in range(2):
        def body(j, carry, k=k):
            t0 = j * INVERT_UNROLL
            where = [pos_ref[k * N_TOK + t0 + u] for u in range(INVERT_UNROLL)]
            for u in range(INVERT_UNROLL):
                tok_ref[where[u]] = t0 + u
            return carry

        lax.fori_loop(0, N_TOK // INVERT_UNROLL, body, 0)


def _invert_call(layer, pos):
    smem = pl.BlockSpec(memory_space=pltpu.SMEM)
    return pl.pallas_call(
        _invert_kernel,
        out_shape=jax.ShapeDtypeStruct((N_SLOTS,), jnp.int32),
        in_specs=[smem],
        out_specs=smem,
        name=f"moe_invert{layer}",
    )(pos)


def _gmm_kernel(wt_ref, we_ref, wlo_ref, whi_ref, nw_ref, tok_ref, h2_hbm, wg_hbm, wu_hbm, wd_hbm, ys_ref,
                xbuf, wg_f, wu_f, wd_f, wg_s, wu_s, wd_s, sem_x, sem, nexp_ref, *, layer):
    w = pl.program_id(0)
    n_work = nw_ref[0]
    prev = jnp.maximum(w - 1, 0)
    tile = wt_ref[w]
    new_expert = (w == 0) | (we_ref[w] != we_ref[prev])
    new_tile = (w == 0) | (tile != wt_ref[prev])

    def weight_copies(expert, slot):
        return [pltpu.make_async_copy(hbm.at[layer, expert], buf.at[slot], sem.at[slot, k])
                for k, (hbm, buf) in enumerate(((wg_hbm, wg_f), (wu_hbm, wu_f), (wd_hbm, wd_f)))]

    def gather(t):
        slot = t % 2
        for r in range(TM):
            pltpu.make_async_copy(h2_hbm.at[pl.ds(tok_ref[t * TM + r], 1)], xbuf.at[slot, pl.ds(r, 1)],
                                  sem_x.at[slot]).start()

    @pl.when(w == 0)
    def _():
        nexp_ref[0] = 0
        for cp in weight_copies(we_ref[0], 0):
            cp.start()
        gather(tile)

    @pl.when((w < n_work) & new_tile & (tile + 1 < SORT_TILES))
    def _():
        gather(tile + 1)

    @pl.when((w < n_work) & new_expert)
    def _():
        slot = nexp_ref[0] % 2
        nexp_ref[0] += 1
        nxt = lax.while_loop(
            lambda k: (k < n_work) & (we_ref[jnp.minimum(k, MAX_WORK - 1)] == we_ref[w]),
            lambda k: k + 1, w + 1)

        @pl.when(nxt < n_work)
        def _():
            for cp in weight_copies(we_ref[jnp.minimum(nxt, MAX_WORK - 1)], 1 - slot):
                cp.start()

        for cp in weight_copies(we_ref[w], slot):
            cp.wait()
        wg_s[...] = wg_f[slot].astype(BF16)
        wu_s[...] = wu_f[slot].astype(BF16)
        wd_s[...] = wd_f[slot].astype(BF16)

    @pl.when((w < n_work) & new_tile)
    def _():
        slot = tile % 2
        pltpu.make_async_copy(h2_hbm.at[pl.ds(0, TM)], xbuf.at[slot], sem_x.at[slot]).wait()

    @pl.when(w < n_work)
    def _():
        x = xbuf[tile % 2].astype(BF16)
        gate = jnp.dot(x, wg_s[...], preferred_element_type=F32)
        up = jnp.dot(x, wu_s[...], preferred_element_type=F32)
        row = tile * TM + lax.broadcasted_iota(jnp.int32, (TM, 1), 0)
        mine = (row >= wlo_ref[w]) & (row < whi_ref[w])
        hid = jnp.where(mine, (gate * jax.nn.sigmoid(gate)) * up, 0.0)
        y = jnp.dot(hid.astype(BF16), wd_s[...], preferred_element_type=F32)

        @pl.when(new_tile)
        def _():
            ys_ref[...] = y

        @pl.when(jnp.logical_not(new_tile))
        def _():
            ys_ref[...] += y


def _gmm_call(layer, h2, work, slot_tok, w_gate, w_up, w_down):
    up_shape, down_shape = (D_MODEL, EXPERT_HIDDEN), (EXPERT_HIDDEN, D_MODEL)
    hbm = pl.BlockSpec(memory_space=pl.ANY)
    grid_spec = pltpu.PrefetchScalarGridSpec(
        num_scalar_prefetch=6,
        grid=(MAX_WORK,),
        in_specs=[hbm, hbm, hbm, hbm],
        out_specs=pl.BlockSpec((TM, D_MODEL), lambda w, wt, we, wlo, whi, nw, tok: (wt[w], 0)),
        scratch_shapes=[
            pltpu.VMEM((2, TM, D_MODEL), F32),
            pltpu.VMEM((2,) + up_shape, F32),
            pltpu.VMEM((2,) + up_shape, F32),
            pltpu.VMEM((2,) + down_shape, F32),
            pltpu.VMEM(up_shape, BF16),
            pltpu.VMEM(up_shape, BF16),
            pltpu.VMEM(down_shape, BF16),
            pltpu.SemaphoreType.DMA((2,)),
            pltpu.SemaphoreType.DMA((2, 3)),
            pltpu.SMEM((1,), jnp.int32),
        ],
    )
    return pl.pallas_call(
        functools.partial(_gmm_kernel, layer=layer),
        out_shape=jax.ShapeDtypeStruct((N_SLOTS, D_MODEL), F32),
        grid_spec=grid_spec,
        compiler_params=_params(48),
        name=f"moe_gmm{layer}",
    )(*work, slot_tok, h2, w_gate, w_up, w_down)


def _combine_kernel(pos_ref, x_ref, route_ref, mod_ref, gfin_ref, ys_hbm, *rest, final):
    if final:
        octx_ref, olat_ref, buf1, buf2, sem1, sem2 = rest
    else:
        o_ref, buf1, buf2, sem1, sem2 = rest
    i = pl.program_id(0)

    def fetch(tile, slot):
        def issue(r, carry):
            t = tile * TM + r
            pltpu.make_async_copy(ys_hbm.at[pl.ds(pos_ref[t], 1)], buf1.at[slot, pl.ds(r, 1)],
                                  sem1.at[slot]).start(priority=0)
            pltpu.make_async_copy(ys_hbm.at[pl.ds(pos_ref[N_TOK + t], 1)], buf2.at[slot, pl.ds(r, 1)],
                                  sem2.at[slot]).start(priority=1)
            return carry

        for r in range(TM):
            issue(r, 0)

    def wait_rows(slot):
        pltpu.make_async_copy(ys_hbm.at[pl.ds(0, TM)], buf1.at[slot], sem1.at[slot]).wait()
        pltpu.make_async_copy(ys_hbm.at[pl.ds(0, TM)], buf2.at[slot], sem2.at[slot]).wait()

    @pl.when(i == 0)
    def _():
        fetch(i, 0)

    slot = i % 2

    @pl.when(i + 1 < N_TILES)
    def _():
        fetch(i + 1, 1 - slot)

    wait_rows(slot)
    r = route_ref[...]
    moe = r[:, 2:3] * buf1[slot] + r[:, 3:4] * buf2[slot]
    x = x_ref[...] + mod_ref[5:6, :] * moe
    if final:
        y = x * lax.rsqrt(jnp.mean(x * x, axis=-1, keepdims=True) + EPS) * gfin_ref[...]

        @pl.when(i < CTX_TILES)
        def _():
            octx_ref[...] = y

        @pl.when(i >= CTX_TILES)
        def _():
            olat_ref[...] = y
    else:
        o_ref[...] = x


def _combine_call(layer, pos, x_new, route, ys, mod, g_final, final):
    def tok_spec(width):
        return pl.BlockSpec((TM, width), lambda i, pos: (i, 0))

    if final:
        out_shape = (jax.ShapeDtypeStruct((N_CTX, D_MODEL), F32), jax.ShapeDtypeStruct((N_LAT, D_MODEL), F32))
        out_specs = (pl.BlockSpec((TM, D_MODEL), lambda i, pos: (jnp.minimum(i, CTX_TILES - 1), 0)),
                     pl.BlockSpec((TM, D_MODEL), lambda i, pos: (jnp.maximum(i - CTX_TILES, 0), 0)))
    else:
        out_shape = jax.ShapeDtypeStruct((N_TOK, D_MODEL), F32)
        out_specs = tok_spec(D_MODEL)

    grid_spec = pltpu.PrefetchScalarGridSpec(
        num_scalar_prefetch=1,
        grid=(N_TILES,),
        in_specs=[
            tok_spec(D_MODEL),
            tok_spec(ROUTE_LANES),
            pl.BlockSpec((None, None, N_MOD, D_MODEL), lambda i, pos: (layer, _mod_row(i), 0, 0)),
            pl.BlockSpec((1, D_MODEL), lambda i, pos: (0, 0)),
            pl.BlockSpec(memory_space=pl.ANY),
        ],
        out_specs=out_specs,
        scratch_shapes=[
            pltpu.VMEM((2, TM, D_MODEL), F32),
            pltpu.VMEM((2, TM, D_MODEL), F32),
            pltpu.SemaphoreType.DMA((2,)),
            pltpu.SemaphoreType.DMA((2,)),
        ],
    )
    return pl.pallas_call(
        functools.partial(_combine_kernel, final=final),
        out_shape=out_shape,
        grid_spec=grid_spec,
        compiler_params=_params(40),
        name=f"moe_combine{layer}",
    )(pos, x_new, route, mod, g_final, ys)


def _rope_tables():
    rows = DEC_SEQ // GRID_W
    row_ids = np.repeat(np.arange(rows), GRID_W).astype(np.float32)
    col_ids = np.tile(np.arange(GRID_W), rows).astype(np.float32)
    freqs = (np.float32(ROPE_THETA) ** (-np.arange(ROPE_QUARTER, dtype=np.float32) / ROPE_QUARTER))
    ar = (row_ids[:, None] * freqs[None, :]).astype(np.float32).astype(np.float64)
    ac = (col_ids[:, None] * freqs[None, :]).astype(np.float32).astype(np.float64)
    cos = np.concatenate([np.cos(ar), np.cos(ar), np.cos(ac), np.cos(ac)], axis=1)
    sin = np.concatenate([-np.sin(ar), np.sin(ar), -np.sin(ac), np.sin(ac)], axis=1)
    return np.concatenate([cos, sin], axis=1).astype(np.float32)


def _swap_partners(w):
    q = ROPE_QUARTER
    return jnp.concatenate([w[..., q:2 * q], w[..., 0:q], w[..., 3 * q:4 * q], w[..., 2 * q:3 * q]], axis=-1)


def kernel(x_prompt, x_sample, cache_ckv, cache_krope, c, c_ctx, g_mix, g_ffn, w_ada, b_ada, w_pool, pool_scale,
           w_dq, g_q, w_uq, w_dkv, g_kv, w_uk, w_uv, w_o, w_router_grp, b_router_grp, w_router_exp,
           b_router_exp, w_gate, w_up, w_down, g_final):
    n_mla = DEPTH // N_MIXERS
    x = (x_prompt.reshape(N_CTX, D_MODEL), x_sample.reshape(N_LAT, D_MODEL))

    cond = jnp.concatenate([c_ctx[None, :], c, jnp.zeros((COND_ROWS - 1 - DEC_BATCH, D_MODEL), F32)], axis=0)
    mod = _ada_call(cond, w_ada, b_ada).reshape(DEPTH, COND_ROWS, N_MOD, D_MODEL)

    wr = jnp.concatenate([w_router_exp, w_router_grp,
                          jnp.zeros((DEPTH, D_MODEL, ROUTE_LANES - N_EXPERTS - N_EXPERT_GROUPS), F32)],
                         axis=2).astype(BF16)
    br = jnp.concatenate([b_router_exp, b_router_grp,
                          jnp.zeros((DEPTH, ROUTE_LANES - N_EXPERTS - N_EXPERT_GROUPS), F32)], axis=1)
    band = jnp.asarray(_pool_band(), dtype=BF16)
    wpool = w_pool.astype(BF16)
    wdq = w_dq.astype(BF16)
    wdkv = w_dkv[:, :, :KV_RANK].astype(BF16)
    w_kr = w_dkv[:, :, KV_RANK:]
    wkr = jnp.concatenate([w_kr, _swap_partners(w_kr)], axis=-1).astype(BF16)
    wuqn = w_uq[..., :NOPE_DIM].reshape(n_mla, Q_RANK, N_HEADS * NOPE_DIM).astype(BF16)
    w_qr = w_uq[..., NOPE_DIM:]
    wuqr = jnp.concatenate([w_qr, _swap_partners(w_qr)], axis=-1).reshape(n_mla, Q_RANK, N_HEADS * LANES)
    wuqr = wuqr.astype(BF16)
    wuk = w_uk.reshape(n_mla, KV_RANK, N_HEADS * NOPE_DIM).astype(BF16)
    wuv = w_uv.reshape(n_mla, KV_RANK, N_HEADS * V_DIM).astype(BF16)
    wo = w_o.astype(BF16)
    rope_tab = _rope_tables()
    tabk = jnp.asarray(rope_tab)
    tabq = jnp.asarray(np.tile(rope_tab, (1, N_HEADS)))
    cache_kr = jnp.pad(cache_krope, ((0, 0), (0, 0), (0, 0), (0, LANES - ROPE_DIM)))

    def row(a):
        return a.reshape(1, -1)

    ctx_ckv = jnp.zeros((BATCH, n_mla, SEQ, KV_RANK), F32)
    ctx_krope = jnp.zeros((BATCH, n_mla, SEQ, ROPE_DIM), F32)
    for layer in range(DEPTH):
        if layer % N_MIXERS == 0:
            p = layer // N_MIXERS
            streams = x if isinstance(x, tuple) else (x,)
            x_new, h2, route, counts = _pool_call(layer, streams, mod, row(g_mix[layer]), row(g_ffn[layer]),
                                                  band, wpool[p], row(pool_scale[p]), wr[layer], row(br[layer]))
        else:
            m = layer // N_MIXERS
            qn, qr, ckv, kr, ctx_ckv, ctx_krope = _qkv_call(
                layer, x, mod, row(g_mix[layer]), wdq[m], row(g_q[m]), wdkv[m], row(g_kv[m]), wkr[m], wuqn[m],
                wuqr[m], tabq, tabk, ctx_ckv, ctx_krope)
            o_ctx = _attn_call(layer, qn, qr, ckv, kr, wuk[m], wuv[m])
            o_lat = _attn_call(layer, qn, qr, ckv, kr, wuk[m], wuv[m], cache=(cache_ckv, cache_kr))
            x_new, h2, route, counts = _attn_out_call(layer, o_ctx, o_lat, wo[m], x, mod, row(g_ffn[layer]),
                                                      wr[layer], row(br[layer]))
        pos8, *work = _plan_call(layer, counts, route)
        pos = pos8[:2].reshape(N_SLOTS)
        ys = _gmm_call(layer, h2, work, _invert_call(layer, pos), w_gate, w_up, w_down)
        x = _combine_call(layer, pos, x_new, route, ys, mod, row(g_final), final=(layer == DEPTH - 1))

    y_prompt = x[0].reshape(BATCH, SEQ, D_MODEL)
    y_sample = x[1].reshape(DEC_BATCH, DEC_SEQ, D_MODEL)
    return y_prompt, y_sample, ctx_ckv, ctx_krope
```

```python
import functools

import numpy as np
import jax
import jax.numpy as jnp
from jax import lax
from jax.experimental import pallas as pl
from jax.experimental.pallas import tpu as pltpu

D_MODEL = 2048
BATCH = 32
SEQ = 256
DEPTH = 4
DEC_BATCH = 2
DEC_SEQ = 1024
PAST_LEN = 512
GRID_W = 64
N_MIXERS = 2
POOL_WINDOWS = (2, 4, 8, 16)
N_POOL_GROUPS = 4
POOL_CH = D_MODEL // N_POOL_GROUPS
N_HEADS = 16
Q_RANK = 512
KV_RANK = 512
NOPE_DIM = 128
ROPE_DIM = 64
V_DIM = 128
ROPE_QUARTER = ROPE_DIM // 4
ROPE_THETA = 10000.0
ATTN_SCALE = (NOPE_DIM + ROPE_DIM) ** -0.5
N_EXPERT_GROUPS = 4
EXPERTS_PER_GROUP = 8
N_EXPERTS = N_EXPERT_GROUPS * EXPERTS_PER_GROUP
EXPERT_HIDDEN = 512
N_MOD = 6
EPS = 1e-6

LANES = 128
SUBLANES = 8
TM = 256
N_CTX = BATCH * SEQ
N_LAT = DEC_BATCH * DEC_SEQ
N_TOK = N_CTX + N_LAT
CTX_TILES = N_CTX // TM
LAT_TILES = N_LAT // TM
N_TILES = CTX_TILES + LAT_TILES
LAT_TILES_PER_SEQ = DEC_SEQ // TM
COND_ROWS = SUBLANES
POOL_HALO = max(POOL_WINDOWS) // 2
POOL_K = TM + LANES
ROUTE_LANES = LANES
N_SLOTS = 2 * N_TOK
SORT_TILES = N_SLOTS // TM
MAX_WORK = SORT_TILES + N_EXPERTS - 1
MIB = 1024 * 1024

F32 = jnp.float32
BF16 = jnp.bfloat16


def _mod_row(tile):
    return jnp.where(tile < CTX_TILES, 0, 1 + jnp.maximum(tile - CTX_TILES, 0) // LAT_TILES_PER_SEQ)


def _params(vmem_mib, n_axes=1):
    return pltpu.CompilerParams(
        dimension_semantics=("arbitrary",) * n_axes,
        vmem_limit_bytes=vmem_mib * MIB,
    )


def _norm_mod(x, g, shift, scale):
    y = x * lax.rsqrt(jnp.mean(x * x, axis=-1, keepdims=True) + EPS)
    return (y * g) * (1.0 + scale) + shift


def _route(h_bf, wr_ref, br_ref, seen_ref):
    logits = jnp.dot(h_bf, wr_ref[...], preferred_element_type=F32) + br_ref[...]
    n_rows = logits.shape[0]
    lt = logits.T
    sub = lax.broadcasted_iota(jnp.int32, (EXPERTS_PER_GROUP, n_rows), 0).astype(F32)
    neg = -jnp.inf
    far = float(EXPERTS_PER_GROUP)
    gl = jnp.where(sub < N_EXPERT_GROUPS, lt[N_EXPERTS:N_EXPERTS + EXPERTS_PER_GROUP, :], neg)
    gmax = jnp.max(gl, axis=0, keepdims=True)
    grp = jnp.min(jnp.where(gl == gmax, sub, far), axis=0, keepdims=True)
    p_grp = 1.0 / jnp.sum(jnp.exp(gl - gmax), axis=0, keepdims=True)
    el = lt[0:EXPERTS_PER_GROUP, :]
    for g in range(1, N_EXPERT_GROUPS):
        el = jnp.where(grp == float(g), lt[g * EXPERTS_PER_GROUP:(g + 1) * EXPERTS_PER_GROUP, :], el)
    m1 = jnp.max(el, axis=0, keepdims=True)
    j1 = jnp.min(jnp.where(el == m1, sub, far), axis=0, keepdims=True)
    el2 = jnp.where(sub == j1, neg, el)
    m2 = jnp.max(el2, axis=0, keepdims=True)
    j2 = jnp.min(jnp.where((el2 == m2) & (sub != j1), sub, far), axis=0, keepdims=True)
    e2 = jnp.exp(m2 - m1)
    p1 = p_grp / (1.0 + e2)
    p2 = p1 * e2
    base = grp * float(EXPERTS_PER_GROUP)
    picked = jnp.concatenate(
        [base + j1, base + j2, p1, p2, jnp.zeros((ROUTE_LANES - 4, n_rows), F32)], axis=0).T
    i1, i2, p1, p2 = (picked[:, k:k + 1] for k in range(4))
    lane = lax.broadcasted_iota(jnp.int32, logits.shape, 1).astype(F32)
    earlier = (lax.broadcasted_iota(jnp.int32, (n_rows, n_rows), 0)
               > lax.broadcasted_iota(jnp.int32, (n_rows, n_rows), 1)).astype(BF16)
    hot1 = (lane == i1).astype(F32)
    hot2 = (lane == i2).astype(F32)
    before1 = jnp.dot(earlier, hot1.astype(BF16), preferred_element_type=F32)
    before2 = jnp.dot(earlier, hot2.astype(BF16), preferred_element_type=F32)
    cnt1 = jnp.sum(hot1, axis=0, keepdims=True)
    cnt2 = jnp.sum(hot2, axis=0, keepdims=True)
    seen = seen_ref[...]
    rank1 = jnp.sum(hot1 * (seen + before1), axis=1, keepdims=True)
    rank2 = jnp.sum(hot2 * (seen + cnt1 + before2), axis=1, keepdims=True)
    seen_ref[...] = seen + cnt1 + cnt2

    out = jnp.where(lane == 0, i1, 0.0)
    for k, col in enumerate((i2, p1, p2, rank1, rank2), start=1):
        out = jnp.where(lane == k, col, out)
    return out


def _mixer_tail(x, mix_out, mod_ref, gffn_ref, wr_ref, br_ref, xnew_ref, h2_ref, route_ref, cnt_ref, seen_ref):
    @pl.when(pl.program_id(0) == 0)
    def _():
        seen_ref[...] = jnp.zeros_like(seen_ref)

    x_new = x + mod_ref[2:3, :] * mix_out
    xnew_ref[...] = x_new
    h2 = _norm_mod(x_new, gffn_ref[...], mod_ref[3:4, :], mod_ref[4:5, :])
    h2_ref[...] = h2
    route_ref[...] = _route(h2.astype(BF16), wr_ref, br_ref, seen_ref)
    cnt_ref[...] = seen_ref[...].astype(jnp.int32)


ADA_TN = 2048


def _ada_kernel(cond_ref, w_ref, b_ref, o_ref):
    c = cond_ref[...]
    s = c * jax.nn.sigmoid(c)
    o_ref[...] = jnp.dot(s.astype(BF16), w_ref[...].astype(BF16), preferred_element_type=F32) + b_ref[...]


def _ada_call(cond, w_ada, b_ada):
    nj = (N_MOD * D_MODEL) // ADA_TN
    return pl.pallas_call(
        _ada_kernel,
        out_shape=jax.ShapeDtypeStruct((DEPTH, COND_ROWS, N_MOD * D_MODEL), F32),
        grid=(DEPTH, nj),
        in_specs=[
            pl.BlockSpec((COND_ROWS, D_MODEL), lambda l, j: (0, 0)),
            pl.BlockSpec((None, D_MODEL, ADA_TN), lambda l, j: (l, 0, j)),
            pl.BlockSpec((None, 1, ADA_TN), lambda l, j: (l, 0, j)),
        ],
        out_specs=pl.BlockSpec((None, COND_ROWS, ADA_TN), lambda l, j: (l, 0, j)),
        compiler_params=_params(56, 2),
        name="ada_mod",
    )(cond, w_ada, b_ada.reshape(DEPTH, 1, N_MOD * D_MODEL))


def _pool_band():
    col = np.arange(POOL_K)
    rel = np.where(col < TM, col, np.where(col < TM + POOL_HALO, col - TM - POOL_HALO, col - POOL_HALO))
    ok = col < TM + 2 * POOL_HALO
    row = np.arange(TM)[:, None]
    mats = []
    for w in POOL_WINDOWS:
        mats.append((rel[None, :] >= row - w // 2) & (rel[None, :] < row + (w - w // 2)) & ok[None, :])
    return np.stack(mats).astype(np.float32)


def _pool_kernel(*refs, split_streams):
    if split_streams:
        xc_ref, xl_ref, *refs = refs
    else:
        x_ref, *refs = refs
    (xp_ref, xn_ref, mod_ref, gmix_ref, gffn_ref, band_ref, wpool_ref, pscale_ref, wr_ref, br_ref,
     xnew_ref, h2_ref, route_ref, cnt_ref, seen_ref) = refs
    i = pl.program_id(0)
    lat = i >= CTX_TILES
    tile_pos = jnp.where(lat, jnp.maximum(i - CTX_TILES, 0) % LAT_TILES_PER_SEQ, 0)
    seq_len = jnp.where(lat, DEC_SEQ, SEQ)
    has_prev = tile_pos > 0
    has_next = (tile_pos + 1) * TM < seq_len

    g = gmix_ref[...]
    shift, scale = mod_ref[0:1, :], mod_ref[1:2, :]
    x = jnp.where(lat, xl_ref[...], xc_ref[...]) if split_streams else x_ref[...]
    h = _norm_mod(x, g, shift, scale)
    hp = jnp.where(has_prev, _norm_mod(xp_ref[...], g, shift, scale), 0.0)
    hn = jnp.where(has_next, _norm_mod(xn_ref[...], g, shift, scale), 0.0)
    hall = jnp.concatenate(
        [h, hp, hn, jnp.zeros((POOL_K - TM - 2 * POOL_HALO, D_MODEL), F32)], axis=0)
    hi = hall.astype(BF16)

    pos = tile_pos * TM + lax.broadcasted_iota(jnp.int32, (TM, 1), 0)
    outs = []
    for gi, w in enumerate(POOL_WINDOWS):
        sl = slice(gi * POOL_CH, (gi + 1) * POOL_CH)
        band = band_ref[gi]
        wsum = jnp.dot(band, hi[:, sl], preferred_element_type=F32)
        cnt = jnp.minimum(pos + (w - w // 2), seq_len) - jnp.maximum(pos - w // 2, 0)
        diff = wsum / cnt.astype(F32) - h[:, sl]
        outs.append(jnp.dot(diff.astype(BF16), wpool_ref[gi], preferred_element_type=F32))
    mix = jnp.concatenate(outs, axis=1) * pscale_ref[...]
    _mixer_tail(x, mix, mod_ref, gffn_ref, wr_ref, br_ref, xnew_ref, h2_ref, route_ref, cnt_ref, seen_ref)


def _tail_out_shapes():
    return (jax.ShapeDtypeStruct((N_TOK, D_MODEL), F32),
            jax.ShapeDtypeStruct((N_TOK, D_MODEL), F32),
            jax.ShapeDtypeStruct((N_TOK, ROUTE_LANES), F32),
            jax.ShapeDtypeStruct((1, ROUTE_LANES), jnp.int32))


def _tail_out_specs():
    return (pl.BlockSpec((TM, D_MODEL), lambda i: (i, 0)),
            pl.BlockSpec((TM, D_MODEL), lambda i: (i, 0)),
            pl.BlockSpec((TM, ROUTE_LANES), lambda i: (i, 0)),
            pl.BlockSpec((1, ROUTE_LANES), lambda i: (0, 0)))


def _tail_scratch():
    return [pltpu.VMEM((1, ROUTE_LANES), F32)]


def _row_spec(width):
    return pl.BlockSpec((1, width), lambda i: (0, 0))


def _mod_spec(layer):
    return pl.BlockSpec((None, None, N_MOD, D_MODEL), lambda i: (layer, _mod_row(i), 0, 0))


def _pool_call(layer, xs, mod, g_mix, g_ffn, band, wpool, pscale, wr, br):
    halo_blocks = TM // POOL_HALO
    split_streams = len(xs) == 2
    tile0 = CTX_TILES if split_streams else 0
    last_halo = xs[-1].shape[0] // POOL_HALO - 1

    def prev_map(i):
        return (jnp.clip((i - tile0) * halo_blocks - 1, 0, last_halo), 0)

    def next_map(i):
        return (jnp.clip((i - tile0 + 1) * halo_blocks, 0, last_halo), 0)

    if split_streams:
        main_specs = [pl.BlockSpec((TM, D_MODEL), lambda i: (jnp.minimum(i, CTX_TILES - 1), 0)),
                      pl.BlockSpec((TM, D_MODEL), lambda i: (jnp.maximum(i - CTX_TILES, 0), 0))]
    else:
        main_specs = [pl.BlockSpec((TM, D_MODEL), lambda i: (i, 0))]
    return pl.pallas_call(
        functools.partial(_pool_kernel, split_streams=split_streams),
        out_shape=_tail_out_shapes(),
        grid=(N_TILES,),
        in_specs=main_specs + [
            pl.BlockSpec((POOL_HALO, D_MODEL), prev_map),
            pl.BlockSpec((POOL_HALO, D_MODEL), next_map),
            _mod_spec(layer),
            _row_spec(D_MODEL),
            _row_spec(D_MODEL),
            pl.BlockSpec((N_POOL_GROUPS, TM, POOL_K), lambda i: (0, 0, 0)),
            pl.BlockSpec((N_POOL_GROUPS, POOL_CH, POOL_CH), lambda i: (0, 0, 0)),
            _row_spec(D_MODEL),
            pl.BlockSpec((D_MODEL, ROUTE_LANES), lambda i: (0, 0)),
            _row_spec(ROUTE_LANES),
        ],
        out_specs=_tail_out_specs(),
        scratch_shapes=_tail_scratch(),
        compiler_params=_params(48),
        name=f"pool_layer{layer}",
    )(*xs, xs[-1], xs[-1], mod, g_mix, g_ffn, band, wpool, pscale, wr, br)


def _qkv_kernel(x_ref, mod_ref, gmix_ref, wdq_ref, gq_ref, wdkv_ref, gkv_ref, wkr_ref, wuqn_ref, wuqr_ref,
                tabq_ref, tabk_ref, _ckv_leaf_in, _kr_leaf_in, qn_ref, qr_ref, ckv_ref, kr_ref, ckv_leaf_ref,
                kr_leaf_ref):
    i = pl.program_id(0)
    h = _norm_mod(x_ref[...], gmix_ref[...], mod_ref[0:1, :], mod_ref[1:2, :]).astype(BF16)

    def rms(v, g):
        return v * lax.rsqrt(jnp.mean(v * v, axis=-1, keepdims=True) + EPS) * g

    cq = rms(jnp.dot(h, wdq_ref[...], preferred_element_type=F32), gq_ref[...]).astype(BF16)
    ckv = rms(jnp.dot(h, wdkv_ref[...], preferred_element_type=F32), gkv_ref[...])
    ckv_ref[...] = ckv
    qn_ref[...] = (jnp.dot(cq, wuqn_ref[...], preferred_element_type=F32) * ATTN_SCALE).astype(BF16)
    qr = jnp.dot(cq, wuqr_ref[...], preferred_element_type=F32) * ATTN_SCALE
    kr = jnp.dot(h, wkr_ref[...], preferred_element_type=F32)
    lane = lax.broadcasted_iota(jnp.int32, kr.shape, 1)

    @pl.when(i < CTX_TILES)
    def _():
        qr_ref[...] = qr.astype(BF16)
        kr_ref[...] = jnp.where(lane < ROPE_DIM, kr, 0.0)
        ckv_leaf_ref[...] = ckv
        kr_leaf_ref[...] = kr[:, :ROPE_DIM]

    @pl.when(i >= CTX_TILES)
    def _():
        tq = qr * tabq_ref[...]
        qr_ref[...] = (tq + pltpu.roll(tq, N_HEADS * LANES - ROPE_DIM, 1)).astype(BF16)
        tk = kr * tabk_ref[...]
        kr_ref[...] = jnp.where(lane < ROPE_DIM, tk + pltpu.roll(tk, ROPE_DIM, 1), 0.0)


def _qkv_call(layer, x, mod, g_mix, wdq, gq, wdkv, gkv, wkr, wuqn, wuqr, tabq, tabk, ckv_leaf, kr_leaf):
    m = layer // N_MIXERS

    def leaf_spec(width):
        return pl.BlockSpec((None, None, SEQ, width), lambda i: (jnp.minimum(i, CTX_TILES - 1), m, 0, 0))

    hd = N_HEADS * LANES

    def lat_tile(i):
        return jnp.maximum(i - CTX_TILES, 0) % LAT_TILES_PER_SEQ

    def full(a):
        return pl.BlockSpec(a.shape, lambda i: (0,) * a.ndim)

    return pl.pallas_call(
        _qkv_kernel,
        out_shape=(jax.ShapeDtypeStruct((N_TOK, hd), BF16),
                   jax.ShapeDtypeStruct((N_TOK, hd), BF16),
                   jax.ShapeDtypeStruct((N_TOK, KV_RANK), F32),
                   jax.ShapeDtypeStruct((N_TOK, LANES), F32),
                   jax.ShapeDtypeStruct(ckv_leaf.shape, F32),
                   jax.ShapeDtypeStruct(kr_leaf.shape, F32)),
        grid=(N_TILES,),
        in_specs=[
            pl.BlockSpec((TM, D_MODEL), lambda i: (i, 0)),
            _mod_spec(layer),
            _row_spec(D_MODEL),
            full(wdq), _row_spec(Q_RANK), full(wdkv), _row_spec(KV_RANK), full(wkr), full(wuqn), full(wuqr),
            pl.BlockSpec((TM, hd), lambda i: (lat_tile(i), 0)),
            pl.BlockSpec((TM, LANES), lambda i: (lat_tile(i), 0)),
            pl.BlockSpec(memory_space=pl.ANY),
            pl.BlockSpec(memory_space=pl.ANY),
        ],
        out_specs=(pl.BlockSpec((TM, hd), lambda i: (i, 0)),
                   pl.BlockSpec((TM, hd), lambda i: (i, 0)),
                   pl.BlockSpec((TM, KV_RANK), lambda i: (i, 0)),
                   pl.BlockSpec((TM, LANES), lambda i: (i, 0)),
                   leaf_spec(KV_RANK),
                   leaf_spec(ROPE_DIM)),
        input_output_aliases={12: 4, 13: 5},
        compiler_params=_params(48),
        name=f"mla_qkv{layer}",
    )(x, mod, g_mix, wdq, gq, wdkv, gkv, wkr, wuqn, wuqr, tabq, tabk, ckv_leaf, kr_leaf)


HEAD_K = 2 * LANES
CTX_SEQS_PER_STEP = 2


def _attn_kernel(*refs, n_new, n_cache, seqs, chunk):
    n_keys = n_new + n_cache
    if n_cache:
        (qn_ref, qr_ref, ckv_ref, kr_ref, cckv_ref, ckr_ref, wuk_ref, wuv_ref, o_ref, kcat, vs) = refs
    else:
        (qn_ref, qr_ref, ckv_ref, kr_ref, wuk_ref, wuv_ref, o_ref, kcat, vs) = refs
        cckv_ref = ckr_ref = None

    @pl.when(pl.program_id(1) == 0)
    def _():
        def expand(src_ref, kr_src_ref, n_rows, base):
            for c in range(n_rows // chunk):
                rows = slice(c * chunk, (c + 1) * chunk)
                dst = slice(base + c * chunk, base + (c + 1) * chunk)
                lat = src_ref[rows, :].astype(BF16)
                kn = jnp.dot(lat, wuk_ref[...], preferred_element_type=F32).astype(BF16)
                vs[dst, :] = jnp.dot(lat, wuv_ref[...], preferred_element_type=F32).astype(BF16)
                krb = kr_src_ref[rows, :].astype(BF16)
                for hh in range(N_HEADS):
                    kcat[dst, hh * HEAD_K:hh * HEAD_K + LANES] = kn[:, hh * LANES:(hh + 1) * LANES]
                    kcat[dst, hh * HEAD_K + LANES:(hh + 1) * HEAD_K] = krb

        expand(ckv_ref, kr_ref, seqs * n_new, 0)
        if n_cache:
            expand(cckv_ref, ckr_ref, n_cache, n_new)

    for sq in range(seqs):
        qrows = slice(sq * TM, (sq + 1) * TM)
        krows = slice(sq * n_keys, (sq + 1) * n_keys)
        for hh in range(N_HEADS):
            hs = slice(hh * LANES, (hh + 1) * LANES)
            q = jnp.concatenate([qn_ref[qrows, hs], qr_ref[qrows, hs]], axis=1)
            s = lax.dot_general(q, kcat[krows, hh * HEAD_K:(hh + 1) * HEAD_K], (((1,), (1,)), ((), ())),
                                preferred_element_type=F32)
            p = jnp.exp(s - jnp.max(s, axis=1, keepdims=True))
            den = jnp.sum(p, axis=1, keepdims=True)
            o = jnp.dot(p.astype(BF16), vs[krows, hs], preferred_element_type=F32) / den
            o_ref[qrows, hs] = o.astype(BF16)


def _attn_call(layer, qn, qr, ckv, kr, wuk, wuv, cache=None):
    hd = N_HEADS * LANES
    if cache is None:
        seqs = CTX_SEQS_PER_STEP
        n_b, n_q, n_new, n_cache, tile0 = BATCH // seqs, SEQ // TM, SEQ, 0, 0
        chunk = seqs * SEQ
    else:
        seqs = 1
        n_b, n_q, n_new, n_cache, tile0 = DEC_BATCH, DEC_SEQ // TM, DEC_SEQ, PAST_LEN, CTX_TILES
        chunk = TM
    q_rows, kv_rows = seqs * TM, seqs * n_new
    q_blk0, seq_blk0 = tile0 * TM // q_rows, tile0 * TM // kv_rows
    m = layer // N_MIXERS

    def q_map(b, j):
        return (q_blk0 + b * n_q + j, 0)

    in_specs = [
        pl.BlockSpec((q_rows, hd), q_map),
        pl.BlockSpec((q_rows, hd), q_map),
        pl.BlockSpec((kv_rows, KV_RANK), lambda b, j: (seq_blk0 + b, 0)),
        pl.BlockSpec((kv_rows, LANES), lambda b, j: (seq_blk0 + b, 0)),
    ]
    args = [qn, qr, ckv, kr]
    if cache is not None:
        in_specs += [
            pl.BlockSpec((None, None, n_cache, KV_RANK), lambda b, j: (b, m, 0, 0)),
            pl.BlockSpec((None, None, n_cache, LANES), lambda b, j: (b, m, 0, 0)),
        ]
        args += list(cache)
    in_specs += [
        pl.BlockSpec((KV_RANK, hd), lambda b, j: (0, 0)),
        pl.BlockSpec((KV_RANK, hd), lambda b, j: (0, 0)),
    ]
    args += [wuk, wuv]
    key_rows = seqs * (n_new + n_cache)
    return pl.pallas_call(
        functools.partial(_attn_kernel, n_new=n_new, n_cache=n_cache, seqs=seqs, chunk=chunk),
        out_shape=jax.ShapeDtypeStruct((n_b * n_q * q_rows, hd), BF16),
        grid=(n_b, n_q),
        in_specs=in_specs,
        out_specs=pl.BlockSpec((q_rows, hd), lambda b, j: (b * n_q + j, 0)),
        scratch_shapes=[pltpu.VMEM((key_rows, N_HEADS * HEAD_K), BF16), pltpu.VMEM((key_rows, hd), BF16)],
        compiler_params=_params(60, 2),
        name=f"mla_attn{layer}_{'lat' if cache is not None else 'ctx'}",
    )(*args)


def _attn_out_kernel(octx_ref, olat_ref, wo_ref, x_ref, mod_ref, gffn_ref, wr_ref, br_ref, xnew_ref, h2_ref,
                     route_ref, cnt_ref, seen_ref):
    o = jnp.where(pl.program_id(0) >= CTX_TILES, olat_ref[...], octx_ref[...])
    mix = jnp.dot(o, wo_ref[...], preferred_element_type=F32)
    _mixer_tail(x_ref[...], mix, mod_ref, gffn_ref, wr_ref, br_ref, xnew_ref, h2_ref, route_ref, cnt_ref,
                seen_ref)


def _attn_out_call(layer, o_ctx, o_lat, wo, x, mod, g_ffn, wr, br):
    hd = N_HEADS * V_DIM
    return pl.pallas_call(
        _attn_out_kernel,
        out_shape=_tail_out_shapes(),
        grid=(N_TILES,),
        in_specs=[
            pl.BlockSpec((TM, hd), lambda i: (jnp.minimum(i, CTX_TILES - 1), 0)),
            pl.BlockSpec((TM, hd), lambda i: (jnp.maximum(i - CTX_TILES, 0), 0)),
            pl.BlockSpec((hd, D_MODEL), lambda i: (0, 0)),
            pl.BlockSpec((TM, D_MODEL), lambda i: (i, 0)),
            _mod_spec(layer),
            _row_spec(D_MODEL),
            pl.BlockSpec((D_MODEL, ROUTE_LANES), lambda i: (0, 0)),
            _row_spec(ROUTE_LANES),
        ],
        out_specs=_tail_out_specs(),
        scratch_shapes=_tail_scratch(),
        compiler_params=_params(48),
        name=f"mla_out{layer}",
    )(o_ctx, o_lat, wo, x, mod, g_ffn, wr, br)


def _plan_kernel(cnt_ref, route_ref, pos_ref, wt_ref, we_ref, wlo_ref, whi_ref, nw_ref, start_ref):
    @pl.when(pl.program_id(0) == 0)
    def _():
        lane = lax.broadcasted_iota(jnp.int32, (1, ROUTE_LANES), 1)
        shift = TM.bit_length() - 1

        def per_expert(e, carry):
            n_work, lo, last_e, starts = carry
            cnt = cnt_ref[0, e]
            hi = lo + cnt
            first_tile = lax.shift_right_logical(lo, shift)
            end_tile = jnp.where(cnt > 0, lax.shift_right_logical(hi + (TM - 1), shift), first_tile)

            def per_tile(t, w):
                wt_ref[w] = t
                we_ref[w] = e
                wlo_ref[w] = lo
                whi_ref[w] = hi
                return w + 1

            n_work = lax.fori_loop(first_tile, end_tile, per_tile, n_work)
            starts = jnp.where(lane == e, lo.astype(F32), starts)
            return n_work, hi, jnp.where(cnt > 0, e, last_e), starts

        n_work, _, last_e, starts = lax.fori_loop(
            0, N_EXPERTS, per_expert,
            (jnp.int32(0), jnp.int32(0), jnp.int32(0), jnp.zeros((1, ROUTE_LANES), F32)))
        nw_ref[0] = n_work
        start_ref[...] = starts

        def fill(w, carry):
            wt_ref[w] = SORT_TILES - 1
            we_ref[w] = last_e
            wlo_ref[w] = 0
            whi_ref[w] = 0
            return carry

        lax.fori_loop(n_work, MAX_WORK, fill, 0)

    r = route_ref[...]
    lane = lax.broadcasted_iota(jnp.int32, r.shape, 1).astype(F32)
    starts = start_ref[...]
    pos1 = jnp.sum(jnp.where(lane == r[:, 0:1], starts, 0.0), axis=1, keepdims=True) + r[:, 4:5]
    pos2 = jnp.sum(jnp.where(lane == r[:, 1:2], starts, 0.0), axis=1, keepdims=True) + r[:, 5:6]
    pos = jnp.where(lane == 0.0, pos1, jnp.where(lane == 1.0, pos2, 0.0))
    pos_ref[...] = pos.T[0:SUBLANES, :].astype(jnp.int32)


PLAN_ROWS = 4 * TM


def _plan_call(layer, counts, route):
    smem = pl.BlockSpec(memory_space=pltpu.SMEM)
    work = jax.ShapeDtypeStruct((MAX_WORK,), jnp.int32)
    return pl.pallas_call(
        _plan_kernel,
        out_shape=(jax.ShapeDtypeStruct((SUBLANES, N_TOK), jnp.int32), work, work, work, work,
                   jax.ShapeDtypeStruct((1,), jnp.int32)),
        grid=(N_TOK // PLAN_ROWS,),
        in_specs=[smem, pl.BlockSpec((PLAN_ROWS, ROUTE_LANES), lambda i: (i, 0))],
        out_specs=(pl.BlockSpec((SUBLANES, PLAN_ROWS), lambda i: (0, i)), smem, smem, smem, smem, smem),
        scratch_shapes=[pltpu.VMEM((1, ROUTE_LANES), F32)],
        compiler_params=_params(16),
        name=f"moe_plan{layer}",
    )(counts, route)


INVERT_UNROLL = 16


def _invert_kernel(pos_ref, tok_ref):
    for k in range(2):
        def body(j, carry, k=k):
            t0 = j * INVERT_UNROLL
            where = [pos_ref[k * N_TOK + t0 + u] for u in range(INVERT_UNROLL)]
            for u in range(INVERT_UNROLL):
                tok_ref[where[u]] = t0 + u
            return carry

        lax.fori_loop(0, N_TOK // INVERT_UNROLL, body, 0)


def _invert_call(layer, pos):
    smem = pl.BlockSpec(memory_space=pltpu.SMEM)
    return pl.pallas_call(
        _invert_kernel,
        out_shape=jax.ShapeDtypeStruct((N_SLOTS,), jnp.int32),
        in_specs=[smem],
        out_specs=smem,
        name=f"moe_invert{layer}",
    )(pos)


def _gmm_kernel(wt_ref, we_ref, wlo_ref, whi_ref, nw_ref, tok_ref, h2_hbm, wg_hbm, wu_hbm, wd_hbm, ys_ref,
                xbuf, wg_f, wu_f, wd_f, wg_s, wu_s, wd_s, sem_x, sem, nexp_ref, *, layer):
    w = pl.program_id(0)
    n_work = nw_ref[0]
    prev = jnp.maximum(w - 1, 0)
    tile = wt_ref[w]
    new_expert = (w == 0) | (we_ref[w] != we_ref[prev])
    new_tile = (w == 0) | (tile != wt_ref[prev])

    def weight_copies(expert, slot):
        return [pltpu.make_async_copy(hbm.at[layer, expert], buf.at[slot], sem.at[slot, k])
                for k, (hbm, buf) in enumerate(((wg_hbm, wg_f), (wu_hbm, wu_f), (wd_hbm, wd_f)))]

    def gather(t):
        slot = t % 2
        for r in range(TM):
            pltpu.make_async_copy(h2_hbm.at[pl.ds(tok_ref[t * TM + r], 1)], xbuf.at[slot, pl.ds(r, 1)],
                                  sem_x.at[slot]).start()

    @pl.when(w == 0)
    def _():
        nexp_ref[0] = 0
        for cp in weight_copies(we_ref[0], 0):
            cp.start()
        gather(tile)

    @pl.when((w < n_work) & new_tile & (tile + 1 < SORT_TILES))
    def _():
        gather(tile + 1)

    @pl.when((w < n_work) & new_expert)
    def _():
        slot = nexp_ref[0] % 2
        nexp_ref[0] += 1
        nxt = lax.while_loop(
            lambda k: (k < n_work) & (we_ref[jnp.minimum(k, MAX_WORK - 1)] == we_ref[w]),
            lambda k: k + 1, w + 1)

        @pl.when(nxt < n_work)
        def _():
            for cp in weight_copies(we_ref[jnp.minimum(nxt, MAX_WORK - 1)], 1 - slot):
                cp.start()

        for cp in weight_copies(we_ref[w], slot):
            cp.wait()
        wg_s[...] = wg_f[slot].astype(BF16)
        wu_s[...] = wu_f[slot].astype(BF16)
        wd_s[...] = wd_f[slot].astype(BF16)

    @pl.when((w < n_work) & new_tile)
    def _():
        slot = tile % 2
        pltpu.make_async_copy(h2_hbm.at[pl.ds(0, TM)], xbuf.at[slot], sem_x.at[slot]).wait()

    @pl.when(w < n_work)
    def _():
        x = xbuf[tile % 2].astype(BF16)
        gate = jnp.dot(x, wg_s[...], preferred_element_type=F32)
        up = jnp.dot(x, wu_s[...], preferred_element_type=F32)
        row = tile * TM + lax.broadcasted_iota(jnp.int32, (TM, 1), 0)
        mine = (row >= wlo_ref[w]) & (row < whi_ref[w])
        hid = jnp.where(mine, (gate * jax.nn.sigmoid(gate)) * up, 0.0)
        y = jnp.dot(hid.astype(BF16), wd_s[...], preferred_element_type=F32)

        @pl.when(new_tile)
        def _():
            ys_ref[...] = y

        @pl.when(jnp.logical_not(new_tile))
        def _():
            ys_ref[...] += y


def _gmm_call(layer, h2, work, slot_tok, w_gate, w_up, w_down):
    up_shape, down_shape = (D_MODEL, EXPERT_HIDDEN), (EXPERT_HIDDEN, D_MODEL)
    hbm = pl.BlockSpec(memory_space=pl.ANY)
    grid_spec = pltpu.PrefetchScalarGridSpec(
        num_scalar_prefetch=6,
        grid=(MAX_WORK,),
        in_specs=[hbm, hbm, hbm, hbm],
        out_specs=pl.BlockSpec((TM, D_MODEL), lambda w, wt, we, wlo, whi, nw, tok: (wt[w], 0)),
        scratch_shapes=[
            pltpu.VMEM((2, TM, D_MODEL), F32),
            pltpu.VMEM((2,) + up_shape, F32),
            pltpu.VMEM((2,) + up_shape, F32),
            pltpu.VMEM((2,) + down_shape, F32),
            pltpu.VMEM(up_shape, BF16),
            pltpu.VMEM(up_shape, BF16),
            pltpu.VMEM(down_shape, BF16),
            pltpu.SemaphoreType.DMA((2,)),
            pltpu.SemaphoreType.DMA((2, 3)),
            pltpu.SMEM((1,), jnp.int32),
        ],
    )
    return pl.pallas_call(
        functools.partial(_gmm_kernel, layer=layer),
        out_shape=jax.ShapeDtypeStruct((N_SLOTS, D_MODEL), F32),
        grid_spec=grid_spec,
        compiler_params=_params(48),
        name=f"moe_gmm{layer}",
    )(*work, slot_tok, h2, w_gate, w_up, w_down)


def _combine_kernel(pos_ref, x_ref, route_ref, mod_ref, gfin_ref, ys_hbm, *rest, final):
    if final:
        octx_ref, olat_ref, buf1, buf2, sem1, sem2 = rest
    else:
        o_ref, buf1, buf2, sem1, sem2 = rest
    i = pl.program_id(0)

    def fetch(tile, slot):
        def issue(r, carry):
            t = tile * TM + r
            pltpu.make_async_copy(ys_hbm.at[pl.ds(pos_ref[t], 1)], buf1.at[slot, pl.ds(r, 1)],
                                  sem1.at[slot]).start(priority=0)
            pltpu.make_async_copy(ys_hbm.at[pl.ds(pos_ref[N_TOK + t], 1)], buf2.at[slot, pl.ds(r, 1)],
                                  sem2.at[slot]).start(priority=1)
            return carry

        for r in range(TM):
            issue(r, 0)

    def wait_rows(slot):
        pltpu.make_async_copy(ys_hbm.at[pl.ds(0, TM)], buf1.at[slot], sem1.at[slot]).wait()
        pltpu.make_async_copy(ys_hbm.at[pl.ds(0, TM)], buf2.at[slot], sem2.at[slot]).wait()

    @pl.when(i == 0)
    def _():
        fetch(i, 0)

    slot = i % 2

    @pl.when(i + 1 < N_TILES)
    def _():
        fetch(i + 1, 1 - slot)

    wait_rows(slot)
    r = route_ref[...]
    moe = r[:, 2:3] * buf1[slot] + r[:, 3:4] * buf2[slot]
    x = x_ref[...] + mod_ref[5:6, :] * moe
    if final:
        y = x * lax.rsqrt(jnp.mean(x * x, axis=-1, keepdims=True) + EPS) * gfin_ref[...]

        @pl.when(i < CTX_TILES)
        def _():
            octx_ref[...] = y

        @pl.when(i >= CTX_TILES)
        def _():
            olat_ref[...] = y
    else:
        o_ref[...] = x


def _combine_call(layer, pos, x_new, route, ys, mod, g_final, final):
    def tok_spec(width):
        return pl.BlockSpec((TM, width), lambda i, pos: (i, 0))

    if final:
        out_shape = (jax.ShapeDtypeStruct((N_CTX, D_MODEL), F32), jax.ShapeDtypeStruct((N_LAT, D_MODEL), F32))
        out_specs = (pl.BlockSpec((TM, D_MODEL), lambda i, pos: (jnp.minimum(i, CTX_TILES - 1), 0)),
                     pl.BlockSpec((TM, D_MODEL), lambda i, pos: (jnp.maximum(i - CTX_TILES, 0), 0)))
    else:
        out_shape = jax.ShapeDtypeStruct((N_TOK, D_MODEL), F32)
        out_specs = tok_spec(D_MODEL)

    grid_spec = pltpu.PrefetchScalarGridSpec(
        num_scalar_prefetch=1,
        grid=(N_TILES,),
        in_specs=[
            tok_spec(D_MODEL),
            tok_spec(ROUTE_LANES),
            pl.BlockSpec((None, None, N_MOD, D_MODEL), lambda i, pos: (layer, _mod_row(i), 0, 0)),
            pl.BlockSpec((1, D_MODEL), lambda i, pos: (0, 0)),
            pl.BlockSpec(memory_space=pl.ANY),
        ],
        out_specs=out_specs,
        scratch_shapes=[
            pltpu.VMEM((2, TM, D_MODEL), F32),
            pltpu.VMEM((2, TM, D_MODEL), F32),
            pltpu.SemaphoreType.DMA((2,)),
            pltpu.SemaphoreType.DMA((2,)),
        ],
    )
    return pl.pallas_call(
        functools.partial(_combine_kernel, final=final),
        out_shape=out_shape,
        grid_spec=grid_spec,
        compiler_params=_params(40),
        name=f"moe_combine{layer}",
    )(pos, x_new, route, mod, g_final, ys)


def _rope_tables():
    rows = DEC_SEQ // GRID_W
    row_ids = np.repeat(np.arange(rows), GRID_W).astype(np.float32)
    col_ids = np.tile(np.arange(GRID_W), rows).astype(np.float32)
    freqs = (np.float32(ROPE_THETA) ** (-np.arange(ROPE_QUARTER, dtype=np.float32) / ROPE_QUARTER))
    ar = (row_ids[:, None] * freqs[None, :]).astype(np.float32).astype(np.float64)
    ac = (col_ids[:, None] * freqs[None, :]).astype(np.float32).astype(np.float64)
    cos = np.concatenate([np.cos(ar), np.cos(ar), np.cos(ac), np.cos(ac)], axis=1)
    sin = np.concatenate([-np.sin(ar), np.sin(ar), -np.sin(ac), np.sin(ac)], axis=1)
    return np.concatenate([cos, sin], axis=1).astype(np.float32)


def _swap_partners(w):
    q = ROPE_QUARTER
    return jnp.concatenate([w[..., q:2 * q], w[..., 0:q], w[..., 3 * q:4 * q], w[..., 2 * q:3 * q]], axis=-1)


def kernel(x_prompt, x_sample, cache_ckv, cache_krope, c, c_ctx, g_mix, g_ffn, w_ada, b_ada, w_pool, pool_scale,
           w_dq, g_q, w_uq, w_dkv, g_kv, w_uk, w_uv, w_o, w_router_grp, b_router_grp, w_router_exp,
           b_router_exp, w_gate, w_up, w_down, g_final):
    n_mla = DEPTH // N_MIXERS
    x = (x_prompt.reshape(N_CTX, D_MODEL), x_sample.reshape(N_LAT, D_MODEL))

    cond = jnp.concatenate([c_ctx[None, :], c, jnp.zeros((COND_ROWS - 1 - DEC_BATCH, D_MODEL), F32)], axis=0)
    mod = _ada_call(cond, w_ada, b_ada).reshape(DEPTH, COND_ROWS, N_MOD, D_MODEL)

    wr = jnp.concatenate([w_router_exp, w_router_grp,
                          jnp.zeros((DEPTH, D_MODEL, ROUTE_LANES - N_EXPERTS - N_EXPERT_GROUPS), F32)],
                         axis=2).astype(BF16)
    br = jnp.concatenate([b_router_exp, b_router_grp,
                          jnp.zeros((DEPTH, ROUTE_LANES - N_EXPERTS - N_EXPERT_GROUPS), F32)], axis=1)
    band = jnp.asarray(_pool_band(), dtype=BF16)
    wpool = w_pool.astype(BF16)
    wdq = w_dq.astype(BF16)
    wdkv = w_dkv[:, :, :KV_RANK].astype(BF16)
    w_kr = w_dkv[:, :, KV_RANK:]
    wkr = jnp.concatenate([w_kr, _swap_partners(w_kr)], axis=-1).astype(BF16)
    wuqn = w_uq[..., :NOPE_DIM].reshape(n_mla, Q_RANK, N_HEADS * NOPE_DIM).astype(BF16)
    w_qr = w_uq[..., NOPE_DIM:]
    wuqr = jnp.concatenate([w_qr, _swap_partners(w_qr)], axis=-1).reshape(n_mla, Q_RANK, N_HEADS * LANES)
    wuqr = wuqr.astype(BF16)
    wuk = w_uk.reshape(n_mla, KV_RANK, N_HEADS * NOPE_DIM).astype(BF16)
    wuv = w_uv.reshape(n_mla, KV_RANK, N_HEADS * V_DIM).astype(BF16)
    wo = w_o.astype(BF16)
    rope_tab = _rope_tables()
    tabk = jnp.asarray(rope_tab)
    tabq = jnp.asarray(np.tile(rope_tab, (1, N_HEADS)))
    cache_kr = jnp.pad(cache_krope, ((0, 0), (0, 0), (0, 0), (0, LANES - ROPE_DIM)))

    def row(a):
        return a.reshape(1, -1)

    ctx_ckv = jnp.zeros((BATCH, n_mla, SEQ, KV_RANK), F32)
    ctx_krope = jnp.zeros((BATCH, n_mla, SEQ, ROPE_DIM), F32)
    for layer in range(DEPTH):
        if layer % N_MIXERS == 0:
            p = layer // N_MIXERS
            streams = x if isinstance(x, tuple) else (x,)
            x_new, h2, route, counts = _pool_call(layer, streams, mod, row(g_mix[layer]), row(g_ffn[layer]),
                                                  band, wpool[p], row(pool_scale[p]), wr[layer], row(br[layer]))
        else:
            m = layer // N_MIXERS
            qn, qr, ckv, kr, ctx_ckv, ctx_krope = _qkv_call(
                layer, x, mod, row(g_mix[layer]), wdq[m], row(g_q[m]), wdkv[m], row(g_kv[m]), wkr[m], wuqn[m],
                wuqr[m], tabq, tabk, ctx_ckv, ctx_krope)
            o_ctx = _attn_call(layer, qn, qr, ckv, kr, wuk[m], wuv[m])
            o_lat = _attn_call(layer, qn, qr, ckv, kr, wuk[m], wuv[m], cache=(cache_ckv, cache_kr))
            x_new, h2, route, counts = _attn_out_call(layer, o_ctx, o_lat, wo[m], x, mod, row(g_ffn[layer]),
                                                      wr[layer], row(br[layer]))
        pos8, *work = _plan_call(layer, counts, route)
        pos = pos8[:2].reshape(N_SLOTS)
        ys = _gmm_call(layer, h2, work, _invert_call(layer, pos), w_gate, w_up, w_down)
        x = _combine_call(layer, pos, x_new, route, ys, mod, row(g_final), final=(layer == DEPTH - 1))

    y_prompt = x[0].reshape(BATCH, SEQ, D_MODEL)
    y_sample = x[1].reshape(DEC_BATCH, DEC_SEQ, D_MODEL)
    return y_prompt, y_sample, ctx_ckv, ctx_krope
```
